```python
import math
import jax, jax.numpy as jnp
from jax import lax
import numpy as np

D_MODEL = 2048
BATCH = 2
SEQ = 16384
DEPTH = 2

SGU_GROUPS = 8
SGU_WIDTH = 1024
SGU_GROUP_DIM = SGU_WIDTH // SGU_GROUPS
CHUNK = 128
DIFF_HEADS = 8
DIFF_HEAD_DIM = 64
DIFF_V_DIM = 2 * DIFF_HEAD_DIM
DIFF_QK_WIDTH = DIFF_HEADS * 2 * DIFF_HEAD_DIM
DIFF_WIDTH = DIFF_HEADS * DIFF_V_DIM
Q_BLOCK = 128
N_BRANCHES = 2
IN_COLS = 2 * SGU_WIDTH + 2 * DIFF_QK_WIDTH + DIFF_WIDTH + N_BRANCHES * D_MODEL
N_GROUPS = 4
EXPERTS_PER_GROUP = 4
N_EXPERTS = N_GROUPS * EXPERTS_PER_GROUP
EXPERT_FF = 512
TOP_K_IN_GROUP = 2

RMS_EPS = 1e-6
LN_EPS = 1e-5

kernel_name = "hybrid_sgu_diffattn_hmoe_encoder"


def rmsnorm(x, g):
    xf = x.astype(jnp.float32)
    y = xf * lax.rsqrt(jnp.mean(xf * xf, axis=-1, keepdims=True) + RMS_EPS)
    return (y * g.astype(jnp.float32)).astype(x.dtype)


def layernorm(x, g, b):
    xf = x.astype(jnp.float32)
    mu = jnp.mean(xf, axis=-1, keepdims=True)
    var = jnp.mean(jnp.square(xf - mu), axis=-1, keepdims=True)
    y = (xf - mu) * lax.rsqrt(var + LN_EPS)
    return (y * g.astype(jnp.float32) + b.astype(jnp.float32)).astype(x.dtype)


def alibi_slopes(n_heads):
    return jnp.exp2(-8.0 * jnp.arange(1, n_heads + 1, dtype=jnp.float32) / n_heads)


def spatial_gating(u, v, ln_g, ln_b, w_s, b_s):
    B, S, _ = v.shape
    u = jax.nn.gelu(u)
    v = layernorm(jax.nn.gelu(v), ln_g, ln_b)
    vc = v.reshape(B, S // CHUNK, CHUNK, SGU_GROUPS, SGU_GROUP_DIM)
    mixed = jnp.einsum('gts,bnsgc->bntgc', w_s, vc) + b_s.T[None, None, :, :, None]
    return u * mixed.reshape(B, S, SGU_WIDTH)


def diff_attention(q, k, v, lam, slopes):
    B, S = q.shape[:2]
    nb = S // Q_BLOCK
    scale = DIFF_HEAD_DIM ** -0.5
    qb = q.reshape(B, nb, Q_BLOCK, DIFF_HEADS, 2, DIFF_HEAD_DIM).transpose(1, 0, 2, 3, 4, 5)
    starts = jnp.arange(nb, dtype=jnp.int32) * Q_BLOCK
    k_pos = jnp.arange(S, dtype=jnp.int32)

    def block(args):
        q_blk, start = args
        q_pos = start + jnp.arange(Q_BLOCK, dtype=jnp.int32)
        dist = jnp.abs(q_pos[:, None] - k_pos[None, :]).astype(jnp.float32)
        bias = -slopes[:, None, None] * dist[None]
        s = jnp.einsum('bqhmd,bkhmd->mbhqk', q_blk, k,
                       preferred_element_type=jnp.float32) * scale + bias
        p = jax.nn.softmax(s, axis=-1)
        w = p[0] - lam * p[1]
        return jnp.einsum('bhqk,bkhe->bqhe', w.astype(v.dtype), v)

    out = lax.map(block, (qb, starts))
    return out.transpose(1, 0, 2, 3, 4).reshape(B, S, DIFF_HEADS, DIFF_V_DIM)


def hier_moe(h, rg_w, rg_b, re_w, re_b, w1, w3, w2):
    B, S, D = h.shape
    t = h.reshape(B * S, D)
    g_logits = (t @ rg_w).astype(jnp.float32) + rg_b.astype(jnp.float32)
    g_prob = jax.nn.softmax(g_logits, axis=-1)
    g_w, g_idx = lax.top_k(g_prob, 1)
    e_logits = ((t @ re_w).astype(jnp.float32) + re_b.astype(jnp.float32)
                ).reshape(-1, N_GROUPS, EXPERTS_PER_GROUP)
    g_onehot = jax.nn.one_hot(g_idx[:, 0], N_GROUPS, dtype=jnp.float32)
    e_sel = jnp.einsum('tg,tge->te', g_onehot, e_logits)
    e_prob = jax.nn.softmax(e_sel, axis=-1)
    e_w, e_idx = lax.top_k(e_prob, TOP_K_IN_GROUP)
    e_w = e_w / jnp.sum(e_w, axis=-1, keepdims=True)
    weights = g_w * e_w
    experts = g_idx * EXPERTS_PER_GROUP + e_idx
    combine = jnp.einsum('tk,tke->et', weights,
                         jax.nn.one_hot(experts, N_EXPERTS, dtype=jnp.float32)).astype(t.dtype)

    def body(y, xs):
        w1_e, w3_e, w2_e, c_e = xs
        hid = jax.nn.silu(t @ w1_e) * (t @ w3_e)
        return y + c_e[:, None] * (hid @ w2_e), None

    y, _ = lax.scan(body, jnp.zeros_like(t), (w1, w3, w2, combine))
    return y.reshape(B, S, D)


def setup_inputs(seed: int = 0) -> dict:
    key = jax.random.key(seed)
    ks = jax.random.split(key, 24)
    f32 = jnp.float32

    def nrm(k, shape, scale):
        return jax.random.normal(k, shape, f32) * scale

    L, D = DEPTH, D_MODEL
    return {
        "x": nrm(ks[0], (BATCH, SEQ, D), 1.0),
        "norm1_g": 1.0 + nrm(ks[1], (L, D), 0.02),
        "w_in": nrm(ks[2], (L, D, IN_COLS), D ** -0.5),
        "b_gate": nrm(ks[3], (L, N_BRANCHES, D), 0.02),
        "sgu_ln_g": 1.0 + nrm(ks[4], (L, SGU_WIDTH), 0.02),
        "sgu_ln_b": nrm(ks[5], (L, SGU_WIDTH), 0.02),
        "sgu_w": nrm(ks[6], (L, SGU_GROUPS, CHUNK, CHUNK), CHUNK ** -0.5),
        "sgu_b": 1.0 + nrm(ks[7], (L, SGU_GROUPS, CHUNK), 0.02),
        "lam_q1": nrm(ks[8], (L, DIFF_HEAD_DIM), 0.1),
        "lam_k1": nrm(ks[9], (L, DIFF_HEAD_DIM), 0.1),
        "lam_q2": nrm(ks[10], (L, DIFF_HEAD_DIM), 0.1),
        "lam_k2": nrm(ks[11], (L, DIFF_HEAD_DIM), 0.1),
        "diff_norm_g": 1.0 + nrm(ks[12], (L, DIFF_V_DIM), 0.02),
        "w_proj_a": nrm(ks[13], (L, SGU_WIDTH, D), SGU_WIDTH ** -0.5),
        "w_proj_b": nrm(ks[14], (L, DIFF_WIDTH, D), DIFF_WIDTH ** -0.5),
        "w_out": nrm(ks[15], (L, D, D), D ** -0.5),
        "norm2_g": 1.0 + nrm(ks[16], (L, D), 0.02),
        "router_g_w": nrm(ks[17], (L, D, N_GROUPS), D ** -0.5),
        "router_g_b": nrm(ks[18], (L, N_GROUPS), 0.01),
        "router_e_w": nrm(ks[19], (L, D, N_EXPERTS), D ** -0.5),
        "router_e_b": nrm(ks[20], (L, N_EXPERTS), 0.01),
        "w1": nrm(ks[21], (L, N_EXPERTS, D, EXPERT_FF), D ** -0.5),
        "w3": nrm(ks[22], (L, N_EXPERTS, D, EXPERT_FF), D ** -0.5),
        "w2": nrm(ks[23], (L, N_EXPERTS, EXPERT_FF, D), EXPERT_FF ** -0.5),
        "final_g": 1.0 + nrm(jax.random.fold_in(key, 99), (D,), 0.02),
    }


def reference(x, norm1_g, w_in, b_gate, sgu_ln_g, sgu_ln_b, sgu_w, sgu_b,
              lam_q1, lam_k1, lam_q2, lam_k2, diff_norm_g, w_proj_a, w_proj_b,
              w_out, norm2_g, router_g_w, router_g_b, router_e_w, router_e_b,
              w1, w3, w2, final_g):
    B, S, D = x.shape
    slopes = alibi_slopes(DIFF_HEADS)
    splits = [SGU_WIDTH, 2 * SGU_WIDTH,
              2 * SGU_WIDTH + DIFF_QK_WIDTH,
              2 * SGU_WIDTH + 2 * DIFF_QK_WIDTH,
              2 * SGU_WIDTH + 2 * DIFF_QK_WIDTH + DIFF_WIDTH]
    for l in range(DEPTH):
        h = rmsnorm(x, norm1_g[l])
        z = h @ w_in[l]
        u_a, v_a, q, k, v_b, gate_logits = jnp.split(z, splits, axis=-1)

        a = spatial_gating(u_a, v_a, sgu_ln_g[l], sgu_ln_b[l], sgu_w[l], sgu_b[l])

        lam_init = 0.8 - 0.6 * math.exp(-0.3 * l)
        lam = (jnp.exp(jnp.sum(lam_q1[l].astype(jnp.float32) * lam_k1[l].astype(jnp.float32)))
               - jnp.exp(jnp.sum(lam_q2[l].astype(jnp.float32) * lam_k2[l].astype(jnp.float32)))
               + lam_init)
        qh = q.reshape(B, S, DIFF_HEADS, 2, DIFF_HEAD_DIM)
        kh = k.reshape(B, S, DIFF_HEADS, 2, DIFF_HEAD_DIM)
        vh = v_b.reshape(B, S, DIFF_HEADS, DIFF_V_DIM)
        o = diff_attention(qh, kh, vh, lam, slopes)
        o = rmsnorm(o, diff_norm_g[l]) * (1.0 - lam_init)
        b_out = o.reshape(B, S, DIFF_WIDTH)

        gates = jax.nn.sigmoid(gate_logits.reshape(B, S, N_BRANCHES, D) + b_gate[l])
        merged = gates[:, :, 0] * (a @ w_proj_a[l]) + gates[:, :, 1] * (b_out @ w_proj_b[l])
        x = x + merged @ w_out[l]

        h2 = rmsnorm(x, norm2_g[l])
        x = x + hier_moe(h2, router_g_w[l], router_g_b[l], router_e_w[l], router_e_b[l],
                         w1[l], w3[l], w2[l])
    return rmsnorm(x, final_g)
```

```python
import functools
import math

import jax
import jax.numpy as jnp
from jax import lax
from jax.experimental import pallas as pl
from jax.experimental.pallas import tpu as pltpu

F32 = jnp.float32
BF16 = jnp.bfloat16

D_MODEL = 2048
SGU_GROUPS = 8
SGU_WIDTH = 1024
SGU_GROUP_DIM = SGU_WIDTH // SGU_GROUPS
CHUNK = 128
DIFF_HEADS = 8
DIFF_HEAD_DIM = 64
DIFF_V_DIM = 2 * DIFF_HEAD_DIM
DIFF_QK_WIDTH = DIFF_HEADS * 2 * DIFF_HEAD_DIM
DIFF_WIDTH = DIFF_HEADS * DIFF_V_DIM
N_BRANCHES = 2
IN_COLS = 2 * SGU_WIDTH + 2 * DIFF_QK_WIDTH + DIFF_WIDTH + N_BRANCHES * D_MODEL
N_GROUPS = 4
EXPERTS_PER_GROUP = 4
N_EXPERTS = N_GROUPS * EXPERTS_PER_GROUP
EXPERT_FF = 512
RMS_EPS = 1e-6
LN_EPS = 1e-5

Q_COL128 = (2 * SGU_WIDTH) // 128
K_COL128 = (2 * SGU_WIDTH + DIFF_QK_WIDTH) // 128
V_COL128 = (2 * SGU_WIDTH + 2 * DIFF_QK_WIDTH) // 128
GATE_COL1024 = (2 * SGU_WIDTH + 2 * DIFF_QK_WIDTH + DIFF_WIDTH) // 1024

LANES = 128
ROUTER_COLS = LANES
VMEM_LIMIT = 56 * 1024 * 1024


def _params(semantics):
    return pltpu.CompilerParams(dimension_semantics=semantics, vmem_limit_bytes=VMEM_LIMIT)


def _gelu(x):
    return 0.5 * x * (1.0 + jnp.tanh(0.7978845608028654 * (x + 0.044715 * (x * x * x))))


def _rms_scale(x):
    return lax.rsqrt(jnp.mean(x * x, axis=-1, keepdims=True) + RMS_EPS)


def _inproj_kernel(x_ref, g_ref, w_ref, z_ref, h_ref):
    @pl.when(pl.program_id(1) == 0)
    def _():
        x = x_ref[...]
        h_ref[...] = (x * _rms_scale(x) * g_ref[...]).astype(BF16)

    z_ref[...] = jnp.dot(h_ref[...], w_ref[...], preferred_element_type=F32).astype(BF16)


def _inproj(x2, g, w_bf16, tm, tn):
    T, D = x2.shape
    N = w_bf16.shape[1]
    return pl.pallas_call(
        _inproj_kernel,
        grid=(T // tm, N // tn),
        in_specs=[
            pl.BlockSpec((tm, D), lambda i, j: (i, 0)),
            pl.BlockSpec((1, D), lambda i, j: (0, 0)),
            pl.BlockSpec((D, tn), lambda i, j: (0, j)),
        ],
        out_specs=pl.BlockSpec((tm, tn), lambda i, j: (i, j)),
        out_shape=jax.ShapeDtypeStruct((T, N), BF16),
        scratch_shapes=[pltpu.VMEM((tm, D), BF16)],
        compiler_params=_params(("parallel", "arbitrary")),
        name="inproj",
    )(x2, g, w_bf16)


def _sgu_kernel(u_ref, v_ref, lng_ref, lnb_ref, ws_ref, bs_ref, a_ref, *, chunks):
    v = _gelu(v_ref[...].astype(F32))
    mu = jnp.mean(v, axis=-1, keepdims=True)
    vc = v - mu
    var = jnp.mean(vc * vc, axis=-1, keepdims=True)
    vn = (vc * lax.rsqrt(var + LN_EPS) * lng_ref[...] + lnb_ref[...]).astype(BF16)
    for c in range(chunks):
        rows = slice(c * CHUNK, (c + 1) * CHUNK)
        for g in range(SGU_GROUPS):
            cols = slice(g * SGU_GROUP_DIM, (g + 1) * SGU_GROUP_DIM)
            mixed = jnp.dot(ws_ref[g], vn[rows, cols], preferred_element_type=F32)
            mixed = mixed + bs_ref[g]
            u = _gelu(u_ref[rows, cols].astype(F32))
            a_ref[rows, cols] = (u * mixed).astype(BF16)


def _sgu(z, ln_g, ln_b, ws_bf16, bs_col, tm):
    T = z.shape[0]
    wblk = SGU_WIDTH
    return pl.pallas_call(
        functools.partial(_sgu_kernel, chunks=tm // CHUNK),
        grid=(T // tm,),
        in_specs=[
            pl.BlockSpec((tm, wblk), lambda i: (i, 0)),
            pl.BlockSpec((tm, wblk), lambda i: (i, 1)),
            pl.BlockSpec((1, wblk), lambda i: (0, 0)),
            pl.BlockSpec((1, wblk), lambda i: (0, 0)),
            pl.BlockSpec((SGU_GROUPS, CHUNK, CHUNK), lambda i: (0, 0, 0)),
            pl.BlockSpec((SGU_GROUPS, CHUNK, 1), lambda i: (0, 0, 0)),
        ],
        out_specs=pl.BlockSpec((tm, wblk), lambda i: (i, 0)),
        out_shape=jax.ShapeDtypeStruct((T, SGU_WIDTH), BF16),
        compiler_params=_params(("parallel",)),
        name="sgu",
    )(z, z, ln_g, ln_b, ws_bf16, bs_col)


def _attn_kernel(slopes_ref, lq1_ref, lk1_ref, lq2_ref, lk2_ref, dg_ref, q_ref, k_ref, v_ref,
                 o_ref, q2_ref, m_ref, l_ref, acc_ref, rel_ref, *, tq, tk, seq, lam_init):
    h = pl.program_id(1)
    qi = pl.program_id(2)
    slope = slopes_ref[h]
    scale = DIFF_HEAD_DIM ** -0.5

    q = q_ref[...].astype(F32) * scale
    lane = lax.broadcasted_iota(jnp.int32, (tq, 2 * DIFF_HEAD_DIM), 1)
    q2_ref[0:tq, :] = jnp.where(lane < DIFF_HEAD_DIM, q, 0.0).astype(BF16)
    q2_ref[tq:2 * tq, :] = jnp.where(lane >= DIFF_HEAD_DIM, q, 0.0).astype(BF16)
    m_ref[...] = jnp.full(m_ref.shape, -jnp.inf, F32)
    l_ref[...] = jnp.zeros(l_ref.shape, F32)
    acc_ref[...] = jnp.zeros(acc_ref.shape, F32)
    rel_ref[...] = (lax.broadcasted_iota(jnp.int32, (tq, tk), 0)
                    - lax.broadcasted_iota(jnp.int32, (tq, tk), 1)).astype(F32)

    def body(j, carry):
        off = (qi * tq - j * tk).astype(F32)
        kc = k_ref[pl.ds(pl.multiple_of(j * tk, tk), tk), :]
        vc = v_ref[pl.ds(pl.multiple_of(j * tk, tk), tk), :]
        s = lax.dot_general(q2_ref[...], kc, (((1,), (1,)), ((), ())),
                            preferred_element_type=F32)
        bias = -slope * jnp.abs(rel_ref[...] + off)
        s = s + jnp.concatenate([bias, bias], axis=0)
        m_prev = m_ref[...]
        m_new = jnp.maximum(m_prev, jnp.max(s, axis=-1, keepdims=True))
        alpha = jnp.exp(m_prev - m_new)
        p = jnp.exp(s - m_new)
        l_ref[...] = alpha * l_ref[...] + jnp.sum(p, axis=-1, keepdims=True)
        acc_ref[...] = alpha * acc_ref[...] + jnp.dot(p.astype(BF16), vc,
                                                      preferred_element_type=F32)
        m_ref[...] = m_new
        return carry

    lax.fori_loop(0, seq // tk, body, 0)

    lam = (jnp.exp(jnp.sum(lq1_ref[...] * lk1_ref[...], axis=-1, keepdims=True))
           - jnp.exp(jnp.sum(lq2_ref[...] * lk2_ref[...], axis=-1, keepdims=True))
           + lam_init)
    out = acc_ref[...] / l_ref[...]
    o = out[0:tq, :] - lam * out[tq:2 * tq, :]
    o = o * _rms_scale(o) * dg_ref[...] * (1.0 - lam_init)
    o_ref[...] = o.astype(BF16)


def _attention(z, slopes, lq1, lk1, lq2, lk2, dg, batch, seq, tq, tk, lam_init):
    T = z.shape[0]
    nq = seq // tq
    hd = DIFF_V_DIM
    lam_spec = pl.BlockSpec((1, DIFF_HEAD_DIM), lambda b, h, i, s: (0, 0))
    grid_spec = pltpu.PrefetchScalarGridSpec(
        num_scalar_prefetch=1,
        grid=(batch, DIFF_HEADS, nq),
        in_specs=[
            lam_spec, lam_spec, lam_spec, lam_spec,
            pl.BlockSpec((1, hd), lambda b, h, i, s: (0, 0)),
            pl.BlockSpec((tq, hd), lambda b, h, i, s: (b * nq + i, Q_COL128 + h)),
            pl.BlockSpec((seq, hd), lambda b, h, i, s: (b, K_COL128 + h)),
            pl.BlockSpec((seq, hd), lambda b, h, i, s: (b, V_COL128 + h)),
        ],
        out_specs=pl.BlockSpec((tq, hd), lambda b, h, i, s: (b * nq + i, h)),
        scratch_shapes=[
            pltpu.VMEM((2 * tq, hd), BF16),
            pltpu.VMEM((2 * tq, 1), F32),
            pltpu.VMEM((2 * tq, 1), F32),
            pltpu.VMEM((2 * tq, hd), F32),
            pltpu.VMEM((tq, tk), F32),
        ],
    )
    return pl.pallas_call(
        functools.partial(_attn_kernel, tq=tq, tk=tk, seq=seq, lam_init=lam_init),
        grid_spec=grid_spec,
        out_shape=jax.ShapeDtypeStruct((T, DIFF_WIDTH), BF16),
        compiler_params=_params(("parallel", "parallel", "arbitrary")),
        name="diff_attn",
    )(slopes, lq1, lk1, lq2, lk2, dg, z, z, z)


def _merge_kernel(x_ref, a_ref, b_ref, g00_ref, g01_ref, g10_ref, g11_ref, bg_ref,
                  wa_ref, wb_ref, wo_ref, o_ref, merged_ref):
    gate_refs = ((g00_ref, g01_ref), (g10_ref, g11_ref))
    half = D_MODEL // 2
    a = a_ref[...]
    b = b_ref[...]
    for c in range(2):
        cols = slice(c * half, (c + 1) * half)
        ga = jax.nn.sigmoid(gate_refs[0][c][...].astype(F32) + bg_ref[0:1, cols])
        gb = jax.nn.sigmoid(gate_refs[1][c][...].astype(F32) + bg_ref[1:2, cols])
        pa = jnp.dot(a, wa_ref[:, cols], preferred_element_type=F32)
        pb = jnp.dot(b, wb_ref[:, cols], preferred_element_type=F32)
        merged_ref[:, cols] = (ga * pa + gb * pb).astype(BF16)
    o_ref[...] = x_ref[...] + jnp.dot(merged_ref[...], wo_ref[...], preferred_element_type=F32)


def _merge(x2, a, b, z, b_gate, wa, wb, wo, tm):
    T, D = x2.shape
    half = D // 2

    def gate_spec(k):
        return pl.BlockSpec((tm, half), lambda i: (i, GATE_COL1024 + k))

    def const_spec(shape):
        return pl.BlockSpec(shape, lambda i: (0, 0), pipeline_mode=pl.Buffered(1))

    return pl.pallas_call(
        _merge_kernel,
        grid=(T // tm,),
        in_specs=[
            pl.BlockSpec((tm, D), lambda i: (i, 0)),
            pl.BlockSpec((tm, SGU_WIDTH), lambda i: (i, 0)),
            pl.BlockSpec((tm, DIFF_WIDTH), lambda i: (i, 0)),
            gate_spec(0), gate_spec(1), gate_spec(2), gate_spec(3),
            const_spec((N_BRANCHES, D)),
            const_spec((SGU_WIDTH, D)),
            const_spec((DIFF_WIDTH, D)),
            const_spec((D, D)),
        ],
        out_specs=pl.BlockSpec((tm, D), lambda i: (i, 0)),
        out_shape=jax.ShapeDtypeStruct((T, D), F32),
        scratch_shapes=[pltpu.VMEM((tm, D), BF16)],
        compiler_params=_params(("parallel",)),
        name="merge_out",
    )(x2, a, b, z, z, z, z, b_gate, wa, wb, wo)


def _split_bf16(x):
    hi = x.astype(BF16)
    lo = (x - hi.astype(F32)).astype(BF16)
    return hi, lo


def _dot3(x_hi, x_lo, w_hi, w_lo):
    return (jnp.dot(x_hi, w_hi, preferred_element_type=F32)
            + jnp.dot(x_lo, w_hi, preferred_element_type=F32)
            + jnp.dot(x_hi, w_lo, preferred_element_type=F32))


def _router(h, rw_hi_ref, rw_lo_ref, rb_ref):
    tm = h.shape[0]
    h_hi, h_lo = _split_bf16(h)
    logits = _dot3(h_hi, h_lo, rw_hi_ref[...], rw_lo_ref[...]) + rb_ref[...]
    lane = lax.broadcasted_iota(jnp.int32, (tm, ROUTER_COLS), 1)
    neg = jnp.float32(-jnp.inf)
    is_group = lane < N_GROUPS
    gl = jnp.where(is_group, logits, neg)
    gmax = jnp.max(gl, axis=-1, keepdims=True)
    gexp = jnp.exp(gl - gmax)
    g_w = 1.0 / jnp.sum(gexp, axis=-1, keepdims=True)
    g_idx = jnp.min(jnp.where(gl == gmax, lane, ROUTER_COLS), axis=-1, keepdims=True)
    e_lo = N_GROUPS + g_idx * EXPERTS_PER_GROUP
    in_group = (lane >= e_lo) & (lane < e_lo + EXPERTS_PER_GROUP)
    el = jnp.where(in_group, logits, neg)
    m1 = jnp.max(el, axis=-1, keepdims=True)
    i1 = jnp.min(jnp.where(el == m1, lane, ROUTER_COLS), axis=-1, keepdims=True)
    el2 = jnp.where(lane == i1, neg, el)
    m2 = jnp.max(el2, axis=-1, keepdims=True)
    i2 = jnp.min(jnp.where(el2 == m2, lane, ROUTER_COLS), axis=-1, keepdims=True)
    t = jnp.exp(m2 - m1)
    w1 = g_w / (1.0 + t)
    w2 = g_w * t / (1.0 + t)
    comb = jnp.where(lane == i1, w1, 0.0) + jnp.where(lane == i2, w2, 0.0)
    return comb


def _moe_kernel(x_ref, g_ref, rw_hi_ref, rw_lo_ref, rb_ref, w1_ref, w3_ref, w2_ref, fg_ref,
                o_ref, h_ref, comb_ref, acc_ref, *, final_norm):
    e = pl.program_id(1)

    @pl.when(e == 0)
    def _():
        x = x_ref[...]
        h = x * _rms_scale(x) * g_ref[...]
        h_ref[...] = h.astype(BF16)
        comb_ref[...] = _router(h, rw_hi_ref, rw_lo_ref, rb_ref)
        acc_ref[...] = x

    h = h_ref[...]
    hid = jax.nn.silu(jnp.dot(h, w1_ref[0], preferred_element_type=F32)) \
        * jnp.dot(h, w3_ref[0], preferred_element_type=F32)
    lane = lax.broadcasted_iota(jnp.int32, comb_ref.shape, 1)
    c_e = jnp.sum(jnp.where(lane == e + N_GROUPS, comb_ref[...], 0.0), axis=-1, keepdims=True)
    acc_ref[...] += c_e * jnp.dot(hid.astype(BF16), w2_ref[0], preferred_element_type=F32)

    @pl.when(e == N_EXPERTS - 1)
    def _():
        y = acc_ref[...]
        if final_norm:
            y = y * _rms_scale(y) * fg_ref[...]
        o_ref[...] = y


def _moe(x2, g, rw_hi, rw_lo, rb, w1, w3, w2, final_g, tm, final_norm):
    T, D = x2.shape
    row = pl.BlockSpec((1, D), lambda i, e: (0, 0))
    rspec = pl.BlockSpec((D, ROUTER_COLS), lambda i, e: (0, 0))
    return pl.pallas_call(
        functools.partial(_moe_kernel, final_norm=final_norm),
        grid=(T // tm, N_EXPERTS),
        in_specs=[
            pl.BlockSpec((tm, D), lambda i, e: (i, 0)),
            row, rspec, rspec,
            pl.BlockSpec((1, ROUTER_COLS), lambda i, e: (0, 0)),
            pl.BlockSpec((1, D, EXPERT_FF), lambda i, e: (e, 0, 0)),
            pl.BlockSpec((1, D, EXPERT_FF), lambda i, e: (e, 0, 0)),
            pl.BlockSpec((1, EXPERT_FF, D), lambda i, e: (e, 0, 0)),
            row,
        ],
        out_specs=pl.BlockSpec((tm, D), lambda i, e: (i, 0)),
        out_shape=jax.ShapeDtypeStruct((T, D), F32),
        scratch_shapes=[
            pltpu.VMEM((tm, D), BF16),
            pltpu.VMEM((tm, ROUTER_COLS), F32),
            pltpu.VMEM((tm, D), F32),
        ],
        compiler_params=_params(("parallel", "arbitrary")),
        name="moe",
    )(x2, g, rw_hi, rw_lo, rb, w1, w3, w2, final_g)


def _router_weights(rg_w, rg_b, re_w, re_b):
    D = rg_w.shape[0]
    w = jnp.zeros((D, ROUTER_COLS), F32)
    w = w.at[:, :N_GROUPS].set(rg_w).at[:, N_GROUPS:N_GROUPS + N_EXPERTS].set(re_w)
    b = jnp.zeros((1, ROUTER_COLS), F32)
    b = b.at[0, :N_GROUPS].set(rg_b).at[0, N_GROUPS:N_GROUPS + N_EXPERTS].set(re_b)
    hi = w.astype(BF16)
    lo = (w - hi.astype(F32)).astype(BF16)
    return hi, lo, b


def kernel(x, norm1_g, w_in, b_gate, sgu_ln_g, sgu_ln_b, sgu_w, sgu_b, lam_q1, lam_k1, lam_q2,
           lam_k2, diff_norm_g, w_proj_a, w_proj_b, w_out, norm2_g, router_g_w, router_g_b,
           router_e_w, router_e_b, w1, w3, w2, final_g):
    B, S, D = x.shape
    assert D == D_MODEL and w_in.shape[2] == IN_COLS
    depth = w_in.shape[0]
    T = B * S
    slopes = jnp.exp2(-8.0 * jnp.arange(1, DIFF_HEADS + 1, dtype=F32) / DIFF_HEADS)

    tm_in = min(1024, T)
    tm_sgu = min(512, T)
    tq = min(256, S)
    tk = min(512, S)
    tm_merge = min(256, T)
    tm_moe = min(512, T)

    x2 = x.reshape(T, D)
    for l in range(depth):
        lam_init = 0.8 - 0.6 * math.exp(-0.3 * l)
        z = _inproj(x2, norm1_g[l][None], w_in[l].astype(BF16), tm_in, 512)
        a = _sgu(z, sgu_ln_g[l][None], sgu_ln_b[l][None], sgu_w[l].astype(BF16),
                 sgu_b[l][:, :, None], tm_sgu)
        o = _attention(z, slopes, lam_q1[l][None], lam_k1[l][None], lam_q2[l][None],
                       lam_k2[l][None], diff_norm_g[l][None], B, S, tq, tk, lam_init)
        x2 = _merge(x2, a, o, z, b_gate[l], w_proj_a[l].astype(BF16), w_proj_b[l].astype(BF16),
                    w_out[l].astype(BF16), tm_merge)
        rw_hi, rw_lo, rb = _router_weights(router_g_w[l], router_g_b[l], router_e_w[l],
                                           router_e_b[l])
        x2 = _moe(x2, norm2_g[l][None], rw_hi, rw_lo, rb, w1[l].astype(BF16), w3[l].astype(BF16),
                  w2[l].astype(BF16), final_g[None], tm_moe, final_norm=(l == depth - 1))
    return x2.reshape(B, S, D)
```

```python
import functools
import math

import jax
import jax.numpy as jnp
from jax import lax
from jax.experimental import pallas as pl
from jax.experimental.pallas import tpu as pltpu

F32 = jnp.float32
BF16 = jnp.bfloat16

D_MODEL = 2048
SGU_GROUPS = 8
SGU_WIDTH = 1024
SGU_GROUP_DIM = SGU_WIDTH // SGU_GROUPS
CHUNK = 128
DIFF_HEADS = 8
DIFF_HEAD_DIM = 64
DIFF_V_DIM = 2 * DIFF_HEAD_DIM
DIFF_QK_WIDTH = DIFF_HEADS * 2 * DIFF_HEAD_DIM
DIFF_WIDTH = DIFF_HEADS * DIFF_V_DIM
N_BRANCHES = 2
IN_COLS = 2 * SGU_WIDTH + 2 * DIFF_QK_WIDTH + DIFF_WIDTH + N_BRANCHES * D_MODEL
N_GROUPS = 4
EXPERTS_PER_GROUP = 4
N_EXPERTS = N_GROUPS * EXPERTS_PER_GROUP
EXPERT_FF = 512
RMS_EPS = 1e-6
LN_EPS = 1e-5

Q_COL128 = (2 * SGU_WIDTH) // 128
K_COL128 = (2 * SGU_WIDTH + DIFF_QK_WIDTH) // 128
V_COL128 = (2 * SGU_WIDTH + 2 * DIFF_QK_WIDTH) // 128
GATE_COL1024 = (2 * SGU_WIDTH + 2 * DIFF_QK_WIDTH + DIFF_WIDTH) // 1024

LANES = 128
ROUTER_COLS = LANES
VMEM_LIMIT = 56 * 1024 * 1024


def _params(semantics):
    return pltpu.CompilerParams(dimension_semantics=semantics, vmem_limit_bytes=VMEM_LIMIT)


def _gelu(x):
    return 0.5 * x * (1.0 + jnp.tanh(0.7978845608028654 * (x + 0.044715 * (x * x * x))))


def _rms_scale(x):
    return lax.rsqrt(jnp.mean(x * x, axis=-1, keepdims=True) + RMS_EPS)


def _inproj_kernel(x_ref, g_ref, w_ref, z_ref, h_ref):
    @pl.when(pl.program_id(1) == 0)
    def _():
        x = x_ref[...]
        h_ref[...] = (x * _rms_scale(x) * g_ref[...]).astype(BF16)

    z_ref[...] = jnp.dot(h_ref[...], w_ref[...], preferred_element_type=F32).astype(BF16)


def _inproj(x2, g, w_bf16, tm, tn):
    T, D = x2.shape
    N = w_bf16.shape[1]
    return pl.pallas_call(
        _inproj_kernel,
        grid=(T // tm, N // tn),
        in_specs=[
            pl.BlockSpec((tm, D), lambda i, j: (i, 0)),
            pl.BlockSpec((1, D), lambda i, j: (0, 0)),
            pl.BlockSpec((D, tn), lambda i, j: (0, j)),
        ],
        out_specs=pl.BlockSpec((tm, tn), lambda i, j: (i, j)),
        out_shape=jax.ShapeDtypeStruct((T, N), BF16),
        scratch_shapes=[pltpu.VMEM((tm, D), BF16)],
        compiler_params=_params(("parallel", "arbitrary")),
        name="inproj",
    )(x2, g, w_bf16)


def _sgu_kernel(u_ref, v_ref, lng_ref, lnb_ref, ws_ref, bs_ref, a_ref, *, chunks):
    v = _gelu(v_ref[...].astype(F32))
    mu = jnp.mean(v, axis=-1, keepdims=True)
    vc = v - mu
    var = jnp.mean(vc * vc, axis=-1, keepdims=True)
    vn = (vc * lax.rsqrt(var + LN_EPS) * lng_ref[...] + lnb_ref[...]).astype(BF16)
    for c in range(chunks):
        rows = slice(c * CHUNK, (c + 1) * CHUNK)
        for g in range(SGU_GROUPS):
            cols = slice(g * SGU_GROUP_DIM, (g + 1) * SGU_GROUP_DIM)
            mixed = jnp.dot(ws_ref[g], vn[rows, cols], preferred_element_type=F32)
            mixed = mixed + bs_ref[g]
            u = _gelu(u_ref[rows, cols].astype(F32))
            a_ref[rows, cols] = (u * mixed).astype(BF16)


def _sgu(z, ln_g, ln_b, ws_bf16, bs_col, tm):
    T = z.shape[0]
    wblk = SGU_WIDTH
    return pl.pallas_call(
        functools.partial(_sgu_kernel, chunks=tm // CHUNK),
        grid=(T // tm,),
        in_specs=[
            pl.BlockSpec((tm, wblk), lambda i: (i, 0)),
            pl.BlockSpec((tm, wblk), lambda i: (i, 1)),
            pl.BlockSpec((1, wblk), lambda i: (0, 0)),
            pl.BlockSpec((1, wblk), lambda i: (0, 0)),
            pl.BlockSpec((SGU_GROUPS, CHUNK, CHUNK), lambda i: (0, 0, 0)),
            pl.BlockSpec((SGU_GROUPS, CHUNK, 1), lambda i: (0, 0, 0)),
        ],
        out_specs=pl.BlockSpec((tm, wblk), lambda i: (i, 0)),
        out_shape=jax.ShapeDtypeStruct((T, SGU_WIDTH), BF16),
        compiler_params=_params(("parallel",)),
        name="sgu",
    )(z, z, ln_g, ln_b, ws_bf16, bs_col)


def _attn_kernel(slopes_ref, lq1_ref, lk1_ref, lq2_ref, lk2_ref, dgt_ref, q_ref, k_ref, v_ref,
                 o_ref, qx_ref, kx_ref, vt_ref, s0_ref, s1_ref, ml0_ref, ml1_ref,
                 m_ref, l_ref, acc_ref, *, tq, tk, seq, lam_init):
    h = pl.program_id(1)
    qi = pl.program_id(2)
    slope = slopes_ref[h]
    scale = DIFF_HEAD_DIM ** -0.5
    hd = DIFF_V_DIM
    n = seq // tk
    q0 = qi * tq
    jd = q0 // tk

    @pl.when(qi == 0)
    def _():
        def transpose_chunk(c, carry):
            sl = pl.ds(pl.multiple_of(c * tk, tk), tk)
            vt_ref[:, sl] = v_ref[sl, :].astype(F32).T.astype(BF16)
            return carry
        lax.fori_loop(0, n, transpose_chunk, 0)

    c_idx = lax.broadcasted_iota(jnp.int32, (tk, hd), 0)
    k_lane = lax.broadcasted_iota(jnp.int32, (tk, hd), 1)
    c_lo = (c_idx & (LANES - 1)).astype(F32)
    c_hi = (c_idx >> 7).astype(F32)
    kx = jnp.where(k_lane < 2, -1.0,
                   jnp.where(k_lane == 2, slope * c_lo,
                             jnp.where(k_lane == 3, (slope * LANES) * c_hi, 0.0)))
    kx_ref[...] = kx.astype(BF16)

    qt = (q_ref[...].astype(F32) * scale).T
    d_row = lax.broadcasted_iota(jnp.int32, (hd, tq), 0)
    q_main = jnp.concatenate([jnp.where(d_row < DIFF_HEAD_DIM, qt, 0.0),
                              jnp.where(d_row >= DIFF_HEAD_DIM, qt, 0.0)], axis=1).astype(BF16)
    x_row = lax.broadcasted_iota(jnp.int32, (hd, 2 * tq), 0)
    r_idx = lax.broadcasted_iota(jnp.int32, (hd, 2 * tq), 1)
    r_idx = jnp.where(r_idx >= tq, r_idx - tq, r_idx)
    r_lo = (r_idx & (LANES - 1)).astype(F32)
    r_hi = (r_idx >> 7).astype(F32)
    ext = jnp.where(x_row == 0, slope * r_lo,
                    jnp.where(x_row == 1, (slope * LANES) * r_hi,
                              jnp.where(x_row < 4, 1.0, 0.0)))
    for side, sign in ((0, 1.0), (1, -1.0), (2, 0.0)):
        qx_ref[side, 0:hd, :] = q_main
        qx_ref[side, hd:2 * hd, :] = (sign * ext).astype(BF16)

    m_ref[...] = jnp.full(m_ref.shape, -jnp.inf, F32)
    l_ref[...] = jnp.zeros(l_ref.shape, F32)
    acc_ref[...] = jnp.zeros(acc_ref.shape, F32)

    def scores(j, side, s_ref, ml_ref, bias=None):
        rows = pl.ds(pl.multiple_of(j * tk, tk), tk)
        kcx = jnp.concatenate([k_ref[rows, :], kx_ref[...]], axis=1)
        s = jnp.dot(kcx, qx_ref[side], preferred_element_type=F32)
        if bias is not None:
            s = s + bias
        s_ref[...] = s
        ml_ref[...] = jnp.max(s, axis=0, keepdims=True)

    def accumulate(j, cst, s_ref, ml_ref):
        m_prev = m_ref[...]
        m_new = jnp.maximum(m_prev, ml_ref[...] + cst)
        alpha = jnp.exp(m_prev - m_new)
        p = jnp.exp(s_ref[...] - (m_new - cst))
        l_ref[...] = alpha * l_ref[...] + jnp.sum(p, axis=0, keepdims=True)
        cols = pl.ds(pl.multiple_of(j * tk, tk), tk)
        acc_ref[...] = alpha * acc_ref[...] + jnp.dot(vt_ref[:, cols], p.astype(BF16),
                                                      preferred_element_type=F32)
        m_ref[...] = m_new

    kk = lax.broadcasted_iota(jnp.int32, (tk, 2 * tq), 0)
    rr = lax.broadcasted_iota(jnp.int32, (tk, 2 * tq), 1)
    rr = jnp.where(rr >= tq, rr - tq, rr)
    diag_bias = -slope * jnp.abs((kk - rr + (jd * tk - q0)).astype(F32))
    scores(jd, 2, s0_ref, ml0_ref, diag_bias)
    accumulate(jd, 0.0, s0_ref, ml0_ref)

    def chunk(t):
        side = (t >= jd).astype(jnp.int32)
        j = t + side
        sign = (1 - 2 * side).astype(F32)
        cst = -sign * slope * (q0 - j * tk).astype(F32)
        return j, side, cst

    if n > 1:
        ja0, sa0, _ = chunk(jnp.int32(0))
        scores(ja0, sa0, s0_ref, ml0_ref)

        def pair(i, carry):
            ja, _, ca = chunk(2 * i)
            jb, sb, cb = chunk(2 * i + 1)
            jc, sc, _ = chunk(2 * i + 2)
            scores(jb, sb, s1_ref, ml1_ref)
            accumulate(ja, ca, s0_ref, ml0_ref)
            scores(jc, sc, s0_ref, ml0_ref)
            accumulate(jb, cb, s1_ref, ml1_ref)
            return carry

        lax.fori_loop(0, (n - 2) // 2, pair, 0)
        jl, _, cl = chunk(jnp.int32(n - 2))
        accumulate(jl, cl, s0_ref, ml0_ref)

    lam = (jnp.exp(jnp.sum(lq1_ref[...] * lk1_ref[...], axis=-1, keepdims=True))
           - jnp.exp(jnp.sum(lq2_ref[...] * lk2_ref[...], axis=-1, keepdims=True))
           + lam_init)
    out = acc_ref[...] * (1.0 / l_ref[...])
    o = out[:, 0:tq] - lam * out[:, tq:2 * tq]
    o = o * lax.rsqrt(jnp.mean(o * o, axis=0, keepdims=True) + RMS_EPS)
    o = o * dgt_ref[...] * (1.0 - lam_init)
    o_ref[...] = o.T.astype(BF16)


def _attention(z, slopes, lq1, lk1, lq2, lk2, dg, batch, seq, tq, tk, lam_init):
    T = z.shape[0]
    nq = seq // tq
    hd = DIFF_V_DIM
    lam_spec = pl.BlockSpec((1, DIFF_HEAD_DIM), lambda b, h, i, s: (0, 0))
    grid_spec = pltpu.PrefetchScalarGridSpec(
        num_scalar_prefetch=1,
        grid=(batch, DIFF_HEADS, nq),
        in_specs=[
            lam_spec, lam_spec, lam_spec, lam_spec,
            pl.BlockSpec((hd, 1), lambda b, h, i, s: (0, 0)),
            pl.BlockSpec((tq, hd), lambda b, h, i, s: (b * nq + i, Q_COL128 + h)),
            pl.BlockSpec((seq, hd), lambda b, h, i, s: (b, K_COL128 + h)),
            pl.BlockSpec((seq, hd), lambda b, h, i, s: (b, V_COL128 + h)),
        ],
        out_specs=pl.BlockSpec((tq, hd), lambda b, h, i, s: (b * nq + i, h)),
        scratch_shapes=[
            pltpu.VMEM((3, 2 * hd, 2 * tq), BF16),
            pltpu.VMEM((tk, hd), BF16),
            pltpu.VMEM((hd, seq), BF16),
            pltpu.VMEM((tk, 2 * tq), F32),
            pltpu.VMEM((tk, 2 * tq), F32),
            pltpu.VMEM((1, 2 * tq), F32),
            pltpu.VMEM((1, 2 * tq), F32),
            pltpu.VMEM((1, 2 * tq), F32),
            pltpu.VMEM((1, 2 * tq), F32),
            pltpu.VMEM((hd, 2 * tq), F32),
        ],
    )
    return pl.pallas_call(
        functools.partial(_attn_kernel, tq=tq, tk=tk, seq=seq, lam_init=lam_init),
        grid_spec=grid_spec,
        out_shape=jax.ShapeDtypeStruct((T, DIFF_WIDTH), BF16),
        compiler_params=_params(("parallel", "parallel", "arbitrary")),
        name="diff_attn",
    )(slopes, lq1, lk1, lq2, lk2, dg, z, z, z)


def _merge_kernel(x_ref, a_ref, b_ref, g00_ref, g01_ref, g10_ref, g11_ref, bg_ref,
                  wa_ref, wb_ref, wo_ref, o_ref, merged_ref):
    gate_refs = ((g00_ref, g01_ref), (g10_ref, g11_ref))
    half = D_MODEL // 2
    a = a_ref[...]
    b = b_ref[...]
    for c in range(2):
        cols = slice(c * half, (c + 1) * half)
        ga = jax.nn.sigmoid(gate_refs[0][c][...].astype(F32) + bg_ref[0:1, cols])
        gb = jax.nn.sigmoid(gate_refs[1][c][...].astype(F32) + bg_ref[1:2, cols])
        pa = jnp.dot(a, wa_ref[:, cols], preferred_element_type=F32)
        pb = jnp.dot(b, wb_ref[:, cols], preferred_element_type=F32)
        merged_ref[:, cols] = (ga * pa + gb * pb).astype(BF16)
    o_ref[...] = x_ref[...] + jnp.dot(merged_ref[...], wo_ref[...], preferred_element_type=F32)


def _merge(x2, a, b, z, b_gate, wa, wb, wo, tm):
    T, D = x2.shape
    half = D // 2

    def gate_spec(k):
        return pl.BlockSpec((tm, half), lambda i: (i, GATE_COL1024 + k))

    def const_spec(shape):
        return pl.BlockSpec(shape, lambda i: (0, 0), pipeline_mode=pl.Buffered(1))

    return pl.pallas_call(
        _merge_kernel,
        grid=(T // tm,),
        in_specs=[
            pl.BlockSpec((tm, D), lambda i: (i, 0)),
            pl.BlockSpec((tm, SGU_WIDTH), lambda i: (i, 0)),
            pl.BlockSpec((tm, DIFF_WIDTH), lambda i: (i, 0)),
            gate_spec(0), gate_spec(1), gate_spec(2), gate_spec(3),
            const_spec((N_BRANCHES, D)),
            const_spec((SGU_WIDTH, D)),
            const_spec((DIFF_WIDTH, D)),
            const_spec((D, D)),
        ],
        out_specs=pl.BlockSpec((tm, D), lambda i: (i, 0)),
        out_shape=jax.ShapeDtypeStruct((T, D), F32),
        scratch_shapes=[pltpu.VMEM((tm, D), BF16)],
        compiler_params=_params(("parallel",)),
        name="merge_out",
    )(x2, a, b, z, z, z, z, b_gate, wa, wb, wo)


def _split_bf16(x):
    hi = x.astype(BF16)
    lo = (x - hi.astype(F32)).astype(BF16)
    return hi, lo


def _dot3(x_hi, x_lo, w_hi, w_lo):
    return (jnp.dot(x_hi, w_hi, preferred_element_type=F32)
            + jnp.dot(x_lo, w_hi, preferred_element_type=F32)
            + jnp.dot(x_hi, w_lo, preferred_element_type=F32))


def _router(h, rw_hi_ref, rw_lo_ref, rb_ref):
    tm = h.shape[0]
    h_hi, h_lo = _split_bf16(h)
    logits = _dot3(h_hi, h_lo, rw_hi_ref[...], rw_lo_ref[...]) + rb_ref[...]
    lane = lax.broadcasted_iota(jnp.int32, (tm, ROUTER_COLS), 1)
    neg = jnp.float32(-jnp.inf)
    is_group = lane < N_GROUPS
    gl = jnp.where(is_group, logits, neg)
    gmax = jnp.max(gl, axis=-1, keepdims=True)
    gexp = jnp.exp(gl - gmax)
    g_w = 1.0 / jnp.sum(gexp, axis=-1, keepdims=True)
    g_idx = jnp.min(jnp.where(gl == gmax, lane, ROUTER_COLS), axis=-1, keepdims=True)
    e_lo = N_GROUPS + g_idx * EXPERTS_PER_GROUP
    in_group = (lane >= e_lo) & (lane < e_lo + EXPERTS_PER_GROUP)
    el = jnp.where(in_group, logits, neg)
    m1 = jnp.max(el, axis=-1, keepdims=True)
    i1 = jnp.min(jnp.where(el == m1, lane, ROUTER_COLS), axis=-1, keepdims=True)
    el2 = jnp.where(lane == i1, neg, el)
    m2 = jnp.max(el2, axis=-1, keepdims=True)
    i2 = jnp.min(jnp.where(el2 == m2, lane, ROUTER_COLS), axis=-1, keepdims=True)
    t = jnp.exp(m2 - m1)
    w1 = g_w / (1.0 + t)
    w2 = g_w * t / (1.0 + t)
    comb = jnp.where(lane == i1, w1, 0.0) + jnp.where(lane == i2, w2, 0.0)
    return comb


def _moe_kernel(x_ref, g_ref, rw_hi_ref, rw_lo_ref, rb_ref, w1_ref, w3_ref, w2_ref, fg_ref,
                o_ref, h_ref, comb_ref, acc_ref, *, final_norm):
    e = pl.program_id(1)

    @pl.when(e == 0)
    def _():
        x = x_ref[...]
        h = x * _rms_scale(x) * g_ref[...]
        h_ref[...] = h.astype(BF16)
        comb_ref[...] = _router(h, rw_hi_ref, rw_lo_ref, rb_ref)
        acc_ref[...] = x

    h = h_ref[...]
    hid = jax.nn.silu(jnp.dot(h, w1_ref[0], preferred_element_type=F32)) \
        * jnp.dot(h, w3_ref[0], preferred_element_type=F32)
    lane = lax.broadcasted_iota(jnp.int32, comb_ref.shape, 1)
    c_e = jnp.sum(jnp.where(lane == e + N_GROUPS, comb_ref[...], 0.0), axis=-1, keepdims=True)
    acc_ref[...] += c_e * jnp.dot(hid.astype(BF16), w2_ref[0], preferred_element_type=F32)

    @pl.when(e == N_EXPERTS - 1)
    def _():
        y = acc_ref[...]
        if final_norm:
            y = y * _rms_scale(y) * fg_ref[...]
        o_ref[...] = y


def _moe(x2, g, rw_hi, rw_lo, rb, w1, w3, w2, final_g, tm, final_norm):
    T, D = x2.shape
    row = pl.BlockSpec((1, D), lambda i, e: (0, 0))
    rspec = pl.BlockSpec((D, ROUTER_COLS), lambda i, e: (0, 0))
    return pl.pallas_call(
        functools.partial(_moe_kernel, final_norm=final_norm),
        grid=(T // tm, N_EXPERTS),
        in_specs=[
            pl.BlockSpec((tm, D), lambda i, e: (i, 0)),
            row, rspec, rspec,
            pl.BlockSpec((1, ROUTER_COLS), lambda i, e: (0, 0)),
            pl.BlockSpec((1, D, EXPERT_FF), lambda i, e: (e, 0, 0)),
            pl.BlockSpec((1, D, EXPERT_FF), lambda i, e: (e, 0, 0)),
            pl.BlockSpec((1, EXPERT_FF, D), lambda i, e: (e, 0, 0)),
            row,
        ],
        out_specs=pl.BlockSpec((tm, D), lambda i, e: (i, 0)),
        out_shape=jax.ShapeDtypeStruct((T, D), F32),
        scratch_shapes=[
            pltpu.VMEM((tm, D), BF16),
            pltpu.VMEM((tm, ROUTER_COLS), F32),
            pltpu.VMEM((tm, D), F32),
        ],
        compiler_params=_params(("parallel", "arbitrary")),
        name="moe",
    )(x2, g, rw_hi, rw_lo, rb, w1, w3, w2, final_g)


def _router_weights(rg_w, rg_b, re_w, re_b):
    D = rg_w.shape[0]
    w = jnp.zeros((D, ROUTER_COLS), F32)
    w = w.at[:, :N_GROUPS].set(rg_w).at[:, N_GROUPS:N_GROUPS + N_EXPERTS].set(re_w)
    b = jnp.zeros((1, ROUTER_COLS), F32)
    b = b.at[0, :N_GROUPS].set(rg_b).at[0, N_GROUPS:N_GROUPS + N_EXPERTS].set(re_b)
    hi = w.astype(BF16)
    lo = (w - hi.astype(F32)).astype(BF16)
    return hi, lo, b


def kernel(x, norm1_g, w_in, b_gate, sgu_ln_g, sgu_ln_b, sgu_w, sgu_b, lam_q1, lam_k1, lam_q2,
           lam_k2, diff_norm_g, w_proj_a, w_proj_b, w_out, norm2_g, router_g_w, router_g_b,
           router_e_w, router_e_b, w1, w3, w2, final_g):
    B, S, D = x.shape
    assert D == D_MODEL and w_in.shape[2] == IN_COLS
    depth = w_in.shape[0]
    T = B * S
    slopes = jnp.exp2(-8.0 * jnp.arange(1, DIFF_HEADS + 1, dtype=F32) / DIFF_HEADS)

    tm_in = min(1024, T)
    tm_sgu = min(512, T)
    tq = min(256, S)
    tk = min(512, S)
    tm_merge = min(256, T)
    tm_moe = min(512, T)

    x2 = x.reshape(T, D)
    for l in range(depth):
        lam_init = 0.8 - 0.6 * math.exp(-0.3 * l)
        z = _inproj(x2, norm1_g[l][None], w_in[l].astype(BF16), tm_in, 512)
        a = _sgu(z, sgu_ln_g[l][None], sgu_ln_b[l][None], sgu_w[l].astype(BF16),
                 sgu_b[l][:, :, None], tm_sgu)
        o = _attention(z, slopes, lam_q1[l][None], lam_k1[l][None], lam_q2[l][None],
                       lam_k2[l][None], diff_norm_g[l][:, None], B, S, tq, tk, lam_init)
        x2 = _merge(x2, a, o, z, b_gate[l], w_proj_a[l].astype(BF16), w_proj_b[l].astype(BF16),
                    w_out[l].astype(BF16), tm_merge)
        rw_hi, rw_lo, rb = _router_weights(router_g_w[l], router_g_b[l], router_e_w[l],
                                           router_e_b[l])
        x2 = _moe(x2, norm2_g[l][None], rw_hi, rw_lo, rb, w1[l].astype(BF16), w3[l].astype(BF16),
                  w2[l].astype(BF16), final_g[None], tm_moe, final_norm=(l == depth - 1))
    return x2.reshape(B, S, D)
```

```python
import functools
import math

import jax
import jax.numpy as jnp
from jax import lax
from jax.experimental import pallas as pl
from jax.experimental.pallas import tpu as pltpu

F32 = jnp.float32
BF16 = jnp.bfloat16

D_MODEL = 2048
SGU_GROUPS = 8
SGU_WIDTH = 1024
SGU_GROUP_DIM = SGU_WIDTH // SGU_GROUPS
CHUNK = 128
DIFF_HEADS = 8
DIFF_HEAD_DIM = 64
DIFF_V_DIM = 2 * DIFF_HEAD_DIM
DIFF_QK_WIDTH = DIFF_HEADS * 2 * DIFF_HEAD_DIM
DIFF_WIDTH = DIFF_HEADS * DIFF_V_DIM
N_BRANCHES = 2
IN_COLS = 2 * SGU_WIDTH + 2 * DIFF_QK_WIDTH + DIFF_WIDTH + N_BRANCHES * D_MODEL
N_GROUPS = 4
EXPERTS_PER_GROUP = 4
N_EXPERTS = N_GROUPS * EXPERTS_PER_GROUP
EXPERT_FF = 512
RMS_EPS = 1e-6
LN_EPS = 1e-5

Q_COL128 = (2 * SGU_WIDTH) // 128
K_COL128 = (2 * SGU_WIDTH + DIFF_QK_WIDTH) // 128
V_COL128 = (2 * SGU_WIDTH + 2 * DIFF_QK_WIDTH) // 128
GATE_COL1024 = (2 * SGU_WIDTH + 2 * DIFF_QK_WIDTH + DIFF_WIDTH) // 1024

LANES = 128
ROUTER_COLS = LANES
VMEM_LIMIT = 56 * 1024 * 1024


def _params(semantics):
    return pltpu.CompilerParams(dimension_semantics=semantics, vmem_limit_bytes=VMEM_LIMIT)


def _gelu(x):
    return 0.5 * x * (1.0 + jnp.tanh(0.7978845608028654 * (x + 0.044715 * (x * x * x))))


def _rms_scale(x):
    return lax.rsqrt(jnp.mean(x * x, axis=-1, keepdims=True) + RMS_EPS)


def _inproj_kernel(x_ref, g_ref, w_ref, z_ref, h_ref):
    @pl.when(pl.program_id(1) == 0)
    def _():
        x = x_ref[...]
        h_ref[...] = (x * _rms_scale(x) * g_ref[...]).astype(BF16)

    z_ref[...] = jnp.dot(h_ref[...], w_ref[...], preferred_element_type=F32).astype(BF16)


def _inproj(x2, g, w_bf16, tm, tn):
    T, D = x2.shape
    N = w_bf16.shape[1]
    return pl.pallas_call(
        _inproj_kernel,
        grid=(T // tm, N // tn),
        in_specs=[
            pl.BlockSpec((tm, D), lambda i, j: (i, 0)),
            pl.BlockSpec((1, D), lambda i, j: (0, 0)),
            pl.BlockSpec((D, tn), lambda i, j: (0, j)),
        ],
        out_specs=pl.BlockSpec((tm, tn), lambda i, j: (i, j)),
        out_shape=jax.ShapeDtypeStruct((T, N), BF16),
        scratch_shapes=[pltpu.VMEM((tm, D), BF16)],
        compiler_params=_params(("parallel", "arbitrary")),
        name="inproj",
    )(x2, g, w_bf16)


def _sgu_kernel(u_ref, v_ref, lng_ref, lnb_ref, ws_ref, bs_ref, a_ref, *, chunks):
    v = _gelu(v_ref[...].astype(F32))
    mu = jnp.mean(v, axis=-1, keepdims=True)
    vc = v - mu
    var = jnp.mean(vc * vc, axis=-1, keepdims=True)
    vn = (vc * lax.rsqrt(var + LN_EPS) * lng_ref[...] + lnb_ref[...]).astype(BF16)
    for c in range(chunks):
        rows = slice(c * CHUNK, (c + 1) * CHUNK)
        for g in range(SGU_GROUPS):
            cols = slice(g * SGU_GROUP_DIM, (g + 1) * SGU_GROUP_DIM)
            mixed = jnp.dot(ws_ref[g], vn[rows, cols], preferred_element_type=F32)
            mixed = mixed + bs_ref[g]
            u = _gelu(u_ref[rows, cols].astype(F32))
            a_ref[rows, cols] = (u * mixed).astype(BF16)


def _sgu(z, ln_g, ln_b, ws_bf16, bs_col, tm):
    T = z.shape[0]
    wblk = SGU_WIDTH
    return pl.pallas_call(
        functools.partial(_sgu_kernel, chunks=tm // CHUNK),
        grid=(T // tm,),
        in_specs=[
            pl.BlockSpec((tm, wblk), lambda i: (i, 0)),
            pl.BlockSpec((tm, wblk), lambda i: (i, 1)),
            pl.BlockSpec((1, wblk), lambda i: (0, 0)),
            pl.BlockSpec((1, wblk), lambda i: (0, 0)),
            pl.BlockSpec((SGU_GROUPS, CHUNK, CHUNK), lambda i: (0, 0, 0)),
            pl.BlockSpec((SGU_GROUPS, CHUNK, 1), lambda i: (0, 0, 0)),
        ],
        out_specs=pl.BlockSpec((tm, wblk), lambda i: (i, 0)),
        out_shape=jax.ShapeDtypeStruct((T, SGU_WIDTH), BF16),
        compiler_params=_params(("parallel",)),
        name="sgu",
    )(z, z, ln_g, ln_b, ws_bf16, bs_col)


LOG2E = 1.4426950408889634
VT_PAD = 16


def _split3(x):
    a1 = x.astype(BF16).astype(F32)
    a2 = (x - a1).astype(BF16).astype(F32)
    a3 = (x - a1 - a2).astype(BF16).astype(F32)
    return a1, a2, a3


def _pick(row, values):
    out = jnp.zeros(values[0].shape, F32)
    for i, v in enumerate(values):
        out = jnp.where(row == i, v, out)
    return out


def _attn_kernel(slopes_ref, lq1_ref, lk1_ref, lq2_ref, lk2_ref, dgt_ref, q_ref, k_ref, v_ref,
                 o_ref, qx_ref, kx_ref, vt_ref, rel_ref, s0_ref, s1_ref, ml0_ref, ml1_ref,
                 m_ref, acc_ref, *, tq, tk, seq, lam_init):
    h = pl.program_id(1)
    qi = pl.program_id(2)
    slope2 = slopes_ref[h] * LOG2E
    hd = DIFF_V_DIM
    n = seq // tk
    q0 = qi * tq
    jd = q0 // tk

    @pl.when(qi == 0)
    def _():
        pad_row = lax.broadcasted_iota(jnp.int32, (VT_PAD, tk), 0)
        ones_pad = jnp.where(pad_row == 0, 1.0, 0.0).astype(BF16)

        def transpose_chunk(c, carry):
            sl = pl.ds(pl.multiple_of(c * tk, tk), tk)
            vt_ref[0:hd, sl] = v_ref[sl, :].astype(F32).T.astype(BF16)
            vt_ref[hd:hd + VT_PAD, sl] = ones_pad
            return carry
        lax.fori_loop(0, n, transpose_chunk, 0)

        c_idx = lax.broadcasted_iota(jnp.int32, (tk, hd), 0)
        k_lane = lax.broadcasted_iota(jnp.int32, (tk, hd), 1)
        a = _split3(jnp.full((tk, hd), slope2, F32))
        c_lo = (c_idx & (LANES - 1)).astype(F32)
        c_hi = (c_idx >> 7).astype(F32)
        kx = _pick(k_lane, [-a[0], -a[1], -a[2], -LANES * a[0], -LANES * a[1], -LANES * a[2],
                            c_lo, c_lo, c_lo, c_hi, c_hi, c_hi])
        kx_ref[...] = kx.astype(BF16)

        x_row = lax.broadcasted_iota(jnp.int32, (hd, 2 * tq), 0)
        r_idx = lax.broadcasted_iota(jnp.int32, (hd, 2 * tq), 1)
        r_idx = jnp.where(r_idx >= tq, r_idx - tq, r_idx)
        r_lo = (r_idx & (LANES - 1)).astype(F32)
        r_hi = (r_idx >> 7).astype(F32)
        a = _split3(jnp.full((hd, 2 * tq), slope2, F32))
        ext = _pick(x_row, [r_lo, r_lo, r_lo, r_hi, r_hi, r_hi,
                            a[0], a[1], a[2], LANES * a[0], LANES * a[1], LANES * a[2]])
        for side, sign in ((0, 1.0), (1, -1.0), (2, 0.0)):
            qx_ref[side, hd:2 * hd, :] = (sign * ext).astype(BF16)

        rel_ref[...] = (lax.broadcasted_iota(jnp.int32, (tk, tq), 0)
                        - lax.broadcasted_iota(jnp.int32, (tk, tq), 1)).astype(F32)

    qt = (q_ref[...].astype(F32) * (DIFF_HEAD_DIM ** -0.5 * LOG2E)).T
    d_row = lax.broadcasted_iota(jnp.int32, (hd, tq), 0)
    q_main = jnp.concatenate([jnp.where(d_row < DIFF_HEAD_DIM, qt, 0.0),
                              jnp.where(d_row >= DIFF_HEAD_DIM, qt, 0.0)], axis=1).astype(BF16)
    for side in range(3):
        qx_ref[side, 0:hd, :] = q_main

    m_ref[...] = jnp.full(m_ref.shape, -jnp.inf, F32)
    acc_ref[...] = jnp.zeros(acc_ref.shape, F32)

    def scores(j, side, s_ref, ml_ref, bias=None):
        rows = pl.ds(pl.multiple_of(j * tk, tk), tk)
        kcx = jnp.concatenate([k_ref[rows, :], kx_ref[...]], axis=1)
        s = jnp.dot(kcx, qx_ref[side], preferred_element_type=F32)
        if bias is not None:
            s = s + jnp.concatenate([bias, bias], axis=1)
        s_ref[...] = s
        ml_ref[...] = jnp.max(s, axis=0, keepdims=True)

    def accumulate(j, cst, s_ref, ml_ref):
        m_prev = m_ref[...]
        m_new = jnp.maximum(m_prev, ml_ref[...] + cst)
        alpha = jnp.exp2(m_prev - m_new)
        p = jnp.exp2(s_ref[...] - (m_new - cst)).astype(BF16)
        cols = pl.ds(pl.multiple_of(j * tk, tk), tk)
        acc_ref[...] = alpha * acc_ref[...] + jnp.dot(vt_ref[:, cols], p,
                                                      preferred_element_type=F32)
        m_ref[...] = m_new

    def chunk(t):
        side = (t >= jd).astype(jnp.int32)
        j = t + side
        sign = (1 - 2 * side).astype(F32)
        cst = -sign * slope2 * (q0 - j * tk).astype(F32)
        return j, side, cst

    diag_bias = -slope2 * jnp.abs(rel_ref[...] + (jd * tk - q0).astype(F32))
    scores(jd, 2, s0_ref, ml0_ref, diag_bias)

    if n > 1:
        ja0, sa0, _ = chunk(jnp.int32(0))
        scores(ja0, sa0, s1_ref, ml1_ref)
        accumulate(jd, 0.0, s0_ref, ml0_ref)

        def pair(i, carry):
            ja, _, ca = chunk(2 * i)
            jb, sb, cb = chunk(2 * i + 1)
            jc, sc, _ = chunk(2 * i + 2)
            scores(jb, sb, s0_ref, ml0_ref)
            accumulate(ja, ca, s1_ref, ml1_ref)
            scores(jc, sc, s1_ref, ml1_ref)
            accumulate(jb, cb, s0_ref, ml0_ref)
            return carry

        lax.fori_loop(0, (n - 2) // 2, pair, 0)
        jl, _, cl = chunk(jnp.int32(n - 2))
        accumulate(jl, cl, s1_ref, ml1_ref)
    else:
        accumulate(jd, 0.0, s0_ref, ml0_ref)

    lam = (jnp.exp(jnp.sum(lq1_ref[...] * lk1_ref[...], axis=-1, keepdims=True))
           - jnp.exp(jnp.sum(lq2_ref[...] * lk2_ref[...], axis=-1, keepdims=True))
           + lam_init)
    acc = acc_ref[...]
    out = acc[0:hd, :] * (1.0 / acc[hd:hd + 1, :])
    o = out[:, 0:tq] - lam * out[:, tq:2 * tq]
    o = o * lax.rsqrt(jnp.mean(o * o, axis=0, keepdims=True) + RMS_EPS)
    o = o * dgt_ref[...] * (1.0 - lam_init)
    o_ref[...] = o.T.astype(BF16)


def _attention(z, slopes, lq1, lk1, lq2, lk2, dg, batch, seq, tq, tk, lam_init):
    T = z.shape[0]
    nq = seq // tq
    hd = DIFF_V_DIM
    lam_spec = pl.BlockSpec((1, DIFF_HEAD_DIM), lambda b, h, i, s: (0, 0))
    grid_spec = pltpu.PrefetchScalarGridSpec(
        num_scalar_prefetch=1,
        grid=(batch, DIFF_HEADS, nq),
        in_specs=[
            lam_spec, lam_spec, lam_spec, lam_spec,
            pl.BlockSpec((hd, 1), lambda b, h, i, s: (0, 0)),
            pl.BlockSpec((tq, hd), lambda b, h, i, s: (b * nq + i, Q_COL128 + h)),
            pl.BlockSpec((seq, hd), lambda b, h, i, s: (b, K_COL128 + h)),
            pl.BlockSpec((seq, hd), lambda b, h, i, s: (b, V_COL128 + h)),
        ],
        out_specs=pl.BlockSpec((tq, hd), lambda b, h, i, s: (b * nq + i, h)),
        scratch_shapes=[
            pltpu.VMEM((3, 2 * hd, 2 * tq), BF16),
            pltpu.VMEM((tk, hd), BF16),
            pltpu.VMEM((hd + VT_PAD, seq), BF16),
            pltpu.VMEM((tk, tq), F32),
            pltpu.VMEM((tk, 2 * tq), F32),
            pltpu.VMEM((tk, 2 * tq), F32),
            pltpu.VMEM((1, 2 * tq), F32),
            pltpu.VMEM((1, 2 * tq), F32),
            pltpu.VMEM((1, 2 * tq), F32),
            pltpu.VMEM((hd + VT_PAD, 2 * tq), F32),
        ],
    )
    return pl.pallas_call(
        functools.partial(_attn_kernel, tq=tq, tk=tk, seq=seq, lam_init=lam_init),
        grid_spec=grid_spec,
        out_shape=jax.ShapeDtypeStruct((T, DIFF_WIDTH), BF16),
        compiler_params=_params(("parallel", "parallel", "arbitrary")),
        name="diff_attn",
    )(slopes, lq1, lk1, lq2, lk2, dg, z, z, z)


def _merge_kernel(x_ref, a_ref, b_ref, g00_ref, g01_ref, g10_ref, g11_ref, bg_ref,
                  wa_ref, wb_ref, wo_ref, o_ref, merged_ref):
    gate_refs = ((g00_ref, g01_ref), (g10_ref, g11_ref))
    half = D_MODEL // 2
    a = a_ref[...]
    b = b_ref[...]
    for c in range(2):
        cols = slice(c * half, (c + 1) * half)
        ga = jax.nn.sigmoid(gate_refs[0][c][...].astype(F32) + bg_ref[0:1, cols])
        gb = jax.nn.sigmoid(gate_refs[1][c][...].astype(F32) + bg_ref[1:2, cols])
        pa = jnp.dot(a, wa_ref[:, cols], preferred_element_type=F32)
        pb = jnp.dot(b, wb_ref[:, cols], preferred_element_type=F32)
        merged_ref[:, cols] = (ga * pa + gb * pb).astype(BF16)
    o_ref[...] = x_ref[...] + jnp.dot(merged_ref[...], wo_ref[...], preferred_element_type=F32)


def _merge(x2, a, b, z, b_gate, wa, wb, wo, tm):
    T, D = x2.shape
    half = D // 2

    def gate_spec(k):
        return pl.BlockSpec((tm, half), lambda i: (i, GATE_COL1024 + k))

    def const_spec(shape):
        return pl.BlockSpec(shape, lambda i: (0, 0), pipeline_mode=pl.Buffered(1))

    return pl.pallas_call(
        _merge_kernel,
        grid=(T // tm,),
        in_specs=[
            pl.BlockSpec((tm, D), lambda i: (i, 0)),
            pl.BlockSpec((tm, SGU_WIDTH), lambda i: (i, 0)),
            pl.BlockSpec((tm, DIFF_WIDTH), lambda i: (i, 0)),
            gate_spec(0), gate_spec(1), gate_spec(2), gate_spec(3),
            const_spec((N_BRANCHES, D)),
            const_spec((SGU_WIDTH, D)),
            const_spec((DIFF_WIDTH, D)),
            const_spec((D, D)),
        ],
        out_specs=pl.BlockSpec((tm, D), lambda i: (i, 0)),
        out_shape=jax.ShapeDtypeStruct((T, D), F32),
        scratch_shapes=[pltpu.VMEM((tm, D), BF16)],
        compiler_params=_params(("parallel",)),
        name="merge_out",
    )(x2, a, b, z, z, z, z, b_gate, wa, wb, wo)


def _split_bf16(x):
    hi = x.astype(BF16)
    lo = (x - hi.astype(F32)).astype(BF16)
    return hi, lo


def _dot3(x_hi, x_lo, w_hi, w_lo):
    return (jnp.dot(x_hi, w_hi, preferred_element_type=F32)
            + jnp.dot(x_lo, w_hi, preferred_element_type=F32)
            + jnp.dot(x_hi, w_lo, preferred_element_type=F32))


def _router(h, rw_hi_ref, rw_lo_ref, rb_ref):
    tm = h.shape[0]
    h_hi, h_lo = _split_bf16(h)
    logits = _dot3(h_hi, h_lo, rw_hi_ref[...], rw_lo_ref[...]) + rb_ref[...]
    lane = lax.broadcasted_iota(jnp.int32, (tm, ROUTER_COLS), 1)
    neg = jnp.float32(-jnp.inf)
    is_group = lane < N_GROUPS
    gl = jnp.where(is_group, logits, neg)
    gmax = jnp.max(gl, axis=-1, keepdims=True)
    gexp = jnp.exp(gl - gmax)
    g_w = 1.0 / jnp.sum(gexp, axis=-1, keepdims=True)
    g_idx = jnp.min(jnp.where(gl == gmax, lane, ROUTER_COLS), axis=-1, keepdims=True)
    e_lo = N_GROUPS + g_idx * EXPERTS_PER_GROUP
    in_group = (lane >= e_lo) & (lane < e_lo + EXPERTS_PER_GROUP)
    el = jnp.where(in_group, logits, neg)
    m1 = jnp.max(el, axis=-1, keepdims=True)
    i1 = jnp.min(jnp.where(el == m1, lane, ROUTER_COLS), axis=-1, keepdims=True)
    el2 = jnp.where(lane == i1, neg, el)
    m2 = jnp.max(el2, axis=-1, keepdims=True)
    i2 = jnp.min(jnp.where(el2 == m2, lane, ROUTER_COLS), axis=-1, keepdims=True)
    t = jnp.exp(m2 - m1)
    w1 = g_w / (1.0 + t)
    w2 = g_w * t / (1.0 + t)
    comb = jnp.where(lane == i1, w1, 0.0) + jnp.where(lane == i2, w2, 0.0)
    return comb


def _moe_kernel(x_ref, g_ref, rw_hi_ref, rw_lo_ref, rb_ref, w1_ref, w3_ref, w2_ref, fg_ref,
                o_ref, h_ref, comb_ref, acc_ref, *, final_norm):
    e = pl.program_id(1)

    @pl.when(e == 0)
    def _():
        x = x_ref[...]
        h = x * _rms_scale(x) * g_ref[...]
        h_ref[...] = h.astype(BF16)
        comb_ref[...] = _router(h, rw_hi_ref, rw_lo_ref, rb_ref)
        acc_ref[...] = x

    h = h_ref[...]
    hid = jax.nn.silu(jnp.dot(h, w1_ref[0], preferred_element_type=F32)) \
        * jnp.dot(h, w3_ref[0], preferred_element_type=F32)
    lane = lax.broadcasted_iota(jnp.int32, comb_ref.shape, 1)
    c_e = jnp.sum(jnp.where(lane == e + N_GROUPS, comb_ref[...], 0.0), axis=-1, keepdims=True)
    acc_ref[...] += c_e * jnp.dot(hid.astype(BF16), w2_ref[0], preferred_element_type=F32)

    @pl.when(e == N_EXPERTS - 1)
    def _():
        y = acc_ref[...]
        if final_norm:
            y = y * _rms_scale(y) * fg_ref[...]
        o_ref[...] = y


def _moe(x2, g, rw_hi, rw_lo, rb, w1, w3, w2, final_g, tm, final_norm):
    T, D = x2.shape
    row = pl.BlockSpec((1, D), lambda i, e: (0, 0))
    rspec = pl.BlockSpec((D, ROUTER_COLS), lambda i, e: (0, 0))
    return pl.pallas_call(
        functools.partial(_moe_kernel, final_norm=final_norm),
        grid=(T // tm, N_EXPERTS),
        in_specs=[
            pl.BlockSpec((tm, D), lambda i, e: (i, 0)),
            row, rspec, rspec,
            pl.BlockSpec((1, ROUTER_COLS), lambda i, e: (0, 0)),
            pl.BlockSpec((1, D, EXPERT_FF), lambda i, e: (e, 0, 0)),
            pl.BlockSpec((1, D, EXPERT_FF), lambda i, e: (e, 0, 0)),
            pl.BlockSpec((1, EXPERT_FF, D), lambda i, e: (e, 0, 0)),
            row,
        ],
        out_specs=pl.BlockSpec((tm, D), lambda i, e: (i, 0)),
        out_shape=jax.ShapeDtypeStruct((T, D), F32),
        scratch_shapes=[
            pltpu.VMEM((tm, D), BF16),
            pltpu.VMEM((tm, ROUTER_COLS), F32),
            pltpu.VMEM((tm, D), F32),
        ],
        compiler_params=_params(("parallel", "arbitrary")),
        name="moe",
    )(x2, g, rw_hi, rw_lo, rb, w1, w3, w2, final_g)


def _router_weights(rg_w, rg_b, re_w, re_b):
    D = rg_w.shape[0]
    w = jnp.zeros((D, ROUTER_COLS), F32)
    w = w.at[:, :N_GROUPS].set(rg_w).at[:, N_GROUPS:N_GROUPS + N_EXPERTS].set(re_w)
    b = jnp.zeros((1, ROUTER_COLS), F32)
    b = b.at[0, :N_GROUPS].set(rg_b).at[0, N_GROUPS:N_GROUPS + N_EXPERTS].set(re_b)
    hi = w.astype(BF16)
    lo = (w - hi.astype(F32)).astype(BF16)
    return hi, lo, b


def kernel(x, norm1_g, w_in, b_gate, sgu_ln_g, sgu_ln_b, sgu_w, sgu_b, lam_q1, lam_k1, lam_q2,
           lam_k2, diff_norm_g, w_proj_a, w_proj_b, w_out, norm2_g, router_g_w, router_g_b,
           router_e_w, router_e_b, w1, w3, w2, final_g):
    B, S, D = x.shape
    assert D == D_MODEL and w_in.shape[2] == IN_COLS
    depth = w_in.shape[0]
    T = B * S
    slopes = jnp.exp2(-8.0 * jnp.arange(1, DIFF_HEADS + 1, dtype=F32) / DIFF_HEADS)

    tm_in = min(1024, T)
    tm_sgu = min(512, T)
    tq = min(512, S)
    tk = min(1024, S)
    tm_merge = min(256, T)
    tm_moe = min(512, T)

    x2 = x.reshape(T, D)
    for l in range(depth):
        lam_init = 0.8 - 0.6 * math.exp(-0.3 * l)
        z = _inproj(x2, norm1_g[l][None], w_in[l].astype(BF16), tm_in, 512)
        a = _sgu(z, sgu_ln_g[l][None], sgu_ln_b[l][None], sgu_w[l].astype(BF16),
                 sgu_b[l][:, :, None], tm_sgu)
        o = _attention(z, slopes, lam_q1[l][None], lam_k1[l][None], lam_q2[l][None],
                       lam_k2[l][None], diff_norm_g[l][:, None], B, S, tq, tk, lam_init)
        x2 = _merge(x2, a, o, z, b_gate[l], w_proj_a[l].astype(BF16), w_proj_b[l].astype(BF16),
                    w_out[l].astype(BF16), tm_merge)
        rw_hi, rw_lo, rb = _router_weights(router_g_w[l], router_g_b[l], router_e_w[l],
                                           router_e_b[l])
        x2 = _moe(x2, norm2_g[l][None], rw_hi, rw_lo, rb, w1[l].astype(BF16), w3[l].astype(BF16),
                  w2[l].astype(BF16), final_g[None], tm_moe, final_norm=(l == depth - 1))
    return x2.reshape(B, S, D)
```

```python
import functools
import math

import jax
import jax.numpy as jnp
from jax import lax
from jax.experimental import pallas as pl
from jax.experimental.pallas import tpu as pltpu

F32 = jnp.float32
BF16 = jnp.bfloat16

D_MODEL = 2048
SGU_GROUPS = 8
SGU_WIDTH = 1024
SGU_GROUP_DIM = SGU_WIDTH // SGU_GROUPS
CHUNK = 128
DIFF_HEADS = 8
DIFF_HEAD_DIM = 64
DIFF_V_DIM = 2 * DIFF_HEAD_DIM
DIFF_QK_WIDTH = DIFF_HEADS * 2 * DIFF_HEAD_DIM
DIFF_WIDTH = DIFF_HEADS * DIFF_V_DIM
N_BRANCHES = 2
IN_COLS = 2 * SGU_WIDTH + 2 * DIFF_QK_WIDTH + DIFF_WIDTH + N_BRANCHES * D_MODEL
N_GROUPS = 4
EXPERTS_PER_GROUP = 4
N_EXPERTS = N_GROUPS * EXPERTS_PER_GROUP
EXPERT_FF = 512
RMS_EPS = 1e-6
LN_EPS = 1e-5

Q_COL128 = (2 * SGU_WIDTH) // 128
K_COL128 = (2 * SGU_WIDTH + DIFF_QK_WIDTH) // 128
V_COL128 = (2 * SGU_WIDTH + 2 * DIFF_QK_WIDTH) // 128
GATE_COL1024 = (2 * SGU_WIDTH + 2 * DIFF_QK_WIDTH + DIFF_WIDTH) // 1024

LANES = 128
ROUTER_COLS = LANES
VMEM_LIMIT = 56 * 1024 * 1024


def _params(semantics):
    return pltpu.CompilerParams(dimension_semantics=semantics, vmem_limit_bytes=VMEM_LIMIT)


def _gelu(x):
    return 0.5 * x * (1.0 + jnp.tanh(0.7978845608028654 * (x + 0.044715 * (x * x * x))))


def _rms_scale(x):
    return lax.rsqrt(jnp.mean(x * x, axis=-1, keepdims=True) + RMS_EPS)


def _inproj_kernel(x_ref, g_ref, w_ref, z_ref, h_ref):
    @pl.when(pl.program_id(1) == 0)
    def _():
        x = x_ref[...]
        h_ref[...] = (x * _rms_scale(x) * g_ref[...]).astype(BF16)

    z_ref[...] = jnp.dot(h_ref[...], w_ref[...], preferred_element_type=F32).astype(BF16)


def _inproj(x2, g, w_bf16, tm, tn):
    T, D = x2.shape
    N = w_bf16.shape[1]
    return pl.pallas_call(
        _inproj_kernel,
        grid=(T // tm, N // tn),
        in_specs=[
            pl.BlockSpec((tm, D), lambda i, j: (i, 0)),
            pl.BlockSpec((1, D), lambda i, j: (0, 0)),
            pl.BlockSpec((D, tn), lambda i, j: (0, j)),
        ],
        out_specs=pl.BlockSpec((tm, tn), lambda i, j: (i, j)),
        out_shape=jax.ShapeDtypeStruct((T, N), BF16),
        scratch_shapes=[pltpu.VMEM((tm, D), BF16)],
        compiler_params=_params(("parallel", "arbitrary")),
        name="inproj",
    )(x2, g, w_bf16)


def _sgu_kernel(u_ref, v_ref, lng_ref, lnb_ref, ws_ref, bs_ref, a_ref, *, chunks):
    v = _gelu(v_ref[...].astype(F32))
    mu = jnp.mean(v, axis=-1, keepdims=True)
    vc = v - mu
    var = jnp.mean(vc * vc, axis=-1, keepdims=True)
    vn = (vc * lax.rsqrt(var + LN_EPS) * lng_ref[...] + lnb_ref[...]).astype(BF16)
    for c in range(chunks):
        rows = slice(c * CHUNK, (c + 1) * CHUNK)
        for g in range(SGU_GROUPS):
            cols = slice(g * SGU_GROUP_DIM, (g + 1) * SGU_GROUP_DIM)
            mixed = jnp.dot(ws_ref[g], vn[rows, cols], preferred_element_type=F32)
            mixed = mixed + bs_ref[g]
            u = _gelu(u_ref[rows, cols].astype(F32))
            a_ref[rows, cols] = (u * mixed).astype(BF16)


def _sgu(z, ln_g, ln_b, ws_bf16, bs_col, tm):
    T = z.shape[0]
    wblk = SGU_WIDTH
    return pl.pallas_call(
        functools.partial(_sgu_kernel, chunks=tm // CHUNK),
        grid=(T // tm,),
        in_specs=[
            pl.BlockSpec((tm, wblk), lambda i: (i, 0)),
            pl.BlockSpec((tm, wblk), lambda i: (i, 1)),
            pl.BlockSpec((1, wblk), lambda i: (0, 0)),
            pl.BlockSpec((1, wblk), lambda i: (0, 0)),
            pl.BlockSpec((SGU_GROUPS, CHUNK, CHUNK), lambda i: (0, 0, 0)),
            pl.BlockSpec((SGU_GROUPS, CHUNK, 1), lambda i: (0, 0, 0)),
        ],
        out_specs=pl.BlockSpec((tm, wblk), lambda i: (i, 0)),
        out_shape=jax.ShapeDtypeStruct((T, SGU_WIDTH), BF16),
        compiler_params=_params(("parallel",)),
        name="sgu",
    )(z, z, ln_g, ln_b, ws_bf16, bs_col)


LOG2E = 1.4426950408889634
VT_PAD = 16


def _split3(x):
    a1 = x.astype(BF16).astype(F32)
    a2 = (x - a1).astype(BF16).astype(F32)
    a3 = (x - a1 - a2).astype(BF16).astype(F32)
    return a1, a2, a3


def _pick(row, values):
    out = jnp.zeros(values[0].shape, F32)
    for i, v in enumerate(values):
        out = jnp.where(row == i, v, out)
    return out


def _attn_kernel(slopes_ref, lq1_ref, lk1_ref, lq2_ref, lk2_ref, dgt_ref, q_ref, k_ref, v_ref,
                 o_ref, qx_ref, kx_ref, vt_ref, rel_ref, s0_ref, s1_ref, ml0_ref, ml1_ref,
                 m_ref, acc_ref, *, tq, tk, seq, lam_init):
    h = pl.program_id(1)
    qi = pl.program_id(2)
    slope2 = slopes_ref[h] * LOG2E
    hd = DIFF_V_DIM
    n = seq // tk
    q0 = qi * tq
    jd = q0 // tk

    @pl.when(qi == 0)
    def _():
        pad_row = lax.broadcasted_iota(jnp.int32, (VT_PAD, tk), 0)
        ones_pad = jnp.where(pad_row == 0, 1.0, 0.0).astype(BF16)

        def transpose_chunk(c, carry):
            sl = pl.ds(pl.multiple_of(c * tk, tk), tk)
            vt_ref[0:hd, sl] = v_ref[sl, :].astype(F32).T.astype(BF16)
            vt_ref[hd:hd + VT_PAD, sl] = ones_pad
            return carry
        lax.fori_loop(0, n, transpose_chunk, 0)

        c_idx = lax.broadcasted_iota(jnp.int32, (tk, hd), 0)
        k_lane = lax.broadcasted_iota(jnp.int32, (tk, hd), 1)
        a = _split3(jnp.full((tk, hd), slope2, F32))
        c_lo = (c_idx & (LANES - 1)).astype(F32)
        c_hi = (c_idx >> 7).astype(F32)
        kx = _pick(k_lane, [-a[0], -a[1], -a[2], -LANES * a[0], -LANES * a[1], -LANES * a[2],
                            c_lo, c_lo, c_lo, c_hi, c_hi, c_hi])
        kx_ref[...] = kx.astype(BF16)

        x_row = lax.broadcasted_iota(jnp.int32, (hd, 2 * tq), 0)
        r_idx = lax.broadcasted_iota(jnp.int32, (hd, 2 * tq), 1)
        r_idx = jnp.where(r_idx >= tq, r_idx - tq, r_idx)
        r_lo = (r_idx & (LANES - 1)).astype(F32)
        r_hi = (r_idx >> 7).astype(F32)
        a = _split3(jnp.full((hd, 2 * tq), slope2, F32))
        ext = _pick(x_row, [r_lo, r_lo, r_lo, r_hi, r_hi, r_hi,
                            a[0], a[1], a[2], LANES * a[0], LANES * a[1], LANES * a[2]])
        for side, sign in ((0, 1.0), (1, -1.0), (2, 0.0)):
            qx_ref[side, hd:2 * hd, :] = (sign * ext).astype(BF16)

        rel_ref[...] = (lax.broadcasted_iota(jnp.int32, (tk, tq), 0)
                        - lax.broadcasted_iota(jnp.int32, (tk, tq), 1)).astype(F32)

    qt = (q_ref[...].astype(F32) * (DIFF_HEAD_DIM ** -0.5 * LOG2E)).T
    d_row = lax.broadcasted_iota(jnp.int32, (hd, tq), 0)
    q_main = jnp.concatenate([jnp.where(d_row < DIFF_HEAD_DIM, qt, 0.0),
                              jnp.where(d_row >= DIFF_HEAD_DIM, qt, 0.0)], axis=1).astype(BF16)
    for side in range(3):
        qx_ref[side, 0:hd, :] = q_main

    m_ref[...] = jnp.full(m_ref.shape, -jnp.inf, F32)
    acc_ref[...] = jnp.zeros(acc_ref.shape, F32)

    def scores(j, side, s_ref, ml_ref, bias=None):
        rows = pl.ds(pl.multiple_of(j * tk, tk), tk)
        kcx = jnp.concatenate([k_ref[rows, :], kx_ref[...]], axis=1)
        s = jnp.dot(kcx, qx_ref[side], preferred_element_type=F32)
        if bias is not None:
            s = s + jnp.concatenate([bias, bias], axis=1)
        s_ref[...] = s
        ml_ref[...] = jnp.max(s, axis=0, keepdims=True)

    def accumulate(j, cst, s_ref, ml_ref):
        m_prev = m_ref[...]
        m_new = jnp.maximum(m_prev, ml_ref[...] + cst)
        alpha = jnp.exp2(m_prev - m_new)
        p = jnp.exp2(s_ref[...] - (m_new - cst)).astype(BF16)
        cols = pl.ds(pl.multiple_of(j * tk, tk), tk)
        acc_ref[...] = alpha * acc_ref[...] + jnp.dot(vt_ref[:, cols], p,
                                                      preferred_element_type=F32)
        m_ref[...] = m_new

    def chunk(t):
        side = (t >= jd).astype(jnp.int32)
        j = t + side
        sign = (1 - 2 * side).astype(F32)
        cst = -sign * slope2 * (q0 - j * tk).astype(F32)
        return j, side, cst

    diag_bias = -slope2 * jnp.abs(rel_ref[...] + (jd * tk - q0).astype(F32))
    scores(jd, 2, s0_ref, ml0_ref, diag_bias)

    if n > 1:
        ja0, sa0, _ = chunk(jnp.int32(0))
        scores(ja0, sa0, s1_ref, ml1_ref)
        accumulate(jd, 0.0, s0_ref, ml0_ref)

        def pair(i, carry):
            ja, _, ca = chunk(2 * i)
            jb, sb, cb = chunk(2 * i + 1)
            jc, sc, _ = chunk(2 * i + 2)
            scores(jb, sb, s0_ref, ml0_ref)
            accumulate(ja, ca, s1_ref, ml1_ref)
            scores(jc, sc, s1_ref, ml1_ref)
            accumulate(jb, cb, s0_ref, ml0_ref)
            return carry

        lax.fori_loop(0, (n - 2) // 2, pair, 0)
        jl, _, cl = chunk(jnp.int32(n - 2))
        accumulate(jl, cl, s1_ref, ml1_ref)
    else:
        accumulate(jd, 0.0, s0_ref, ml0_ref)

    lam = (jnp.exp(jnp.sum(lq1_ref[...] * lk1_ref[...], axis=-1, keepdims=True))
           - jnp.exp(jnp.sum(lq2_ref[...] * lk2_ref[...], axis=-1, keepdims=True))
           + lam_init)
    acc = acc_ref[...]
    out = acc[0:hd, :] * (1.0 / acc[hd:hd + 1, :])
    o = out[:, 0:tq] - lam * out[:, tq:2 * tq]
    o = o * lax.rsqrt(jnp.mean(o * o, axis=0, keepdims=True) + RMS_EPS)
    o = o * dgt_ref[...] * (1.0 - lam_init)
    o_ref[...] = o.T.astype(BF16)


def _attention(z, slopes, lq1, lk1, lq2, lk2, dg, batch, seq, tq, tk, lam_init):
    T = z.shape[0]
    nq = seq // tq
    hd = DIFF_V_DIM
    lam_spec = pl.BlockSpec((1, DIFF_HEAD_DIM), lambda b, h, i, s: (0, 0))
    grid_spec = pltpu.PrefetchScalarGridSpec(
        num_scalar_prefetch=1,
        grid=(batch, DIFF_HEADS, nq),
        in_specs=[
            lam_spec, lam_spec, lam_spec, lam_spec,
            pl.BlockSpec((hd, 1), lambda b, h, i, s: (0, 0)),
            pl.BlockSpec((tq, hd), lambda b, h, i, s: (b * nq + i, Q_COL128 + h)),
            pl.BlockSpec((seq, hd), lambda b, h, i, s: (b, K_COL128 + h)),
            pl.BlockSpec((seq, hd), lambda b, h, i, s: (b, V_COL128 + h)),
        ],
        out_specs=pl.BlockSpec((tq, hd), lambda b, h, i, s: (b * nq + i, h)),
        scratch_shapes=[
            pltpu.VMEM((3, 2 * hd, 2 * tq), BF16),
            pltpu.VMEM((tk, hd), BF16),
            pltpu.VMEM((hd + VT_PAD, seq), BF16),
            pltpu.VMEM((tk, tq), F32),
            pltpu.VMEM((tk, 2 * tq), F32),
            pltpu.VMEM((tk, 2 * tq), F32),
            pltpu.VMEM((1, 2 * tq), F32),
            pltpu.VMEM((1, 2 * tq), F32),
            pltpu.VMEM((1, 2 * tq), F32),
            pltpu.VMEM((hd + VT_PAD, 2 * tq), F32),
        ],
    )
    return pl.pallas_call(
        functools.partial(_attn_kernel, tq=tq, tk=tk, seq=seq, lam_init=lam_init),
        grid_spec=grid_spec,
        out_shape=jax.ShapeDtypeStruct((T, DIFF_WIDTH), BF16),
        compiler_params=_params(("parallel", "parallel", "arbitrary")),
        name="diff_attn",
    )(slopes, lq1, lk1, lq2, lk2, dg, z, z, z)


def _merge_kernel(x_ref, a_ref, b_ref, g00_ref, g01_ref, g10_ref, g11_ref, bg_ref,
                  wa_ref, wb_ref, wo_ref, o_ref, merged_ref):
    gate_refs = ((g00_ref, g01_ref), (g10_ref, g11_ref))
    half = D_MODEL // 2
    a = a_ref[...]
    b = b_ref[...]
    for c in range(2):
        cols = slice(c * half, (c + 1) * half)
        ga = jax.nn.sigmoid(gate_refs[0][c][...].astype(F32) + bg_ref[0:1, cols])
        gb = jax.nn.sigmoid(gate_refs[1][c][...].astype(F32) + bg_ref[1:2, cols])
        pa = jnp.dot(a, wa_ref[:, cols], preferred_element_type=F32)
        pb = jnp.dot(b, wb_ref[:, cols], preferred_element_type=F32)
        merged_ref[:, cols] = (ga * pa + gb * pb).astype(BF16)
    o_ref[...] = x_ref[...] + jnp.dot(merged_ref[...], wo_ref[...], preferred_element_type=F32)


def _merge(x2, a, b, z, b_gate, wa, wb, wo, tm):
    T, D = x2.shape
    half = D // 2

    def gate_spec(k):
        return pl.BlockSpec((tm, half), lambda i: (i, GATE_COL1024 + k))

    def const_spec(shape):
        return pl.BlockSpec(shape, lambda i: (0, 0), pipeline_mode=pl.Buffered(1))

    return pl.pallas_call(
        _merge_kernel,
        grid=(T // tm,),
        in_specs=[
            pl.BlockSpec((tm, D), lambda i: (i, 0)),
            pl.BlockSpec((tm, SGU_WIDTH), lambda i: (i, 0)),
            pl.BlockSpec((tm, DIFF_WIDTH), lambda i: (i, 0)),
            gate_spec(0), gate_spec(1), gate_spec(2), gate_spec(3),
            const_spec((N_BRANCHES, D)),
            const_spec((SGU_WIDTH, D)),
            const_spec((DIFF_WIDTH, D)),
            const_spec((D, D)),
        ],
        out_specs=pl.BlockSpec((tm, D), lambda i: (i, 0)),
        out_shape=jax.ShapeDtypeStruct((T, D), F32),
        scratch_shapes=[pltpu.VMEM((tm, D), BF16)],
        compiler_params=_params(("parallel",)),
        name="merge_out",
    )(x2, a, b, z, z, z, z, b_gate, wa, wb, wo)


def _split_bf16(x):
    hi = x.astype(BF16)
    lo = (x - hi.astype(F32)).astype(BF16)
    return hi, lo


def _dot3(x_hi, x_lo, w_hi, w_lo):
    return (jnp.dot(x_hi, w_hi, preferred_element_type=F32)
            + jnp.dot(x_lo, w_hi, preferred_element_type=F32)
            + jnp.dot(x_hi, w_lo, preferred_element_type=F32))


def _router(h, rw_hi_ref, rw_lo_ref, rb_ref):
    tm = h.shape[0]
    h_hi, h_lo = _split_bf16(h)
    logits = _dot3(h_hi, h_lo, rw_hi_ref[...], rw_lo_ref[...]) + rb_ref[...]
    lane = lax.broadcasted_iota(jnp.int32, (tm, ROUTER_COLS), 1)
    neg = jnp.float32(-jnp.inf)
    is_group = lane < N_GROUPS
    gl = jnp.where(is_group, logits, neg)
    gmax = jnp.max(gl, axis=-1, keepdims=True)
    gexp = jnp.exp(gl - gmax)
    g_w = 1.0 / jnp.sum(gexp, axis=-1, keepdims=True)
    g_idx = jnp.min(jnp.where(gl == gmax, lane, ROUTER_COLS), axis=-1, keepdims=True)
    e_lo = N_GROUPS + g_idx * EXPERTS_PER_GROUP
    in_group = (lane >= e_lo) & (lane < e_lo + EXPERTS_PER_GROUP)
    el = jnp.where(in_group, logits, neg)
    m1 = jnp.max(el, axis=-1, keepdims=True)
    i1 = jnp.min(jnp.where(el == m1, lane, ROUTER_COLS), axis=-1, keepdims=True)
    el2 = jnp.where(lane == i1, neg, el)
    m2 = jnp.max(el2, axis=-1, keepdims=True)
    i2 = jnp.min(jnp.where(el2 == m2, lane, ROUTER_COLS), axis=-1, keepdims=True)
    t = jnp.exp(m2 - m1)
    w1 = g_w / (1.0 + t)
    w2 = g_w * t / (1.0 + t)
    comb = jnp.where(lane == i1, w1, 0.0) + jnp.where(lane == i2, w2, 0.0)
    return comb, g_idx


META_COLS = LANES


def _route_kernel(x_ref, g_ref, rw_hi_ref, rw_lo_ref, rb_ref, xr_ref, meta_ref, cnt_ref,
                  carry_ref, tri_ref):
    tm, D = x_ref.shape

    @pl.when(pl.program_id(0) == 0)
    def _():
        carry_ref[...] = jnp.zeros(carry_ref.shape, F32)
        r = lax.broadcasted_iota(jnp.int32, (tm, tm), 0)
        c = lax.broadcasted_iota(jnp.int32, (tm, tm), 1)
        tri_ref[...] = jnp.where(c < r, 1.0, 0.0).astype(BF16)

    x = x_ref[...]
    h = x * _rms_scale(x) * g_ref[...]
    comb, g_idx = _router(h, rw_hi_ref, rw_lo_ref, rb_ref)
    xr_ref[:, 0:D] = x
    xr_ref[:, D:D + ROUTER_COLS] = comb

    lane = lax.broadcasted_iota(jnp.int32, (tm, META_COLS), 1)
    onehot = jnp.where(lane == g_idx, 1.0, 0.0)
    earlier = jnp.dot(tri_ref[...], onehot.astype(BF16), preferred_element_type=F32)
    carry = carry_ref[...]
    rank = jnp.sum((earlier + carry) * onehot, axis=-1, keepdims=True).astype(jnp.int32)
    meta_ref[...] = jnp.where(lane == 0, g_idx, jnp.where(lane == 1, rank, 0))
    carry = carry + jnp.sum(onehot, axis=0, keepdims=True)
    carry_ref[...] = carry
    cnt_ref[...] = carry.astype(jnp.int32)


def _route(x2, g, rw_hi, rw_lo, rb, tm):
    T, D = x2.shape
    row = pl.BlockSpec((1, D), lambda i: (0, 0))
    rspec = pl.BlockSpec((D, ROUTER_COLS), lambda i: (0, 0))
    return pl.pallas_call(
        _route_kernel,
        grid=(T // tm,),
        in_specs=[pl.BlockSpec((tm, D), lambda i: (i, 0)), row, rspec, rspec,
                  pl.BlockSpec((1, ROUTER_COLS), lambda i: (0, 0))],
        out_specs=[pl.BlockSpec((tm, D + ROUTER_COLS), lambda i: (i, 0)),
                   pl.BlockSpec((tm, META_COLS), lambda i: (i, 0)),
                   pl.BlockSpec((1, META_COLS), lambda i: (0, 0))],
        out_shape=[jax.ShapeDtypeStruct((T, D + ROUTER_COLS), F32),
                   jax.ShapeDtypeStruct((T, META_COLS), jnp.int32),
                   jax.ShapeDtypeStruct((1, META_COLS), jnp.int32)],
        scratch_shapes=[pltpu.VMEM((1, META_COLS), F32), pltpu.VMEM((tm, tm), BF16)],
        compiler_params=_params(("arbitrary",)),
        name="moe_route",
    )(x2, g, rw_hi, rw_lo, rb)


def _invert_kernel(pos_ref, tok_ref):
    def clear(s, carry):
        tok_ref[s] = -1
        return carry
    lax.fori_loop(0, tok_ref.shape[0], clear, 0)

    def put(t, carry):
        tok_ref[pos_ref[t]] = t
        return carry
    lax.fori_loop(0, pos_ref.shape[0], put, 0)


def _invert(pos, n_slots):
    return pl.pallas_call(
        _invert_kernel,
        in_specs=[pl.BlockSpec(memory_space=pltpu.SMEM)],
        out_specs=pl.BlockSpec(memory_space=pltpu.SMEM),
        out_shape=jax.ShapeDtypeStruct((n_slots,), jnp.int32),
        name="moe_invert",
    )(pos)


def _experts_kernel(tok_ref, tgroup_ref, ntiles_ref, xr_hbm, g_ref, w1_ref, w3_ref, w2_ref,
                    fg_ref, out_hbm, xbuf, obuf, h_ref, acc_ref, gsem, ssem, *, rows, final_norm):
    i = pl.program_id(0)
    e = pl.program_id(1)
    ntiles = ntiles_ref[0]
    slot = i % 2
    live = i < ntiles
    D = out_hbm.shape[1]

    def gather_copy(tile, r, sl):
        tok = jnp.maximum(tok_ref[tile * rows + r], 0)
        return pltpu.make_async_copy(xr_hbm.at[pl.ds(tok, 1), :],
                                     xbuf.at[sl, pl.ds(r, 1), :], gsem.at[sl])

    def scatter_copy(tile, r, sl):
        tok = tok_ref[tile * rows + r]
        copy = pltpu.make_async_copy(obuf.at[sl, pl.ds(r, 1), :],
                                     out_hbm.at[pl.ds(jnp.maximum(tok, 0), 1), :], ssem.at[sl])
        return tok, copy

    def gather_start(tile, sl):
        def body(r, carry):
            gather_copy(tile, r, sl).start()
            return carry
        lax.fori_loop(0, rows, body, 0, unroll=8)

    def gather_wait(tile, sl):
        def body(r, carry):
            gather_copy(tile, r, sl).wait()
            return carry
        lax.fori_loop(0, rows, body, 0, unroll=8)

    def scatter_start(tile, sl):
        def body(r, carry):
            tok, copy = scatter_copy(tile, r, sl)

            @pl.when(tok >= 0)
            def _():
                copy.start()
            return carry
        lax.fori_loop(0, rows, body, 0, unroll=8)

    def scatter_wait(tile, sl):
        def body(r, carry):
            tok, copy = scatter_copy(tile, r, sl)

            @pl.when(tok >= 0)
            def _():
                copy.wait()
            return carry
        lax.fori_loop(0, rows, body, 0, unroll=8)

    @pl.when(jnp.logical_and(e == 0, live))
    def _():
        @pl.when(i == 0)
        def _():
            gather_start(0, 0)
        gather_wait(i, slot)

        @pl.when(i + 1 < ntiles)
        def _():
            gather_start(i + 1, 1 - slot)
        x = xbuf[slot, :, 0:D]
        h_ref[...] = (x * _rms_scale(x) * g_ref[...]).astype(BF16)
        acc_ref[...] = x

    @pl.when(live)
    def _():
        h = h_ref[...]
        hid = jax.nn.silu(jnp.dot(h, w1_ref[0], preferred_element_type=F32)) \
            * jnp.dot(h, w3_ref[0], preferred_element_type=F32)
        lane = lax.broadcasted_iota(jnp.int32, (rows, ROUTER_COLS), 1)
        col = N_GROUPS + tgroup_ref[i] * EXPERTS_PER_GROUP + e
        c_e = jnp.sum(jnp.where(lane == col, xbuf[slot, :, D:D + ROUTER_COLS], 0.0),
                      axis=-1, keepdims=True)
        acc_ref[...] += c_e * jnp.dot(hid.astype(BF16), w2_ref[0], preferred_element_type=F32)

    @pl.when(jnp.logical_and(e == EXPERTS_PER_GROUP - 1, live))
    def _():
        @pl.when(i >= 2)
        def _():
            scatter_wait(i - 2, slot)
        y = acc_ref[...]
        if final_norm:
            y = y * _rms_scale(y) * fg_ref[...]
        obuf[slot] = y
        scatter_start(i, slot)

        @pl.when(i == ntiles - 1)
        def _():
            @pl.when(i >= 1)
            def _():
                scatter_wait(i - 1, 1 - slot)
            scatter_wait(i, slot)


def _experts(xr, toks, tile_group, ntiles, g, w1, w3, w2, final_g, rows, final_norm):
    T = xr.shape[0]
    D = D_MODEL
    max_tiles = tile_group.shape[0]

    def expert_block(i, e, tok, tg, nt):
        return (tg[jnp.minimum(i, nt[0] - 1)] * EXPERTS_PER_GROUP + e, 0, 0)

    row = pl.BlockSpec((1, D), lambda i, e, tok, tg, nt: (0, 0))
    grid_spec = pltpu.PrefetchScalarGridSpec(
        num_scalar_prefetch=3,
        grid=(max_tiles, EXPERTS_PER_GROUP),
        in_specs=[
            pl.BlockSpec(memory_space=pl.ANY),
            row,
            pl.BlockSpec((1, D, EXPERT_FF), expert_block),
            pl.BlockSpec((1, D, EXPERT_FF), expert_block),
            pl.BlockSpec((1, EXPERT_FF, D), expert_block),
            row,
        ],
        out_specs=pl.BlockSpec(memory_space=pl.ANY),
        scratch_shapes=[
            pltpu.VMEM((2, rows, D + ROUTER_COLS), F32),
            pltpu.VMEM((2, rows, D), F32),
            pltpu.VMEM((rows, D), BF16),
            pltpu.VMEM((rows, D), F32),
            pltpu.SemaphoreType.DMA((2,)),
            pltpu.SemaphoreType.DMA((2,)),
        ],
    )
    return pl.pallas_call(
        functools.partial(_experts_kernel, rows=rows, final_norm=final_norm),
        grid_spec=grid_spec,
        out_shape=jax.ShapeDtypeStruct((T, D), F32),
        compiler_params=_params(("arbitrary", "arbitrary")),
        name="moe_experts",
    )(toks, tile_group, ntiles, xr, g, w1, w3, w2, final_g)


def _moe(x2, g, rw_hi, rw_lo, rb, w1, w3, w2, final_g, tm, final_norm):
    T = x2.shape[0]
    rows = tm
    max_tiles = T // rows + N_GROUPS
    xr, meta, counts = _route(x2, g, rw_hi, rw_lo, rb, tm)
    cnt = counts[0, :N_GROUPS]
    padded = (cnt + rows - 1) // rows * rows
    ends = jnp.cumsum(padded)
    pos = (ends - padded)[meta[:, 0]] + meta[:, 1]
    ntiles = (ends[-1] // rows).reshape(1)
    tile_start = jnp.arange(max_tiles, dtype=jnp.int32) * rows
    tile_group = jnp.minimum(jnp.sum(tile_start[:, None] >= ends[None, :], axis=1),
                             N_GROUPS - 1).astype(jnp.int32)
    toks = _invert(pos.astype(jnp.int32), max_tiles * rows)
    return _experts(xr, toks, tile_group, ntiles.astype(jnp.int32), g, w1, w3, w2, final_g,
                    rows, final_norm)


def _router_weights(rg_w, rg_b, re_w, re_b):
    D = rg_w.shape[0]
    w = jnp.zeros((D, ROUTER_COLS), F32)
    w = w.at[:, :N_GROUPS].set(rg_w).at[:, N_GROUPS:N_GROUPS + N_EXPERTS].set(re_w)
    b = jnp.zeros((1, ROUTER_COLS), F32)
    b = b.at[0, :N_GROUPS].set(rg_b).at[0, N_GROUPS:N_GROUPS + N_EXPERTS].set(re_b)
    hi = w.astype(BF16)
    lo = (w - hi.astype(F32)).astype(BF16)
    return hi, lo, b


def kernel(x, norm1_g, w_in, b_gate, sgu_ln_g, sgu_ln_b, sgu_w, sgu_b, lam_q1, lam_k1, lam_q2,
           lam_k2, diff_norm_g, w_proj_a, w_proj_b, w_out, norm2_g, router_g_w, router_g_b,
           router_e_w, router_e_b, w1, w3, w2, final_g):
    B, S, D = x.shape
    assert D == D_MODEL and w_in.shape[2] == IN_COLS
    depth = w_in.shape[0]
    T = B * S
    slopes = jnp.exp2(-8.0 * jnp.arange(1, DIFF_HEADS + 1, dtype=F32) / DIFF_HEADS)

    tm_in = min(1024, T)
    tm_sgu = min(512, T)
    tq = min(512, S)
    tk = min(1024, S)
    tm_merge = min(256, T)
    tm_moe = min(512, T)

    x2 = x.reshape(T, D)
    for l in range(depth):
        lam_init = 0.8 - 0.6 * math.exp(-0.3 * l)
        z = _inproj(x2, norm1_g[l][None], w_in[l].astype(BF16), tm_in, 512)
        a = _sgu(z, sgu_ln_g[l][None], sgu_ln_b[l][None], sgu_w[l].astype(BF16),
                 sgu_b[l][:, :, None], tm_sgu)
        o = _attention(z, slopes, lam_q1[l][None], lam_k1[l][None], lam_q2[l][None],
                       lam_k2[l][None], diff_norm_g[l][:, None], B, S, tq, tk, lam_init)
        x2 = _merge(x2, a, o, z, b_gate[l], w_proj_a[l].astype(BF16), w_proj_b[l].astype(BF16),
                    w_out[l].astype(BF16), tm_merge)
        rw_hi, rw_lo, rb = _router_weights(router_g_w[l], router_g_b[l], router_e_w[l],
                                           router_e_b[l])
        x2 = _moe(x2, norm2_g[l][None], rw_hi, rw_lo, rb, w1[l].astype(BF16), w3[l].astype(BF16),
                  w2[l].astype(BF16), final_g[None], tm_moe, final_norm=(l == depth - 1))
    return x2.reshape(B, S, D)
```

```python
import functools
import math

import jax
import jax.numpy as jnp
from jax import lax
from jax.experimental import pallas as pl
from jax.experimental.pallas import tpu as pltpu

F32 = jnp.float32
BF16 = jnp.bfloat16

D_MODEL = 2048
SGU_GROUPS = 8
SGU_WIDTH = 1024
SGU_GROUP_DIM = SGU_WIDTH // SGU_GROUPS
CHUNK = 128
DIFF_HEADS = 8
DIFF_HEAD_DIM = 64
DIFF_V_DIM = 2 * DIFF_HEAD_DIM
DIFF_QK_WIDTH = DIFF_HEADS * 2 * DIFF_HEAD_DIM
DIFF_WIDTH = DIFF_HEADS * DIFF_V_DIM
N_BRANCHES = 2
IN_COLS = 2 * SGU_WIDTH + 2 * DIFF_QK_WIDTH + DIFF_WIDTH + N_BRANCHES * D_MODEL
N_GROUPS = 4
EXPERTS_PER_GROUP = 4
N_EXPERTS = N_GROUPS * EXPERTS_PER_GROUP
EXPERT_FF = 512
RMS_EPS = 1e-6
LN_EPS = 1e-5

Q_COL128 = (2 * SGU_WIDTH) // 128
K_COL128 = (2 * SGU_WIDTH + DIFF_QK_WIDTH) // 128
V_COL128 = (2 * SGU_WIDTH + 2 * DIFF_QK_WIDTH) // 128
GATE_COL1024 = (2 * SGU_WIDTH + 2 * DIFF_QK_WIDTH + DIFF_WIDTH) // 1024

LANES = 128
ROUTER_COLS = LANES
VMEM_LIMIT = 56 * 1024 * 1024


def _params(semantics):
    return pltpu.CompilerParams(dimension_semantics=semantics, vmem_limit_bytes=VMEM_LIMIT)


def _gelu(x):
    return 0.5 * x * (1.0 + jnp.tanh(0.7978845608028654 * (x + 0.044715 * (x * x * x))))


def _rms_scale(x):
    return lax.rsqrt(jnp.mean(x * x, axis=-1, keepdims=True) + RMS_EPS)


def _inproj_kernel(x_ref, g_ref, w_ref, z_ref, h_ref):
    @pl.when(pl.program_id(1) == 0)
    def _():
        x = x_ref[...]
        h_ref[...] = (x * _rms_scale(x) * g_ref[...]).astype(BF16)

    z_ref[...] = jnp.dot(h_ref[...], w_ref[...], preferred_element_type=F32).astype(BF16)


def _inproj(x2, g, w_bf16, tm, tn):
    T, D = x2.shape
    N = w_bf16.shape[1]
    return pl.pallas_call(
        _inproj_kernel,
        grid=(T // tm, N // tn),
        in_specs=[
            pl.BlockSpec((tm, D), lambda i, j: (i, 0)),
            pl.BlockSpec((1, D), lambda i, j: (0, 0)),
            pl.BlockSpec((D, tn), lambda i, j: (0, j)),
        ],
        out_specs=pl.BlockSpec((tm, tn), lambda i, j: (i, j)),
        out_shape=jax.ShapeDtypeStruct((T, N), BF16),
        scratch_shapes=[pltpu.VMEM((tm, D), BF16)],
        compiler_params=_params(("parallel", "arbitrary")),
        name="inproj",
    )(x2, g, w_bf16)


def _sgu_kernel(u_ref, v_ref, lng_ref, lnb_ref, ws_ref, bs_ref, a_ref, *, chunks):
    v = _gelu(v_ref[...].astype(F32))
    mu = jnp.mean(v, axis=-1, keepdims=True)
    vc = v - mu
    var = jnp.mean(vc * vc, axis=-1, keepdims=True)
    vn = (vc * lax.rsqrt(var + LN_EPS) * lng_ref[...] + lnb_ref[...]).astype(BF16)
    for c in range(chunks):
        rows = slice(c * CHUNK, (c + 1) * CHUNK)
        for g in range(SGU_GROUPS):
            cols = slice(g * SGU_GROUP_DIM, (g + 1) * SGU_GROUP_DIM)
            mixed = jnp.dot(ws_ref[g], vn[rows, cols], preferred_element_type=F32)
            mixed = mixed + bs_ref[g]
            u = _gelu(u_ref[rows, cols].astype(F32))
            a_ref[rows, cols] = (u * mixed).astype(BF16)


def _sgu(z, ln_g, ln_b, ws_bf16, bs_col, tm):
    T = z.shape[0]
    wblk = SGU_WIDTH
    return pl.pallas_call(
        functools.partial(_sgu_kernel, chunks=tm // CHUNK),
        grid=(T // tm,),
        in_specs=[
            pl.BlockSpec((tm, wblk), lambda i: (i, 0)),
            pl.BlockSpec((tm, wblk), lambda i: (i, 1)),
            pl.BlockSpec((1, wblk), lambda i: (0, 0)),
            pl.BlockSpec((1, wblk), lambda i: (0, 0)),
            pl.BlockSpec((SGU_GROUPS, CHUNK, CHUNK), lambda i: (0, 0, 0)),
            pl.BlockSpec((SGU_GROUPS, CHUNK, 1), lambda i: (0, 0, 0)),
        ],
        out_specs=pl.BlockSpec((tm, wblk), lambda i: (i, 0)),
        out_shape=jax.ShapeDtypeStruct((T, SGU_WIDTH), BF16),
        compiler_params=_params(("parallel",)),
        name="sgu",
    )(z, z, ln_g, ln_b, ws_bf16, bs_col)


LOG2E = 1.4426950408889634
VT_PAD = 16


def _split3(x):
    a1 = x.astype(BF16).astype(F32)
    a2 = (x - a1).astype(BF16).astype(F32)
    a3 = (x - a1 - a2).astype(BF16).astype(F32)
    return a1, a2, a3


def _pick(row, values):
    out = jnp.zeros(values[0].shape, F32)
    for i, v in enumerate(values):
        out = jnp.where(row == i, v, out)
    return out


def _attn_kernel(slopes_ref, lq1_ref, lk1_ref, lq2_ref, lk2_ref, dgt_ref, q_ref, k_ref, v_ref,
                 o_ref, qx_ref, kx_ref, vt_ref, rel_ref, s0_ref, s1_ref, ml0_ref, ml1_ref,
                 m_ref, acc_ref, *, tq, tk, seq, lam_init):
    h = pl.program_id(1)
    qi = pl.program_id(2)
    slope2 = slopes_ref[h] * LOG2E
    hd = DIFF_V_DIM
    n = seq // tk
    q0 = qi * tq
    jd = q0 // tk

    @pl.when(qi == 0)
    def _():
        pad_row = lax.broadcasted_iota(jnp.int32, (VT_PAD, tk), 0)
        ones_pad = jnp.where(pad_row == 0, 1.0, 0.0).astype(BF16)

        def transpose_chunk(c, carry):
            sl = pl.ds(pl.multiple_of(c * tk, tk), tk)
            vt_ref[0:hd, sl] = v_ref[sl, :].astype(F32).T.astype(BF16)
            vt_ref[hd:hd + VT_PAD, sl] = ones_pad
            return carry
        lax.fori_loop(0, n, transpose_chunk, 0)

        c_idx = lax.broadcasted_iota(jnp.int32, (tk, hd), 0)
        k_lane = lax.broadcasted_iota(jnp.int32, (tk, hd), 1)
        a = _split3(jnp.full((tk, hd), slope2, F32))
        c_lo = (c_idx & (LANES - 1)).astype(F32)
        c_hi = (c_idx >> 7).astype(F32)
        kx = _pick(k_lane, [-a[0], -a[1], -a[2], -LANES * a[0], -LANES * a[1], -LANES * a[2],
                            c_lo, c_lo, c_lo, c_hi, c_hi, c_hi])
        kx_ref[...] = kx.astype(BF16)

        x_row = lax.broadcasted_iota(jnp.int32, (hd, 2 * tq), 0)
        r_idx = lax.broadcasted_iota(jnp.int32, (hd, 2 * tq), 1)
        r_idx = jnp.where(r_idx >= tq, r_idx - tq, r_idx)
        r_lo = (r_idx & (LANES - 1)).astype(F32)
        r_hi = (r_idx >> 7).astype(F32)
        a = _split3(jnp.full((hd, 2 * tq), slope2, F32))
        ext = _pick(x_row, [r_lo, r_lo, r_lo, r_hi, r_hi, r_hi,
                            a[0], a[1], a[2], LANES * a[0], LANES * a[1], LANES * a[2]])
        for side, sign in ((0, 1.0), (1, -1.0), (2, 0.0)):
            qx_ref[side, hd:2 * hd, :] = (sign * ext).astype(BF16)

        rel_ref[...] = (lax.broadcasted_iota(jnp.int32, (tk, tq), 0)
                        - lax.broadcasted_iota(jnp.int32, (tk, tq), 1)).astype(F32)

    qt = (q_ref[...].astype(F32) * (DIFF_HEAD_DIM ** -0.5 * LOG2E)).T
    d_row = lax.broadcasted_iota(jnp.int32, (hd, tq), 0)
    q_main = jnp.concatenate([jnp.where(d_row < DIFF_HEAD_DIM, qt, 0.0),
                              jnp.where(d_row >= DIFF_HEAD_DIM, qt, 0.0)], axis=1).astype(BF16)
    for side in range(3):
        qx_ref[side, 0:hd, :] = q_main

    m_ref[...] = jnp.full(m_ref.shape, -jnp.inf, F32)
    acc_ref[...] = jnp.zeros(acc_ref.shape, F32)

    def scores(j, side, s_ref, ml_ref, bias=None):
        rows = pl.ds(pl.multiple_of(j * tk, tk), tk)
        kcx = jnp.concatenate([k_ref[rows, :], kx_ref[...]], axis=1)
        s = jnp.dot(kcx, qx_ref[side], preferred_element_type=F32)
        if bias is not None:
            s = s + jnp.concatenate([bias, bias], axis=1)
        s_ref[...] = s
        ml_ref[...] = jnp.max(s, axis=0, keepdims=True)

    def accumulate(j, cst, s_ref, ml_ref):
        m_prev = m_ref[...]
        m_new = jnp.maximum(m_prev, ml_ref[...] + cst)
        alpha = jnp.exp2(m_prev - m_new)
        p = jnp.exp2(s_ref[...] - (m_new - cst)).astype(BF16)
        cols = pl.ds(pl.multiple_of(j * tk, tk), tk)
        acc_ref[...] = alpha * acc_ref[...] + jnp.dot(vt_ref[:, cols], p,
                                                      preferred_element_type=F32)
        m_ref[...] = m_new

    def chunk(t):
        side = (t >= jd).astype(jnp.int32)
        j = t + side
        sign = (1 - 2 * side).astype(F32)
        cst = -sign * slope2 * (q0 - j * tk).astype(F32)
        return j, side, cst

    diag_bias = -slope2 * jnp.abs(rel_ref[...] + (jd * tk - q0).astype(F32))
    scores(jd, 2, s0_ref, ml0_ref, diag_bias)

    if n > 1:
        ja0, sa0, _ = chunk(jnp.int32(0))
        scores(ja0, sa0, s1_ref, ml1_ref)
        accumulate(jd, 0.0, s0_ref, ml0_ref)

        def pair(i, carry):
            ja, _, ca = chunk(2 * i)
            jb, sb, cb = chunk(2 * i + 1)
            jc, sc, _ = chunk(2 * i + 2)
            scores(jb, sb, s0_ref, ml0_ref)
            accumulate(ja, ca, s1_ref, ml1_ref)
            scores(jc, sc, s1_ref, ml1_ref)
            accumulate(jb, cb, s0_ref, ml0_ref)
            return carry

        lax.fori_loop(0, (n - 2) // 2, pair, 0)
        jl, _, cl = chunk(jnp.int32(n - 2))
        accumulate(jl, cl, s1_ref, ml1_ref)
    else:
        accumulate(jd, 0.0, s0_ref, ml0_ref)

    lam = (jnp.exp(jnp.sum(lq1_ref[...] * lk1_ref[...], axis=-1, keepdims=True))
           - jnp.exp(jnp.sum(lq2_ref[...] * lk2_ref[...], axis=-1, keepdims=True))
           + lam_init)
    acc = acc_ref[...]
    out = acc[0:hd, :] * (1.0 / acc[hd:hd + 1, :])
    o = out[:, 0:tq] - lam * out[:, tq:2 * tq]
    o = o * lax.rsqrt(jnp.mean(o * o, axis=0, keepdims=True) + RMS_EPS)
    o = o * dgt_ref[...] * (1.0 - lam_init)
    o_ref[...] = o.T.astype(BF16)


def _attention(z, slopes, lq1, lk1, lq2, lk2, dg, batch, seq, tq, tk, lam_init):
    T = z.shape[0]
    nq = seq // tq
    hd = DIFF_V_DIM
    lam_spec = pl.BlockSpec((1, DIFF_HEAD_DIM), lambda b, h, i, s: (0, 0))
    grid_spec = pltpu.PrefetchScalarGridSpec(
        num_scalar_prefetch=1,
        grid=(batch, DIFF_HEADS, nq),
        in_specs=[
            lam_spec, lam_spec, lam_spec, lam_spec,
            pl.BlockSpec((hd, 1), lambda b, h, i, s: (0, 0)),
            pl.BlockSpec((tq, hd), lambda b, h, i, s: (b * nq + i, Q_COL128 + h)),
            pl.BlockSpec((seq, hd), lambda b, h, i, s: (b, K_COL128 + h)),
            pl.BlockSpec((seq, hd), lambda b, h, i, s: (b, V_COL128 + h)),
        ],
        out_specs=pl.BlockSpec((tq, hd), lambda b, h, i, s: (b * nq + i, h)),
        scratch_shapes=[
            pltpu.VMEM((3, 2 * hd, 2 * tq), BF16),
            pltpu.VMEM((tk, hd), BF16),
            pltpu.VMEM((hd + VT_PAD, seq), BF16),
            pltpu.VMEM((tk, tq), F32),
            pltpu.VMEM((tk, 2 * tq), F32),
            pltpu.VMEM((tk, 2 * tq), F32),
            pltpu.VMEM((1, 2 * tq), F32),
            pltpu.VMEM((1, 2 * tq), F32),
            pltpu.VMEM((1, 2 * tq), F32),
            pltpu.VMEM((hd + VT_PAD, 2 * tq), F32),
        ],
    )
    return pl.pallas_call(
        functools.partial(_attn_kernel, tq=tq, tk=tk, seq=seq, lam_init=lam_init),
        grid_spec=grid_spec,
        out_shape=jax.ShapeDtypeStruct((T, DIFF_WIDTH), BF16),
        compiler_params=_params(("parallel", "parallel", "arbitrary")),
        name="diff_attn",
    )(slopes, lq1, lk1, lq2, lk2, dg, z, z, z)


def _merge_kernel(x_ref, a_ref, b_ref, g00_ref, g01_ref, g10_ref, g11_ref, bg_ref,
                  wa_ref, wb_ref, wo_ref, o_ref, merged_ref):
    gate_refs = ((g00_ref, g01_ref), (g10_ref, g11_ref))
    half = D_MODEL // 2
    a = a_ref[...]
    b = b_ref[...]
    for c in range(2):
        cols = slice(c * half, (c + 1) * half)
        ga = jax.nn.sigmoid(gate_refs[0][c][...].astype(F32) + bg_ref[0:1, cols])
        gb = jax.nn.sigmoid(gate_refs[1][c][...].astype(F32) + bg_ref[1:2, cols])
        pa = jnp.dot(a, wa_ref[:, cols], preferred_element_type=F32)
        pb = jnp.dot(b, wb_ref[:, cols], preferred_element_type=F32)
        merged_ref[:, cols] = (ga * pa + gb * pb).astype(BF16)
    o_ref[...] = x_ref[...] + jnp.dot(merged_ref[...], wo_ref[...], preferred_element_type=F32)


def _merge(x2, a, b, z, b_gate, wa, wb, wo, tm):
    T, D = x2.shape
    half = D // 2

    def gate_spec(k):
        return pl.BlockSpec((tm, half), lambda i: (i, GATE_COL1024 + k))

    def const_spec(shape):
        return pl.BlockSpec(shape, lambda i: (0, 0), pipeline_mode=pl.Buffered(1))

    return pl.pallas_call(
        _merge_kernel,
        grid=(T // tm,),
        in_specs=[
            pl.BlockSpec((tm, D), lambda i: (i, 0)),
            pl.BlockSpec((tm, SGU_WIDTH), lambda i: (i, 0)),
            pl.BlockSpec((tm, DIFF_WIDTH), lambda i: (i, 0)),
            gate_spec(0), gate_spec(1), gate_spec(2), gate_spec(3),
            const_spec((N_BRANCHES, D)),
            const_spec((SGU_WIDTH, D)),
            const_spec((DIFF_WIDTH, D)),
            const_spec((D, D)),
        ],
        out_specs=pl.BlockSpec((tm, D), lambda i: (i, 0)),
        out_shape=jax.ShapeDtypeStruct((T, D), F32),
        scratch_shapes=[pltpu.VMEM((tm, D), BF16)],
        compiler_params=_params(("parallel",)),
        name="merge_out",
    )(x2, a, b, z, z, z, z, b_gate, wa, wb, wo)


def _split_bf16(x):
    hi = x.astype(BF16)
    lo = (x - hi.astype(F32)).astype(BF16)
    return hi, lo


def _dot3(x_hi, x_lo, w_hi, w_lo):
    return (jnp.dot(x_hi, w_hi, preferred_element_type=F32)
            + jnp.dot(x_lo, w_hi, preferred_element_type=F32)
            + jnp.dot(x_hi, w_lo, preferred_element_type=F32))


def _router(h, rw_hi_ref, rw_lo_ref, rb_ref):
    tm = h.shape[0]
    h_hi, h_lo = _split_bf16(h)
    logits = _dot3(h_hi, h_lo, rw_hi_ref[...], rw_lo_ref[...]) + rb_ref[...]
    lane = lax.broadcasted_iota(jnp.int32, (tm, ROUTER_COLS), 1)
    neg = jnp.float32(-jnp.inf)
    is_group = lane < N_GROUPS
    gl = jnp.where(is_group, logits, neg)
    gmax = jnp.max(gl, axis=-1, keepdims=True)
    gexp = jnp.exp(gl - gmax)
    g_w = 1.0 / jnp.sum(gexp, axis=-1, keepdims=True)
    g_idx = jnp.min(jnp.where(gl == gmax, lane, ROUTER_COLS), axis=-1, keepdims=True)
    e_lo = N_GROUPS + g_idx * EXPERTS_PER_GROUP
    in_group = (lane >= e_lo) & (lane < e_lo + EXPERTS_PER_GROUP)
    el = jnp.where(in_group, logits, neg)
    m1 = jnp.max(el, axis=-1, keepdims=True)
    i1 = jnp.min(jnp.where(el == m1, lane, ROUTER_COLS), axis=-1, keepdims=True)
    el2 = jnp.where(lane == i1, neg, el)
    m2 = jnp.max(el2, axis=-1, keepdims=True)
    i2 = jnp.min(jnp.where(el2 == m2, lane, ROUTER_COLS), axis=-1, keepdims=True)
    t = jnp.exp(m2 - m1)
    w1 = g_w / (1.0 + t)
    w2 = g_w * t / (1.0 + t)
    comb = jnp.where(lane == i1, w1, 0.0) + jnp.where(lane == i2, w2, 0.0)
    return comb, g_idx


META_COLS = LANES


def _route_kernel(x_ref, g_ref, rw_hi_ref, rw_lo_ref, rb_ref, xr_ref, meta_ref, cnt_ref,
                  carry_ref, tri_ref):
    tm, D = x_ref.shape

    @pl.when(pl.program_id(0) == 0)
    def _():
        carry_ref[...] = jnp.zeros(carry_ref.shape, F32)
        r = lax.broadcasted_iota(jnp.int32, (tm, tm), 0)
        c = lax.broadcasted_iota(jnp.int32, (tm, tm), 1)
        tri_ref[...] = jnp.where(c < r, 1.0, 0.0).astype(BF16)

    x = x_ref[...]
    h = x * _rms_scale(x) * g_ref[...]
    comb, g_idx = _router(h, rw_hi_ref, rw_lo_ref, rb_ref)
    xr_ref[:, 0:D] = x
    xr_ref[:, D:D + ROUTER_COLS] = comb

    lane = lax.broadcasted_iota(jnp.int32, (tm, META_COLS), 1)
    onehot = jnp.where(lane == g_idx, 1.0, 0.0)
    earlier = jnp.dot(tri_ref[...], onehot.astype(BF16), preferred_element_type=F32)
    carry = carry_ref[...]
    rank = jnp.sum((earlier + carry) * onehot, axis=-1, keepdims=True).astype(jnp.int32)
    meta_ref[...] = jnp.where(lane == 0, g_idx, jnp.where(lane == 1, rank, 0))
    carry = carry + jnp.sum(onehot, axis=0, keepdims=True)
    carry_ref[...] = carry
    cnt_ref[...] = carry.astype(jnp.int32)


def _route(x2, g, rw_hi, rw_lo, rb, tm):
    T, D = x2.shape
    row = pl.BlockSpec((1, D), lambda i: (0, 0))
    rspec = pl.BlockSpec((D, ROUTER_COLS), lambda i: (0, 0))
    return pl.pallas_call(
        _route_kernel,
        grid=(T // tm,),
        in_specs=[pl.BlockSpec((tm, D), lambda i: (i, 0)), row, rspec, rspec,
                  pl.BlockSpec((1, ROUTER_COLS), lambda i: (0, 0))],
        out_specs=[pl.BlockSpec((tm, D + ROUTER_COLS), lambda i: (i, 0)),
                   pl.BlockSpec((tm, META_COLS), lambda i: (i, 0)),
                   pl.BlockSpec((1, META_COLS), lambda i: (0, 0))],
        out_shape=[jax.ShapeDtypeStruct((T, D + ROUTER_COLS), F32),
                   jax.ShapeDtypeStruct((T, META_COLS), jnp.int32),
                   jax.ShapeDtypeStruct((1, META_COLS), jnp.int32)],
        scratch_shapes=[pltpu.VMEM((1, META_COLS), F32), pltpu.VMEM((tm, tm), BF16)],
        compiler_params=_params(("arbitrary",)),
        name="moe_route",
    )(x2, g, rw_hi, rw_lo, rb)


def _invert_kernel(pos_ref, tok_ref):
    def put(t, carry):
        tok_ref[pos_ref[t]] = t
        return carry
    lax.fori_loop(0, pos_ref.shape[0], put, 0, unroll=16)


def _invert(pos):
    return pl.pallas_call(
        _invert_kernel,
        in_specs=[pl.BlockSpec(memory_space=pltpu.SMEM)],
        out_specs=pl.BlockSpec(memory_space=pltpu.SMEM),
        out_shape=jax.ShapeDtypeStruct(pos.shape, jnp.int32),
        name="moe_invert",
    )(pos)


ITEM_FIRST, ITEM_LAST, ITEM_LIVE, ITEM_FINAL = 1, 2, 4, 8


def _experts_kernel(tok_ref, itile_ref, igroup_ref, iflags_ref, xr_hbm, g_ref, w1_ref, w3_ref,
                    w2_ref, fg_ref, out_hbm, xbuf, obuf, h_ref, comb_ref, acc_ref, gsem, ssem,
                    *, rows, final_norm):
    w = pl.program_id(0)
    e = pl.program_id(1)
    slot = w % 2
    tile = itile_ref[w]
    flags = iflags_ref[w]
    live = (flags & ITEM_LIVE) != 0
    D = out_hbm.shape[1]
    part = rows // EXPERTS_PER_GROUP

    def gather_copy(t, r, sl):
        tok = tok_ref[t * rows + r]
        return pltpu.make_async_copy(xr_hbm.at[pl.ds(tok, 1), :],
                                     xbuf.at[sl, pl.ds(r, 1), :], gsem.at[sl])

    def scatter_copy(t, r, sl):
        tok = tok_ref[t * rows + r]
        return pltpu.make_async_copy(obuf.at[sl, pl.ds(r, 1), :],
                                     out_hbm.at[pl.ds(tok, 1), :], ssem.at[sl])

    def for_rows(fn, n=rows):
        def body(r, carry):
            fn(r)
            return carry
        lax.fori_loop(0, n, body, 0, unroll=8)

    @pl.when(e == 0)
    def _():
        @pl.when(w == 0)
        def _():
            for_rows(lambda r: gather_copy(tile, r, slot).start())
        for_rows(lambda r: gather_copy(tile, r, slot).wait())

        @pl.when((flags & ITEM_FIRST) != 0)
        def _():
            x = xbuf[slot, :, 0:D]
            h_ref[...] = (x * _rms_scale(x) * g_ref[...]).astype(BF16)
            comb_ref[...] = xbuf[slot, :, D:D + ROUTER_COLS]
            acc_ref[...] = x

    next_tile = itile_ref[w + 1]

    @pl.when(live)
    def _():
        for r in range(part):
            gather_copy(next_tile, e * part + r, 1 - slot).start()
        h = h_ref[...]
        hid = jax.nn.silu(jnp.dot(h, w1_ref[0], preferred_element_type=F32)) \
            * jnp.dot(h, w3_ref[0], preferred_element_type=F32)
        lane = lax.broadcasted_iota(jnp.int32, (rows, ROUTER_COLS), 1)
        col = N_GROUPS + igroup_ref[w] * EXPERTS_PER_GROUP + e
        c_e = jnp.sum(jnp.where(lane == col, comb_ref[...], 0.0), axis=-1, keepdims=True)
        acc_ref[...] += c_e * jnp.dot(hid.astype(BF16), w2_ref[0], preferred_element_type=F32)

    @pl.when(jnp.logical_not(live))
    def _():
        for_rows(lambda r: gather_copy(next_tile, e * part + r, 1 - slot).start(), part)

    @pl.when(jnp.logical_and(e == EXPERTS_PER_GROUP - 1, (flags & ITEM_LAST) != 0))
    def _():
        osl = tile % 2

        @pl.when(tile >= 2)
        def _():
            for_rows(lambda r: scatter_copy(tile - 2, r, osl).wait())
        y = acc_ref[...]
        if final_norm:
            y = y * _rms_scale(y) * fg_ref[...]
        obuf[osl] = y
        for_rows(lambda r: scatter_copy(tile, r, osl).start())

        @pl.when((flags & ITEM_FINAL) != 0)
        def _():
            @pl.when(tile >= 1)
            def _():
                for_rows(lambda r: scatter_copy(tile - 1, r, 1 - osl).wait())
            for_rows(lambda r: scatter_copy(tile, r, osl).wait())

    @pl.when(jnp.logical_and(e == EXPERTS_PER_GROUP - 1, w == pl.num_programs(0) - 1))
    def _():
        for_rows(lambda r: gather_copy(next_tile, r, 1 - slot).wait())


def _experts(xr, toks, item_tile, item_group, item_flags, g, w1, w3, w2, final_g, rows,
             final_norm):
    T = xr.shape[0]
    D = D_MODEL
    max_items = item_group.shape[0]

    def expert_block(w, e, tok, itile, igroup, iflags):
        return (igroup[w] * EXPERTS_PER_GROUP + e, 0, 0)

    row = pl.BlockSpec((1, D), lambda w, e, tok, itile, igroup, iflags: (0, 0))
    grid_spec = pltpu.PrefetchScalarGridSpec(
        num_scalar_prefetch=4,
        grid=(max_items, EXPERTS_PER_GROUP),
        in_specs=[
            pl.BlockSpec(memory_space=pl.ANY),
            row,
            pl.BlockSpec((1, D, EXPERT_FF), expert_block),
            pl.BlockSpec((1, D, EXPERT_FF), expert_block),
            pl.BlockSpec((1, EXPERT_FF, D), expert_block),
            row,
        ],
        out_specs=pl.BlockSpec(memory_space=pl.ANY),
        scratch_shapes=[
            pltpu.VMEM((2, rows, D + ROUTER_COLS), F32),
            pltpu.VMEM((2, rows, D), F32),
            pltpu.VMEM((rows, D), BF16),
            pltpu.VMEM((rows, ROUTER_COLS), F32),
            pltpu.VMEM((rows, D), F32),
            pltpu.SemaphoreType.DMA((2,)),
            pltpu.SemaphoreType.DMA((2,)),
        ],
    )
    return pl.pallas_call(
        functools.partial(_experts_kernel, rows=rows, final_norm=final_norm),
        grid_spec=grid_spec,
        out_shape=jax.ShapeDtypeStruct((T, D), F32),
        compiler_params=_params(("arbitrary", "arbitrary")),
        name="moe_experts",
    )(toks, item_tile, item_group, item_flags, xr, g, w1, w3, w2, final_g)


def _work_items(cnt, rows, n_tiles):
    i32 = jnp.int32
    max_items = n_tiles + N_GROUPS - 1
    ends = jnp.cumsum(cnt)
    starts = ends - cnt
    first_tile = starts // rows
    n_g = jnp.where(cnt > 0, (ends - 1) // rows - first_tile + 1, 0)
    item_end = jnp.cumsum(n_g)
    n_items = item_end[-1]
    w = jnp.arange(max_items + 1, dtype=i32)
    grp = jnp.minimum(jnp.sum(w[:, None] >= item_end[None, :], axis=1), N_GROUPS - 1).astype(i32)
    tile = first_tile[grp] + (w - (item_end - n_g)[grp])
    live = w < n_items
    tile = jnp.where(live, tile, n_tiles - 1).astype(i32)
    prev_tile = jnp.concatenate([jnp.full((1,), -1, i32), tile[:-1]])
    next_tile = jnp.concatenate([tile[1:], jnp.full((1,), -1, i32)])
    final = w == n_items - 1
    first = live & (tile != prev_tile)
    last = live & ((tile != next_tile) | final)
    flags = (first * ITEM_FIRST + last * ITEM_LAST + live * ITEM_LIVE
             + final * ITEM_FINAL).astype(i32)
    return starts, tile, grp[:max_items], flags[:max_items]


def _moe(x2, g, rw_hi, rw_lo, rb, w1, w3, w2, final_g, rows, final_norm):
    T = x2.shape[0]
    xr, meta, counts = _route(x2, g, rw_hi, rw_lo, rb, rows)
    starts, item_tile, item_group, item_flags = _work_items(counts[0, :N_GROUPS], rows, T // rows)
    pos = (starts[meta[:, 0]] + meta[:, 1]).astype(jnp.int32)
    toks = _invert(pos)
    return _experts(xr, toks, item_tile, item_group, item_flags, g, w1, w3, w2, final_g, rows,
                    final_norm)


def _router_weights(rg_w, rg_b, re_w, re_b):
    D = rg_w.shape[0]
    w = jnp.zeros((D, ROUTER_COLS), F32)
    w = w.at[:, :N_GROUPS].set(rg_w).at[:, N_GROUPS:N_GROUPS + N_EXPERTS].set(re_w)
    b = jnp.zeros((1, ROUTER_COLS), F32)
    b = b.at[0, :N_GROUPS].set(rg_b).at[0, N_GROUPS:N_GROUPS + N_EXPERTS].set(re_b)
    hi = w.astype(BF16)
    lo = (w - hi.astype(F32)).astype(BF16)
    return hi, lo, b


def kernel(x, norm1_g, w_in, b_gate, sgu_ln_g, sgu_ln_b, sgu_w, sgu_b, lam_q1, lam_k1, lam_q2,
           lam_k2, diff_norm_g, w_proj_a, w_proj_b, w_out, norm2_g, router_g_w, router_g_b,
           router_e_w, router_e_b, w1, w3, w2, final_g):
    B, S, D = x.shape
    assert D == D_MODEL and w_in.shape[2] == IN_COLS
    depth = w_in.shape[0]
    T = B * S
    slopes = jnp.exp2(-8.0 * jnp.arange(1, DIFF_HEADS + 1, dtype=F32) / DIFF_HEADS)

    tm_in = min(1024, T)
    tm_sgu = min(512, T)
    tq = min(512, S)
    tk = min(1024, S)
    tm_merge = min(256, T)
    tm_moe = min(512, T)

    x2 = x.reshape(T, D)
    for l in range(depth):
        lam_init = 0.8 - 0.6 * math.exp(-0.3 * l)
        z = _inproj(x2, norm1_g[l][None], w_in[l].astype(BF16), tm_in, 512)
        a = _sgu(z, sgu_ln_g[l][None], sgu_ln_b[l][None], sgu_w[l].astype(BF16),
                 sgu_b[l][:, :, None], tm_sgu)
        o = _attention(z, slopes, lam_q1[l][None], lam_k1[l][None], lam_q2[l][None],
                       lam_k2[l][None], diff_norm_g[l][:, None], B, S, tq, tk, lam_init)
        x2 = _merge(x2, a, o, z, b_gate[l], w_proj_a[l].astype(BF16), w_proj_b[l].astype(BF16),
                    w_out[l].astype(BF16), tm_merge)
        rw_hi, rw_lo, rb = _router_weights(router_g_w[l], router_g_b[l], router_e_w[l],
                                           router_e_b[l])
        x2 = _moe(x2, norm2_g[l][None], rw_hi, rw_lo, rb, w1[l].astype(BF16), w3[l].astype(BF16),
                  w2[l].astype(BF16), final_g[None], tm_moe, final_norm=(l == depth - 1))
    return x2.reshape(B, S, D)
```

```python
import functools
import math

import jax
import jax.numpy as jnp
from jax import lax
from jax.experimental import pallas as pl
from jax.experimental.pallas import tpu as pltpu

F32 = jnp.float32
BF16 = jnp.bfloat16

D_MODEL = 2048
SGU_GROUPS = 8
SGU_WIDTH = 1024
SGU_GROUP_DIM = SGU_WIDTH // SGU_GROUPS
CHUNK = 128
DIFF_HEADS = 8
DIFF_HEAD_DIM = 64
DIFF_V_DIM = 2 * DIFF_HEAD_DIM
DIFF_QK_WIDTH = DIFF_HEADS * 2 * DIFF_HEAD_DIM
DIFF_WIDTH = DIFF_HEADS * DIFF_V_DIM
N_BRANCHES = 2
IN_COLS = 2 * SGU_WIDTH + 2 * DIFF_QK_WIDTH + DIFF_WIDTH + N_BRANCHES * D_MODEL
N_GROUPS = 4
EXPERTS_PER_GROUP = 4
N_EXPERTS = N_GROUPS * EXPERTS_PER_GROUP
EXPERT_FF = 512
RMS_EPS = 1e-6
LN_EPS = 1e-5

Q_COL128 = (2 * SGU_WIDTH) // 128
K_COL128 = (2 * SGU_WIDTH + DIFF_QK_WIDTH) // 128
V_COL128 = (2 * SGU_WIDTH + 2 * DIFF_QK_WIDTH) // 128
GATE_COL1024 = (2 * SGU_WIDTH + 2 * DIFF_QK_WIDTH + DIFF_WIDTH) // 1024

LANES = 128
MXU_COLS = 256
ROUTER_COLS = LANES
VMEM_LIMIT = 56 * 1024 * 1024


def _params(semantics):
    return pltpu.CompilerParams(dimension_semantics=semantics, vmem_limit_bytes=VMEM_LIMIT)


def _gelu(x):
    return 0.5 * x * (1.0 + jnp.tanh(0.7978845608028654 * (x + 0.044715 * (x * x * x))))


def _rms_scale(x):
    return lax.rsqrt(jnp.mean(x * x, axis=-1, keepdims=True) + RMS_EPS)


def _inproj_kernel(x_ref, g_ref, w_ref, z_ref, h_ref):
    @pl.when(pl.program_id(1) == 0)
    def _():
        x = x_ref[...]
        h_ref[...] = (x * _rms_scale(x) * g_ref[...]).astype(BF16)

    z_ref[...] = jnp.dot(h_ref[...], w_ref[...], preferred_element_type=F32).astype(BF16)


def _inproj(x2, g, w_bf16, tm, tn):
    T, D = x2.shape
    N = w_bf16.shape[1]
    return pl.pallas_call(
        _inproj_kernel,
        grid=(T // tm, N // tn),
        in_specs=[
            pl.BlockSpec((tm, D), lambda i, j: (i, 0)),
            pl.BlockSpec((1, D), lambda i, j: (0, 0)),
            pl.BlockSpec((D, tn), lambda i, j: (0, j)),
        ],
        out_specs=pl.BlockSpec((tm, tn), lambda i, j: (i, j)),
        out_shape=jax.ShapeDtypeStruct((T, N), BF16),
        scratch_shapes=[pltpu.VMEM((tm, D), BF16)],
        compiler_params=_params(("parallel", "arbitrary")),
        name="inproj",
    )(x2, g, w_bf16)


def _sgu_kernel(u_ref, v_ref, lng_ref, lnb_ref, ws_ref, bs_ref, a_ref, *, chunks):
    v = _gelu(v_ref[...].astype(F32))
    mu = jnp.mean(v, axis=-1, keepdims=True)
    vc = v - mu
    var = jnp.mean(vc * vc, axis=-1, keepdims=True)
    vn = (vc * lax.rsqrt(var + LN_EPS) * lng_ref[...] + lnb_ref[...]).astype(BF16)
    for c in range(chunks):
        rows = slice(c * CHUNK, (c + 1) * CHUNK)
        for g in range(SGU_GROUPS):
            cols = slice(g * SGU_GROUP_DIM, (g + 1) * SGU_GROUP_DIM)
            mixed = jnp.dot(ws_ref[g], vn[rows, cols], preferred_element_type=F32)
            mixed = mixed + bs_ref[g]
            u = _gelu(u_ref[rows, cols].astype(F32))
            a_ref[rows, cols] = (u * mixed).astype(BF16)


def _sgu(z, ln_g, ln_b, ws_bf16, bs_col, tm):
    T = z.shape[0]
    wblk = SGU_WIDTH
    return pl.pallas_call(
        functools.partial(_sgu_kernel, chunks=tm // CHUNK),
        grid=(T // tm,),
        in_specs=[
            pl.BlockSpec((tm, wblk), lambda i: (i, 0)),
            pl.BlockSpec((tm, wblk), lambda i: (i, 1)),
            pl.BlockSpec((1, wblk), lambda i: (0, 0)),
            pl.BlockSpec((1, wblk), lambda i: (0, 0)),
            pl.BlockSpec((SGU_GROUPS, CHUNK, CHUNK), lambda i: (0, 0, 0)),
            pl.BlockSpec((SGU_GROUPS, CHUNK, 1), lambda i: (0, 0, 0)),
        ],
        out_specs=pl.BlockSpec((tm, wblk), lambda i: (i, 0)),
        out_shape=jax.ShapeDtypeStruct((T, SGU_WIDTH), BF16),
        compiler_params=_params(("parallel",)),
        name="sgu",
    )(z, z, ln_g, ln_b, ws_bf16, bs_col)


LOG2E = 1.4426950408889634
VT_PAD = 16


def _split3(x):
    a1 = x.astype(BF16).astype(F32)
    a2 = (x - a1).astype(BF16).astype(F32)
    a3 = (x - a1 - a2).astype(BF16).astype(F32)
    return a1, a2, a3


def _pick(row, values):
    out = jnp.zeros(values[0].shape, F32)
    for i, v in enumerate(values):
        out = jnp.where(row == i, v, out)
    return out


def _attn_kernel(slopes_ref, lq1_ref, lk1_ref, lq2_ref, lk2_ref, dgt_ref, q_ref, k_ref, v_ref,
                 o_ref, qx_ref, kx_ref, vt_ref, rel_ref, s0_ref, s1_ref, ml0_ref, ml1_ref,
                 m_ref, acc_ref, *, tq, tk, seq, lam_init):
    h = pl.program_id(1)
    qi = pl.program_id(2)
    slope2 = slopes_ref[h] * LOG2E
    hd = DIFF_V_DIM
    n = seq // tk
    q0 = qi * tq
    jd = q0 // tk

    @pl.when(qi == 0)
    def _():
        pad_row = lax.broadcasted_iota(jnp.int32, (VT_PAD, tk), 0)
        ones_pad = jnp.where(pad_row == 0, 1.0, 0.0).astype(BF16)

        def transpose_chunk(c, carry):
            sl = pl.ds(pl.multiple_of(c * tk, tk), tk)
            vt_ref[0:hd, sl] = v_ref[sl, :].astype(F32).T.astype(BF16)
            vt_ref[hd:hd + VT_PAD, sl] = ones_pad
            return carry
        lax.fori_loop(0, n, transpose_chunk, 0)

        c_idx = lax.broadcasted_iota(jnp.int32, (tk, hd), 0)
        k_lane = lax.broadcasted_iota(jnp.int32, (tk, hd), 1)
        a = _split3(jnp.full((tk, hd), slope2, F32))
        c_lo = (c_idx & (LANES - 1)).astype(F32)
        c_hi = (c_idx >> 7).astype(F32)
        kx = _pick(k_lane, [-a[0], -a[1], -a[2], -LANES * a[0], -LANES * a[1], -LANES * a[2],
                            c_lo, c_lo, c_lo, c_hi, c_hi, c_hi])
        kx_ref[...] = kx.astype(BF16)

        x_row = lax.broadcasted_iota(jnp.int32, (hd, 2 * tq), 0)
        r_idx = lax.broadcasted_iota(jnp.int32, (hd, 2 * tq), 1)
        r_idx = jnp.where(r_idx >= tq, r_idx - tq, r_idx)
        r_lo = (r_idx & (LANES - 1)).astype(F32)
        r_hi = (r_idx >> 7).astype(F32)
        a = _split3(jnp.full((hd, 2 * tq), slope2, F32))
        ext = _pick(x_row, [r_lo, r_lo, r_lo, r_hi, r_hi, r_hi,
                            a[0], a[1], a[2], LANES * a[0], LANES * a[1], LANES * a[2]])
        for side, sign in ((0, 1.0), (1, -1.0), (2, 0.0)):
            qx_ref[side, hd:2 * hd, :] = (sign * ext).astype(BF16)

        rel_ref[...] = (lax.broadcasted_iota(jnp.int32, (tk, tq), 0)
                        - lax.broadcasted_iota(jnp.int32, (tk, tq), 1)).astype(F32)

    qt = (q_ref[...].astype(F32) * (DIFF_HEAD_DIM ** -0.5 * LOG2E)).T
    d_row = lax.broadcasted_iota(jnp.int32, (hd, tq), 0)
    q_main = jnp.concatenate([jnp.where(d_row < DIFF_HEAD_DIM, qt, 0.0),
                              jnp.where(d_row >= DIFF_HEAD_DIM, qt, 0.0)], axis=1).astype(BF16)
    for side in range(3):
        qx_ref[side, 0:hd, :] = q_main

    m_ref[...] = jnp.full(m_ref.shape, -jnp.inf, F32)
    acc_ref[...] = jnp.zeros(acc_ref.shape, F32)

    n_col = 2 * tq // MXU_COLS

    def scores_tile(c, j, side, s_ref, ml_ref, bias=None):
        cs = slice(c * MXU_COLS, (c + 1) * MXU_COLS)
        rows = pl.ds(pl.multiple_of(j * tk, tk), tk)
        kcx = jnp.concatenate([k_ref[rows, :], kx_ref[...]], axis=1)
        s = jnp.dot(kcx, qx_ref[side, :, cs], preferred_element_type=F32)
        if bias is not None:
            b0 = (c * MXU_COLS) % tq
            s = s + bias[:, b0:b0 + MXU_COLS]
        s_ref[:, cs] = s
        ml_ref[:, cs] = jnp.max(s, axis=0, keepdims=True)

    def accumulate_tile(c, j, cst, s_ref, ml_ref):
        cs = slice(c * MXU_COLS, (c + 1) * MXU_COLS)
        m_prev = m_ref[:, cs]
        m_new = jnp.maximum(m_prev, ml_ref[:, cs] + cst)
        alpha = jnp.exp2(m_prev - m_new)
        p = jnp.exp2(s_ref[:, cs] - (m_new - cst)).astype(BF16)
        keys = pl.ds(pl.multiple_of(j * tk, tk), tk)
        acc_ref[:, cs] = alpha * acc_ref[:, cs] + jnp.dot(vt_ref[:, keys], p,
                                                          preferred_element_type=F32)
        m_ref[:, cs] = m_new

    def stage(score_args, acc_args):
        for c in range(n_col):
            if score_args is not None:
                scores_tile(c, *score_args)
            if acc_args is not None:
                accumulate_tile(c, *acc_args)

    def scores(j, side, s_ref, ml_ref, bias=None):
        stage((j, side, s_ref, ml_ref, bias), None)

    def accumulate(j, cst, s_ref, ml_ref):
        stage(None, (j, cst, s_ref, ml_ref))

    def chunk(t):
        side = (t >= jd).astype(jnp.int32)
        j = t + side
        sign = (1 - 2 * side).astype(F32)
        cst = -sign * slope2 * (q0 - j * tk).astype(F32)
        return j, side, cst

    diag_bias = -slope2 * jnp.abs(rel_ref[...] + (jd * tk - q0).astype(F32))
    scores(jd, 2, s0_ref, ml0_ref, diag_bias)

    if n > 1:
        ja0, sa0, _ = chunk(jnp.int32(0))
        stage((ja0, sa0, s1_ref, ml1_ref), (jd, 0.0, s0_ref, ml0_ref))

        def pair(i, carry):
            ja, _, ca = chunk(2 * i)
            jb, sb, cb = chunk(2 * i + 1)
            jc, sc, _ = chunk(2 * i + 2)
            stage((jb, sb, s0_ref, ml0_ref), (ja, ca, s1_ref, ml1_ref))
            stage((jc, sc, s1_ref, ml1_ref), (jb, cb, s0_ref, ml0_ref))
            return carry

        lax.fori_loop(0, (n - 2) // 2, pair, 0)
        jl, _, cl = chunk(jnp.int32(n - 2))
        accumulate(jl, cl, s1_ref, ml1_ref)
    else:
        accumulate(jd, 0.0, s0_ref, ml0_ref)

    lam = (jnp.exp(jnp.sum(lq1_ref[...] * lk1_ref[...], axis=-1, keepdims=True))
           - jnp.exp(jnp.sum(lq2_ref[...] * lk2_ref[...], axis=-1, keepdims=True))
           + lam_init)
    acc = acc_ref[...]
    out = acc[0:hd, :] * (1.0 / acc[hd:hd + 1, :])
    o = out[:, 0:tq] - lam * out[:, tq:2 * tq]
    o = o * lax.rsqrt(jnp.mean(o * o, axis=0, keepdims=True) + RMS_EPS)
    o = o * dgt_ref[...] * (1.0 - lam_init)
    o_ref[...] = o.T.astype(BF16)


def _attention(z, slopes, lq1, lk1, lq2, lk2, dg, batch, seq, tq, tk, lam_init):
    T = z.shape[0]
    nq = seq // tq
    hd = DIFF_V_DIM
    lam_spec = pl.BlockSpec((1, DIFF_HEAD_DIM), lambda b, h, i, s: (0, 0))
    grid_spec = pltpu.PrefetchScalarGridSpec(
        num_scalar_prefetch=1,
        grid=(batch, DIFF_HEADS, nq),
        in_specs=[
            lam_spec, lam_spec, lam_spec, lam_spec,
            pl.BlockSpec((hd, 1), lambda b, h, i, s: (0, 0)),
            pl.BlockSpec((tq, hd), lambda b, h, i, s: (b * nq + i, Q_COL128 + h)),
            pl.BlockSpec((seq, hd), lambda b, h, i, s: (b, K_COL128 + h),
                         pipeline_mode=pl.Buffered(1)),
            pl.BlockSpec((seq, hd), lambda b, h, i, s: (b, V_COL128 + h),
                         pipeline_mode=pl.Buffered(1)),
        ],
        out_specs=pl.BlockSpec((tq, hd), lambda b, h, i, s: (b * nq + i, h)),
        scratch_shapes=[
            pltpu.VMEM((3, 2 * hd, 2 * tq), BF16),
            pltpu.VMEM((tk, hd), BF16),
            pltpu.VMEM((hd + VT_PAD, seq), BF16),
            pltpu.VMEM((tk, tq), F32),
            pltpu.VMEM((tk, 2 * tq), F32),
            pltpu.VMEM((tk, 2 * tq), F32),
            pltpu.VMEM((1, 2 * tq), F32),
            pltpu.VMEM((1, 2 * tq), F32),
            pltpu.VMEM((1, 2 * tq), F32),
            pltpu.VMEM((hd + VT_PAD, 2 * tq), F32),
        ],
    )
    return pl.pallas_call(
        functools.partial(_attn_kernel, tq=tq, tk=tk, seq=seq, lam_init=lam_init),
        grid_spec=grid_spec,
        out_shape=jax.ShapeDtypeStruct((T, DIFF_WIDTH), BF16),
        compiler_params=_params(("parallel", "parallel", "arbitrary")),
        name="diff_attn",
    )(slopes, lq1, lk1, lq2, lk2, dg, z, z, z)


def _merge_kernel(x_ref, a_ref, b_ref, g00_ref, g01_ref, g10_ref, g11_ref, bg_ref,
                  wa_ref, wb_ref, wo_ref, o_ref, merged_ref):
    gate_refs = ((g00_ref, g01_ref), (g10_ref, g11_ref))
    half = D_MODEL // 2
    a = a_ref[...]
    b = b_ref[...]
    for c in range(2):
        cols = slice(c * half, (c + 1) * half)
        ga = jax.nn.sigmoid(gate_refs[0][c][...].astype(F32) + bg_ref[0:1, cols])
        gb = jax.nn.sigmoid(gate_refs[1][c][...].astype(F32) + bg_ref[1:2, cols])
        pa = jnp.dot(a, wa_ref[:, cols], preferred_element_type=F32)
        pb = jnp.dot(b, wb_ref[:, cols], preferred_element_type=F32)
        merged_ref[:, cols] = (ga * pa + gb * pb).astype(BF16)
    o_ref[...] = x_ref[...] + jnp.dot(merged_ref[...], wo_ref[...], preferred_element_type=F32)


def _merge(x2, a, b, z, b_gate, wa, wb, wo, tm):
    T, D = x2.shape
    half = D // 2

    def gate_spec(k):
        return pl.BlockSpec((tm, half), lambda i: (i, GATE_COL1024 + k))

    def const_spec(shape):
        return pl.BlockSpec(shape, lambda i: (0, 0), pipeline_mode=pl.Buffered(1))

    return pl.pallas_call(
        _merge_kernel,
        grid=(T // tm,),
        in_specs=[
            pl.BlockSpec((tm, D), lambda i: (i, 0)),
            pl.BlockSpec((tm, SGU_WIDTH), lambda i: (i, 0)),
            pl.BlockSpec((tm, DIFF_WIDTH), lambda i: (i, 0)),
            gate_spec(0), gate_spec(1), gate_spec(2), gate_spec(3),
            const_spec((N_BRANCHES, D)),
            const_spec((SGU_WIDTH, D)),
            const_spec((DIFF_WIDTH, D)),
            const_spec((D, D)),
        ],
        out_specs=pl.BlockSpec((tm, D), lambda i: (i, 0)),
        out_shape=jax.ShapeDtypeStruct((T, D), F32),
        scratch_shapes=[pltpu.VMEM((tm, D), BF16)],
        compiler_params=_params(("parallel",)),
        name="merge_out",
    )(x2, a, b, z, z, z, z, b_gate, wa, wb, wo)


def _split_bf16(x):
    hi = x.astype(BF16)
    lo = (x - hi.astype(F32)).astype(BF16)
    return hi, lo


def _dot3(x_hi, x_lo, w_hi, w_lo):
    return (jnp.dot(x_hi, w_hi, preferred_element_type=F32)
            + jnp.dot(x_lo, w_hi, preferred_element_type=F32)
            + jnp.dot(x_hi, w_lo, preferred_element_type=F32))


def _router(h, rw_hi_ref, rw_lo_ref, rb_ref):
    tm = h.shape[0]
    h_hi, h_lo = _split_bf16(h)
    logits = _dot3(h_hi, h_lo, rw_hi_ref[...], rw_lo_ref[...]) + rb_ref[...]
    lane = lax.broadcasted_iota(jnp.int32, (tm, ROUTER_COLS), 1)
    neg = jnp.float32(-jnp.inf)
    is_group = lane < N_GROUPS
    gl = jnp.where(is_group, logits, neg)
    gmax = jnp.max(gl, axis=-1, keepdims=True)
    gexp = jnp.exp(gl - gmax)
    g_w = 1.0 / jnp.sum(gexp, axis=-1, keepdims=True)
    g_idx = jnp.min(jnp.where(gl == gmax, lane, ROUTER_COLS), axis=-1, keepdims=True)
    e_lo = N_GROUPS + g_idx * EXPERTS_PER_GROUP
    in_group = (lane >= e_lo) & (lane < e_lo + EXPERTS_PER_GROUP)
    el = jnp.where(in_group, logits, neg)
    m1 = jnp.max(el, axis=-1, keepdims=True)
    i1 = jnp.min(jnp.where(el == m1, lane, ROUTER_COLS), axis=-1, keepdims=True)
    el2 = jnp.where(lane == i1, neg, el)
    m2 = jnp.max(el2, axis=-1, keepdims=True)
    i2 = jnp.min(jnp.where(el2 == m2, lane, ROUTER_COLS), axis=-1, keepdims=True)
    t = jnp.exp(m2 - m1)
    w1 = g_w / (1.0 + t)
    w2 = g_w * t / (1.0 + t)
    comb = jnp.where(lane == i1, w1, 0.0) + jnp.where(lane == i2, w2, 0.0)
    return comb, g_idx


META_COLS = LANES


def _route_kernel(x_ref, g_ref, rw_hi_ref, rw_lo_ref, rb_ref, xr_ref, meta_ref, cnt_ref,
                  carry_ref, tri_ref):
    tm, D = x_ref.shape

    @pl.when(pl.program_id(0) == 0)
    def _():
        carry_ref[...] = jnp.zeros(carry_ref.shape, F32)
        r = lax.broadcasted_iota(jnp.int32, (tm, tm), 0)
        c = lax.broadcasted_iota(jnp.int32, (tm, tm), 1)
        tri_ref[...] = jnp.where(c < r, 1.0, 0.0).astype(BF16)

    x = x_ref[...]
    h = x * _rms_scale(x) * g_ref[...]
    comb, g_idx = _router(h, rw_hi_ref, rw_lo_ref, rb_ref)
    xr_ref[:, 0:D] = x
    xr_ref[:, D:D + ROUTER_COLS] = comb

    lane = lax.broadcasted_iota(jnp.int32, (tm, META_COLS), 1)
    onehot = jnp.where(lane == g_idx, 1.0, 0.0)
    earlier = jnp.dot(tri_ref[...], onehot.astype(BF16), preferred_element_type=F32)
    carry = carry_ref[...]
    rank = jnp.sum((earlier + carry) * onehot, axis=-1, keepdims=True).astype(jnp.int32)
    meta_ref[...] = jnp.where(lane == 0, g_idx, jnp.where(lane == 1, rank, 0))
    carry = carry + jnp.sum(onehot, axis=0, keepdims=True)
    carry_ref[...] = carry
    cnt_ref[...] = carry.astype(jnp.int32)


def _route(x2, g, rw_hi, rw_lo, rb, tm):
    T, D = x2.shape
    row = pl.BlockSpec((1, D), lambda i: (0, 0))
    rspec = pl.BlockSpec((D, ROUTER_COLS), lambda i: (0, 0))
    return pl.pallas_call(
        _route_kernel,
        grid=(T // tm,),
        in_specs=[pl.BlockSpec((tm, D), lambda i: (i, 0)), row, rspec, rspec,
                  pl.BlockSpec((1, ROUTER_COLS), lambda i: (0, 0))],
        out_specs=[pl.BlockSpec((tm, D + ROUTER_COLS), lambda i: (i, 0)),
                   pl.BlockSpec((tm, META_COLS), lambda i: (i, 0)),
                   pl.BlockSpec((1, META_COLS), lambda i: (0, 0))],
        out_shape=[jax.ShapeDtypeStruct((T, D + ROUTER_COLS), F32),
                   jax.ShapeDtypeStruct((T, META_COLS), jnp.int32),
                   jax.ShapeDtypeStruct((1, META_COLS), jnp.int32)],
        scratch_shapes=[pltpu.VMEM((1, META_COLS), F32), pltpu.VMEM((tm, tm), BF16)],
        compiler_params=_params(("arbitrary",)),
        name="moe_route",
    )(x2, g, rw_hi, rw_lo, rb)


def _invert_kernel(pos_ref, tok_ref):
    def put(t, carry):
        tok_ref[pos_ref[t]] = t
        return carry
    lax.fori_loop(0, pos_ref.shape[0], put, 0, unroll=16)


def _invert(pos):
    return pl.pallas_call(
        _invert_kernel,
        in_specs=[pl.BlockSpec(memory_space=pltpu.SMEM)],
        out_specs=pl.BlockSpec(memory_space=pltpu.SMEM),
        out_shape=jax.ShapeDtypeStruct(pos.shape, jnp.int32),
        name="moe_invert",
    )(pos)


ITEM_FIRST, ITEM_LAST, ITEM_LIVE, ITEM_FINAL = 1, 2, 4, 8


def _experts_kernel(tok_ref, itile_ref, igroup_ref, iflags_ref, xr_hbm, g_ref, w1_ref, w3_ref,
                    w2_ref, fg_ref, out_hbm, xbuf, obuf, h_ref, comb_ref, acc_ref, gsem, ssem,
                    *, rows, final_norm):
    w = pl.program_id(0)
    e = pl.program_id(1)
    slot = w % 2
    tile = itile_ref[w]
    flags = iflags_ref[w]
    live = (flags & ITEM_LIVE) != 0
    D = out_hbm.shape[1]
    part = rows // EXPERTS_PER_GROUP

    def gather_copy(t, r, sl):
        tok = tok_ref[t * rows + r]
        return pltpu.make_async_copy(xr_hbm.at[pl.ds(tok, 1), :],
                                     xbuf.at[sl, pl.ds(r, 1), :], gsem.at[sl])

    def scatter_copy(t, r, sl):
        tok = tok_ref[t * rows + r]
        return pltpu.make_async_copy(obuf.at[sl, pl.ds(r, 1), :],
                                     out_hbm.at[pl.ds(tok, 1), :], ssem.at[sl])

    def for_rows(fn, n=rows):
        def body(r, carry):
            fn(r)
            return carry
        lax.fori_loop(0, n, body, 0, unroll=8)

    @pl.when(e == 0)
    def _():
        @pl.when(w == 0)
        def _():
            for_rows(lambda r: gather_copy(tile, r, slot).start())
        for_rows(lambda r: gather_copy(tile, r, slot).wait())

        @pl.when((flags & ITEM_FIRST) != 0)
        def _():
            x = xbuf[slot, :, 0:D]
            h_ref[...] = (x * _rms_scale(x) * g_ref[...]).astype(BF16)
            comb_ref[...] = xbuf[slot, :, D:D + ROUTER_COLS]
            acc_ref[...] = x

    next_tile = itile_ref[w + 1]

    @pl.when(live)
    def _():
        for r in range(part):
            gather_copy(next_tile, e * part + r, 1 - slot).start()
        h = h_ref[...]
        hid = jax.nn.silu(jnp.dot(h, w1_ref[0], preferred_element_type=F32)) \
            * jnp.dot(h, w3_ref[0], preferred_element_type=F32)
        lane = lax.broadcasted_iota(jnp.int32, (rows, ROUTER_COLS), 1)
        col = N_GROUPS + igroup_ref[w] * EXPERTS_PER_GROUP + e
        c_e = jnp.sum(jnp.where(lane == col, comb_ref[...], 0.0), axis=-1, keepdims=True)
        acc_ref[...] += c_e * jnp.dot(hid.astype(BF16), w2_ref[0], preferred_element_type=F32)

    @pl.when(jnp.logical_not(live))
    def _():
        for_rows(lambda r: gather_copy(next_tile, e * part + r, 1 - slot).start(), part)

    @pl.when(jnp.logical_and(e == EXPERTS_PER_GROUP - 1, (flags & ITEM_LAST) != 0))
    def _():
        osl = tile % 2

        @pl.when(tile >= 2)
        def _():
            for_rows(lambda r: scatter_copy(tile - 2, r, osl).wait())
        y = acc_ref[...]
        if final_norm:
            y = y * _rms_scale(y) * fg_ref[...]
        obuf[osl] = y
        for_rows(lambda r: scatter_copy(tile, r, osl).start())

        @pl.when((flags & ITEM_FINAL) != 0)
        def _():
            @pl.when(tile >= 1)
            def _():
                for_rows(lambda r: scatter_copy(tile - 1, r, 1 - osl).wait())
            for_rows(lambda r: scatter_copy(tile, r, osl).wait())

    @pl.when(jnp.logical_and(e == EXPERTS_PER_GROUP - 1, w == pl.num_programs(0) - 1))
    def _():
        for_rows(lambda r: gather_copy(next_tile, r, 1 - slot).wait())


def _experts(xr, toks, item_tile, item_group, item_flags, g, w1, w3, w2, final_g, rows,
             final_norm):
    T = xr.shape[0]
    D = D_MODEL
    max_items = item_group.shape[0]

    def expert_block(w, e, tok, itile, igroup, iflags):
        return (igroup[w] * EXPERTS_PER_GROUP + e, 0, 0)

    row = pl.BlockSpec((1, D), lambda w, e, tok, itile, igroup, iflags: (0, 0))
    grid_spec = pltpu.PrefetchScalarGridSpec(
        num_scalar_prefetch=4,
        grid=(max_items, EXPERTS_PER_GROUP),
        in_specs=[
            pl.BlockSpec(memory_space=pl.ANY),
            row,
            pl.BlockSpec((1, D, EXPERT_FF), expert_block),
            pl.BlockSpec((1, D, EXPERT_FF), expert_block),
            pl.BlockSpec((1, EXPERT_FF, D), expert_block),
            row,
        ],
        out_specs=pl.BlockSpec(memory_space=pl.ANY),
        scratch_shapes=[
            pltpu.VMEM((2, rows, D + ROUTER_COLS), F32),
            pltpu.VMEM((2, rows, D), F32),
            pltpu.VMEM((rows, D), BF16),
            pltpu.VMEM((rows, ROUTER_COLS), F32),
            pltpu.VMEM((rows, D), F32),
            pltpu.SemaphoreType.DMA((2,)),
            pltpu.SemaphoreType.DMA((2,)),
        ],
    )
    return pl.pallas_call(
        functools.partial(_experts_kernel, rows=rows, final_norm=final_norm),
        grid_spec=grid_spec,
        out_shape=jax.ShapeDtypeStruct((T, D), F32),
        compiler_params=_params(("arbitrary", "arbitrary")),
        name="moe_experts",
    )(toks, item_tile, item_group, item_flags, xr, g, w1, w3, w2, final_g)


def _work_items(cnt, rows, n_tiles):
    i32 = jnp.int32
    max_items = n_tiles + N_GROUPS - 1
    ends = jnp.cumsum(cnt)
    starts = ends - cnt
    first_tile = starts // rows
    n_g = jnp.where(cnt > 0, (ends - 1) // rows - first_tile + 1, 0)
    item_end = jnp.cumsum(n_g)
    n_items = item_end[-1]
    w = jnp.arange(max_items + 1, dtype=i32)
    grp = jnp.minimum(jnp.sum(w[:, None] >= item_end[None, :], axis=1), N_GROUPS - 1).astype(i32)
    tile = first_tile[grp] + (w - (item_end - n_g)[grp])
    live = w < n_items
    tile = jnp.where(live, tile, n_tiles - 1).astype(i32)
    prev_tile = jnp.concatenate([jnp.full((1,), -1, i32), tile[:-1]])
    next_tile = jnp.concatenate([tile[1:], jnp.full((1,), -1, i32)])
    final = w == n_items - 1
    first = live & (tile != prev_tile)
    last = live & ((tile != next_tile) | final)
    flags = (first * ITEM_FIRST + last * ITEM_LAST + live * ITEM_LIVE
             + final * ITEM_FINAL).astype(i32)
    return starts, tile, grp[:max_items], flags[:max_items]


def _moe(x2, g, rw_hi, rw_lo, rb, w1, w3, w2, final_g, rows, final_norm):
    T = x2.shape[0]
    xr, meta, counts = _route(x2, g, rw_hi, rw_lo, rb, rows)
    starts, item_tile, item_group, item_flags = _work_items(counts[0, :N_GROUPS], rows, T // rows)
    pos = (starts[meta[:, 0]] + meta[:, 1]).astype(jnp.int32)
    toks = _invert(pos)
    return _experts(xr, toks, item_tile, item_group, item_flags, g, w1, w3, w2, final_g, rows,
                    final_norm)


def _router_weights(rg_w, rg_b, re_w, re_b):
    D = rg_w.shape[0]
    w = jnp.zeros((D, ROUTER_COLS), F32)
    w = w.at[:, :N_GROUPS].set(rg_w).at[:, N_GROUPS:N_GROUPS + N_EXPERTS].set(re_w)
    b = jnp.zeros((1, ROUTER_COLS), F32)
    b = b.at[0, :N_GROUPS].set(rg_b).at[0, N_GROUPS:N_GROUPS + N_EXPERTS].set(re_b)
    hi = w.astype(BF16)
    lo = (w - hi.astype(F32)).astype(BF16)
    return hi, lo, b


def kernel(x, norm1_g, w_in, b_gate, sgu_ln_g, sgu_ln_b, sgu_w, sgu_b, lam_q1, lam_k1, lam_q2,
           lam_k2, diff_norm_g, w_proj_a, w_proj_b, w_out, norm2_g, router_g_w, router_g_b,
           router_e_w, router_e_b, w1, w3, w2, final_g):
    B, S, D = x.shape
    assert D == D_MODEL and w_in.shape[2] == IN_COLS
    depth = w_in.shape[0]
    T = B * S
    slopes = jnp.exp2(-8.0 * jnp.arange(1, DIFF_HEADS + 1, dtype=F32) / DIFF_HEADS)

    tm_in = min(1024, T)
    tm_sgu = min(512, T)
    tq = min(1024, S)
    tk = min(1024, S)
    tm_merge = min(256, T)
    tm_moe = min(512, T)

    x2 = x.reshape(T, D)
    for l in range(depth):
        lam_init = 0.8 - 0.6 * math.exp(-0.3 * l)
        z = _inproj(x2, norm1_g[l][None], w_in[l].astype(BF16), tm_in, 512)
        a = _sgu(z, sgu_ln_g[l][None], sgu_ln_b[l][None], sgu_w[l].astype(BF16),
                 sgu_b[l][:, :, None], tm_sgu)
        o = _attention(z, slopes, lam_q1[l][None], lam_k1[l][None], lam_q2[l][None],
                       lam_k2[l][None], diff_norm_g[l][:, None], B, S, tq, tk, lam_init)
        x2 = _merge(x2, a, o, z, b_gate[l], w_proj_a[l].astype(BF16), w_proj_b[l].astype(BF16),
                    w_out[l].astype(BF16), tm_merge)
        rw_hi, rw_lo, rb = _router_weights(router_g_w[l], router_g_b[l], router_e_w[l],
                                           router_e_b[l])
        x2 = _moe(x2, norm2_g[l][None], rw_hi, rw_lo, rb, w1[l].astype(BF16), w3[l].astype(BF16),
                  w2[l].astype(BF16), final_g[None], tm_moe, final_norm=(l == depth - 1))
    return x2.reshape(B, S, D)
```

```python
import functools
import math

import jax
import jax.numpy as jnp
from jax import lax
from jax.experimental import pallas as pl
from jax.experimental.pallas import tpu as pltpu

F32 = jnp.float32
BF16 = jnp.bfloat16

D_MODEL = 2048
SGU_GROUPS = 8
SGU_WIDTH = 1024
SGU_GROUP_DIM = SGU_WIDTH // SGU_GROUPS
CHUNK = 128
DIFF_HEADS = 8
DIFF_HEAD_DIM = 64
DIFF_V_DIM = 2 * DIFF_HEAD_DIM
DIFF_QK_WIDTH = DIFF_HEADS * 2 * DIFF_HEAD_DIM
DIFF_WIDTH = DIFF_HEADS * DIFF_V_DIM
N_BRANCHES = 2
IN_COLS = 2 * SGU_WIDTH + 2 * DIFF_QK_WIDTH + DIFF_WIDTH + N_BRANCHES * D_MODEL
N_GROUPS = 4
EXPERTS_PER_GROUP = 4
N_EXPERTS = N_GROUPS * EXPERTS_PER_GROUP
EXPERT_FF = 512
RMS_EPS = 1e-6
LN_EPS = 1e-5

Q_COL128 = (2 * SGU_WIDTH) // 128
K_COL128 = (2 * SGU_WIDTH + DIFF_QK_WIDTH) // 128
V_COL128 = (2 * SGU_WIDTH + 2 * DIFF_QK_WIDTH) // 128
GATE_COL1024 = (2 * SGU_WIDTH + 2 * DIFF_QK_WIDTH + DIFF_WIDTH) // 1024

LANES = 128
MXU_COLS = 256
ROUTER_COLS = LANES
VMEM_LIMIT = 56 * 1024 * 1024


def _params(semantics):
    return pltpu.CompilerParams(dimension_semantics=semantics, vmem_limit_bytes=VMEM_LIMIT)


def _gelu(x):
    return 0.5 * x * (1.0 + jnp.tanh(0.7978845608028654 * (x + 0.044715 * (x * x * x))))


def _rms_scale(x):
    return lax.rsqrt(jnp.mean(x * x, axis=-1, keepdims=True) + RMS_EPS)


def _inproj_kernel(x_ref, g_ref, w_ref, z_ref, h_ref):
    @pl.when(pl.program_id(1) == 0)
    def _():
        x = x_ref[...]
        h_ref[...] = (x * _rms_scale(x) * g_ref[...]).astype(BF16)

    z_ref[...] = jnp.dot(h_ref[...], w_ref[...], preferred_element_type=F32).astype(BF16)


def _inproj(x2, g, w_bf16, tm, tn):
    T, D = x2.shape
    N = w_bf16.shape[1]
    return pl.pallas_call(
        _inproj_kernel,
        grid=(T // tm, N // tn),
        in_specs=[
            pl.BlockSpec((tm, D), lambda i, j: (i, 0)),
            pl.BlockSpec((1, D), lambda i, j: (0, 0)),
            pl.BlockSpec((D, tn), lambda i, j: (0, j)),
        ],
        out_specs=pl.BlockSpec((tm, tn), lambda i, j: (i, j)),
        out_shape=jax.ShapeDtypeStruct((T, N), BF16),
        scratch_shapes=[pltpu.VMEM((tm, D), BF16)],
        compiler_params=_params(("parallel", "arbitrary")),
        name="inproj",
    )(x2, g, w_bf16)


def _sgu_kernel(u_ref, v_ref, lng_ref, lnb_ref, ws_ref, bs_ref, a_ref, *, chunks):
    v = _gelu(v_ref[...].astype(F32))
    mu = jnp.mean(v, axis=-1, keepdims=True)
    vc = v - mu
    var = jnp.mean(vc * vc, axis=-1, keepdims=True)
    vn = (vc * lax.rsqrt(var + LN_EPS) * lng_ref[...] + lnb_ref[...]).astype(BF16)
    for c in range(chunks):
        rows = slice(c * CHUNK, (c + 1) * CHUNK)
        for g in range(SGU_GROUPS):
            cols = slice(g * SGU_GROUP_DIM, (g + 1) * SGU_GROUP_DIM)
            mixed = jnp.dot(ws_ref[g], vn[rows, cols], preferred_element_type=F32)
            mixed = mixed + bs_ref[g]
            u = _gelu(u_ref[rows, cols].astype(F32))
            a_ref[rows, cols] = (u * mixed).astype(BF16)


def _sgu(z, ln_g, ln_b, ws_bf16, bs_col, tm):
    T = z.shape[0]
    wblk = SGU_WIDTH
    return pl.pallas_call(
        functools.partial(_sgu_kernel, chunks=tm // CHUNK),
        grid=(T // tm,),
        in_specs=[
            pl.BlockSpec((tm, wblk), lambda i: (i, 0)),
            pl.BlockSpec((tm, wblk), lambda i: (i, 1)),
            pl.BlockSpec((1, wblk), lambda i: (0, 0)),
            pl.BlockSpec((1, wblk), lambda i: (0, 0)),
            pl.BlockSpec((SGU_GROUPS, CHUNK, CHUNK), lambda i: (0, 0, 0)),
            pl.BlockSpec((SGU_GROUPS, CHUNK, 1), lambda i: (0, 0, 0)),
        ],
        out_specs=pl.BlockSpec((tm, wblk), lambda i: (i, 0)),
        out_shape=jax.ShapeDtypeStruct((T, SGU_WIDTH), BF16),
        compiler_params=_params(("parallel",)),
        name="sgu",
    )(z, z, ln_g, ln_b, ws_bf16, bs_col)


LOG2E = 1.4426950408889634
VT_PAD = 16
ATTN_LAG = 2


def _split3(x):
    a1 = x.astype(BF16).astype(F32)
    a2 = (x - a1).astype(BF16).astype(F32)
    a3 = (x - a1 - a2).astype(BF16).astype(F32)
    return a1, a2, a3


def _pick(row, values):
    out = jnp.zeros(values[0].shape, F32)
    for i, v in enumerate(values):
        out = jnp.where(row == i, v, out)
    return out


def _attn_kernel(slopes_ref, lq1_ref, lk1_ref, lq2_ref, lk2_ref, dgt_ref, q_ref, k_ref, v_ref,
                 o_ref, qx_ref, kx_ref, vt_ref, rel_ref, s0_ref, s1_ref, ml0_ref, ml1_ref,
                 m_ref, acc_ref, *, tq, tk, seq, lam_init):
    h = pl.program_id(1)
    qi = pl.program_id(2)
    slope2 = slopes_ref[h] * LOG2E
    hd = DIFF_V_DIM
    n = seq // tk
    q0 = qi * tq
    jd = q0 // tk

    @pl.when(qi == 0)
    def _():
        pad_row = lax.broadcasted_iota(jnp.int32, (VT_PAD, tk), 0)
        ones_pad = jnp.where(pad_row == 0, 1.0, 0.0).astype(BF16)

        def transpose_chunk(c, carry):
            sl = pl.ds(pl.multiple_of(c * tk, tk), tk)
            vt_ref[0:hd, sl] = v_ref[sl, :].astype(F32).T.astype(BF16)
            vt_ref[hd:hd + VT_PAD, sl] = ones_pad
            return carry
        lax.fori_loop(0, n, transpose_chunk, 0)

        c_idx = lax.broadcasted_iota(jnp.int32, (tk, hd), 0)
        k_lane = lax.broadcasted_iota(jnp.int32, (tk, hd), 1)
        a = _split3(jnp.full((tk, hd), slope2, F32))
        c_lo = (c_idx & (LANES - 1)).astype(F32)
        c_hi = (c_idx >> 7).astype(F32)
        kx = _pick(k_lane, [-a[0], -a[1], -a[2], -LANES * a[0], -LANES * a[1], -LANES * a[2],
                            c_lo, c_lo, c_lo, c_hi, c_hi, c_hi])
        kx_ref[...] = kx.astype(BF16)

        x_row = lax.broadcasted_iota(jnp.int32, (hd, 2 * tq), 0)
        r_idx = lax.broadcasted_iota(jnp.int32, (hd, 2 * tq), 1)
        r_idx = jnp.where(r_idx >= tq, r_idx - tq, r_idx)
        r_lo = (r_idx & (LANES - 1)).astype(F32)
        r_hi = (r_idx >> 7).astype(F32)
        a = _split3(jnp.full((hd, 2 * tq), slope2, F32))
        ext = _pick(x_row, [r_lo, r_lo, r_lo, r_hi, r_hi, r_hi,
                            a[0], a[1], a[2], LANES * a[0], LANES * a[1], LANES * a[2]])
        for side, sign in ((0, 1.0), (1, -1.0), (2, 0.0)):
            qx_ref[side, hd:2 * hd, :] = (sign * ext).astype(BF16)

        rel_ref[...] = (lax.broadcasted_iota(jnp.int32, (tk, tq), 0)
                        - lax.broadcasted_iota(jnp.int32, (tk, tq), 1)).astype(F32)

    qt = (q_ref[...].astype(F32) * (DIFF_HEAD_DIM ** -0.5 * LOG2E)).T
    d_row = lax.broadcasted_iota(jnp.int32, (hd, tq), 0)
    q_main = jnp.concatenate([jnp.where(d_row < DIFF_HEAD_DIM, qt, 0.0),
                              jnp.where(d_row >= DIFF_HEAD_DIM, qt, 0.0)], axis=1).astype(BF16)
    for side in range(3):
        qx_ref[side, 0:hd, :] = q_main

    m_ref[...] = jnp.full(m_ref.shape, -jnp.inf, F32)
    acc_ref[...] = jnp.zeros(acc_ref.shape, F32)

    n_col = 2 * tq // MXU_COLS

    def scores_tile(c, j, side, s_ref, ml_ref, bias=None):
        cs = slice(c * MXU_COLS, (c + 1) * MXU_COLS)
        rows = pl.ds(pl.multiple_of(j * tk, tk), tk)
        kcx = jnp.concatenate([k_ref[rows, :], kx_ref[...]], axis=1)
        s = jnp.dot(kcx, qx_ref[side, :, cs], preferred_element_type=F32)
        if bias is not None:
            b0 = (c * MXU_COLS) % tq
            s = s + bias[:, b0:b0 + MXU_COLS]
        s_ref[:, cs] = s
        ml_ref[:, cs] = jnp.max(s, axis=0, keepdims=True)

    def accumulate_tile(c, j, cst, s_ref, ml_ref):
        cs = slice(c * MXU_COLS, (c + 1) * MXU_COLS)
        m_prev = m_ref[:, cs]
        m_new = jnp.maximum(m_prev, ml_ref[:, cs] + cst)
        alpha = jnp.exp2(m_prev - m_new)
        p = jnp.exp2(s_ref[:, cs] - (m_new - cst)).astype(BF16)
        keys = pl.ds(pl.multiple_of(j * tk, tk), tk)
        acc_ref[:, cs] = alpha * acc_ref[:, cs] + jnp.dot(vt_ref[:, keys], p,
                                                          preferred_element_type=F32)
        m_ref[:, cs] = m_new

    def run(score_units, acc_units):
        assert len(score_units) == len(acc_units)
        for su, au in zip(score_units, acc_units):
            if su is not None:
                scores_tile(*su)
            if au is not None:
                accumulate_tile(*au)

    def tiles(desc, lo, hi):
        return [(c,) + desc for c in range(lo, hi)]

    def chunk(t):
        side = (t >= jd).astype(jnp.int32)
        j = t + side
        sign = (1 - 2 * side).astype(F32)
        cst = -sign * slope2 * (q0 - j * tk).astype(F32)
        return j, side, cst

    lag = min(ATTN_LAG, n_col)
    diag_bias = -slope2 * jnp.abs(rel_ref[...] + (jd * tk - q0).astype(F32))
    diag_s = (jd, 2, s0_ref, ml0_ref, diag_bias)
    diag_a = (jd, 0.0, s0_ref, ml0_ref)
    if n > 1:
        j0, side0, cst0 = chunk(jnp.int32(0))
        run(tiles(diag_s, 0, n_col) + tiles((j0, side0, s1_ref, ml1_ref, None), 0, n_col),
            [None] * lag + tiles(diag_a, 0, n_col)
            + tiles((j0, cst0, s1_ref, ml1_ref), 0, n_col - lag))

        def pair(i, carry):
            ja, _, ca = chunk(2 * i)
            jb, sb, cb = chunk(2 * i + 1)
            jc, sc, cc = chunk(2 * i + 2)
            run(tiles((jb, sb, s0_ref, ml0_ref, None), 0, n_col)
                + tiles((jc, sc, s1_ref, ml1_ref, None), 0, n_col),
                tiles((ja, ca, s1_ref, ml1_ref), n_col - lag, n_col)
                + tiles((jb, cb, s0_ref, ml0_ref), 0, n_col)
                + tiles((jc, cc, s1_ref, ml1_ref), 0, n_col - lag))
            return carry

        lax.fori_loop(0, (n - 2) // 2, pair, 0)
        jl, _, cl = chunk(jnp.int32(n - 2))
        run([None] * lag, tiles((jl, cl, s1_ref, ml1_ref), n_col - lag, n_col))
    else:
        run(tiles(diag_s, 0, n_col), [None] * lag + tiles(diag_a, 0, n_col - lag))
        run([None] * lag, tiles(diag_a, n_col - lag, n_col))

    lam = (jnp.exp(jnp.sum(lq1_ref[...] * lk1_ref[...], axis=-1, keepdims=True))
           - jnp.exp(jnp.sum(lq2_ref[...] * lk2_ref[...], axis=-1, keepdims=True))
           + lam_init)
    acc = acc_ref[...]
    out = acc[0:hd, :] * (1.0 / acc[hd:hd + 1, :])
    o = out[:, 0:tq] - lam * out[:, tq:2 * tq]
    o = o * lax.rsqrt(jnp.mean(o * o, axis=0, keepdims=True) + RMS_EPS)
    o = o * dgt_ref[...] * (1.0 - lam_init)
    o_ref[...] = o.T.astype(BF16)


def _attention(z, slopes, lq1, lk1, lq2, lk2, dg, batch, seq, tq, tk, lam_init):
    T = z.shape[0]
    nq = seq // tq
    hd = DIFF_V_DIM
    lam_spec = pl.BlockSpec((1, DIFF_HEAD_DIM), lambda b, h, i, s: (0, 0))
    grid_spec = pltpu.PrefetchScalarGridSpec(
        num_scalar_prefetch=1,
        grid=(batch, DIFF_HEADS, nq),
        in_specs=[
            lam_spec, lam_spec, lam_spec, lam_spec,
            pl.BlockSpec((hd, 1), lambda b, h, i, s: (0, 0)),
            pl.BlockSpec((tq, hd), lambda b, h, i, s: (b * nq + i, Q_COL128 + h)),
            pl.BlockSpec((seq, hd), lambda b, h, i, s: (b, K_COL128 + h),
                         pipeline_mode=pl.Buffered(1)),
            pl.BlockSpec((seq, hd), lambda b, h, i, s: (b, V_COL128 + h),
                         pipeline_mode=pl.Buffered(1)),
        ],
        out_specs=pl.BlockSpec((tq, hd), lambda b, h, i, s: (b * nq + i, h)),
        scratch_shapes=[
            pltpu.VMEM((3, 2 * hd, 2 * tq), BF16),
            pltpu.VMEM((tk, hd), BF16),
            pltpu.VMEM((hd + VT_PAD, seq), BF16),
            pltpu.VMEM((tk, tq), F32),
            pltpu.VMEM((tk, 2 * tq), F32),
            pltpu.VMEM((tk, 2 * tq), F32),
            pltpu.VMEM((1, 2 * tq), F32),
            pltpu.VMEM((1, 2 * tq), F32),
            pltpu.VMEM((1, 2 * tq), F32),
            pltpu.VMEM((hd + VT_PAD, 2 * tq), F32),
        ],
    )
    return pl.pallas_call(
        functools.partial(_attn_kernel, tq=tq, tk=tk, seq=seq, lam_init=lam_init),
        grid_spec=grid_spec,
        out_shape=jax.ShapeDtypeStruct((T, DIFF_WIDTH), BF16),
        compiler_params=_params(("parallel", "parallel", "arbitrary")),
        name="diff_attn",
    )(slopes, lq1, lk1, lq2, lk2, dg, z, z, z)


def _merge_kernel(x_ref, a_ref, b_ref, g00_ref, g01_ref, g10_ref, g11_ref, bg_ref,
                  wa_ref, wb_ref, wo_ref, o_ref, merged_ref):
    gate_refs = ((g00_ref, g01_ref), (g10_ref, g11_ref))
    half = D_MODEL // 2
    a = a_ref[...]
    b = b_ref[...]
    for c in range(2):
        cols = slice(c * half, (c + 1) * half)
        ga = jax.nn.sigmoid(gate_refs[0][c][...].astype(F32) + bg_ref[0:1, cols])
        gb = jax.nn.sigmoid(gate_refs[1][c][...].astype(F32) + bg_ref[1:2, cols])
        pa = jnp.dot(a, wa_ref[:, cols], preferred_element_type=F32)
        pb = jnp.dot(b, wb_ref[:, cols], preferred_element_type=F32)
        merged_ref[:, cols] = (ga * pa + gb * pb).astype(BF16)
    o_ref[...] = x_ref[...] + jnp.dot(merged_ref[...], wo_ref[...], preferred_element_type=F32)


def _merge(x2, a, b, z, b_gate, wa, wb, wo, tm):
    T, D = x2.shape
    half = D // 2

    def gate_spec(k):
        return pl.BlockSpec((tm, half), lambda i: (i, GATE_COL1024 + k))

    def const_spec(shape):
        return pl.BlockSpec(shape, lambda i: (0, 0), pipeline_mode=pl.Buffered(1))

    return pl.pallas_call(
        _merge_kernel,
        grid=(T // tm,),
        in_specs=[
            pl.BlockSpec((tm, D), lambda i: (i, 0)),
            pl.BlockSpec((tm, SGU_WIDTH), lambda i: (i, 0)),
            pl.BlockSpec((tm, DIFF_WIDTH), lambda i: (i, 0)),
            gate_spec(0), gate_spec(1), gate_spec(2), gate_spec(3),
            const_spec((N_BRANCHES, D)),
            const_spec((SGU_WIDTH, D)),
            const_spec((DIFF_WIDTH, D)),
            const_spec((D, D)),
        ],
        out_specs=pl.BlockSpec((tm, D), lambda i: (i, 0)),
        out_shape=jax.ShapeDtypeStruct((T, D), F32),
        scratch_shapes=[pltpu.VMEM((tm, D), BF16)],
        compiler_params=_params(("parallel",)),
        name="merge_out",
    )(x2, a, b, z, z, z, z, b_gate, wa, wb, wo)


def _split_bf16(x):
    hi = x.astype(BF16)
    lo = (x - hi.astype(F32)).astype(BF16)
    return hi, lo


def _dot3(x_hi, x_lo, w_hi, w_lo):
    return (jnp.dot(x_hi, w_hi, preferred_element_type=F32)
            + jnp.dot(x_lo, w_hi, preferred_element_type=F32)
            + jnp.dot(x_hi, w_lo, preferred_element_type=F32))


def _router(h, rw_hi_ref, rw_lo_ref, rb_ref):
    tm = h.shape[0]
    h_hi, h_lo = _split_bf16(h)
    logits = _dot3(h_hi, h_lo, rw_hi_ref[...], rw_lo_ref[...]) + rb_ref[...]
    lane = lax.broadcasted_iota(jnp.int32, (tm, ROUTER_COLS), 1)
    neg = jnp.float32(-jnp.inf)
    is_group = lane < N_GROUPS
    gl = jnp.where(is_group, logits, neg)
    gmax = jnp.max(gl, axis=-1, keepdims=True)
    gexp = jnp.exp(gl - gmax)
    g_w = 1.0 / jnp.sum(gexp, axis=-1, keepdims=True)
    g_idx = jnp.min(jnp.where(gl == gmax, lane, ROUTER_COLS), axis=-1, keepdims=True)
    e_lo = N_GROUPS + g_idx * EXPERTS_PER_GROUP
    in_group = (lane >= e_lo) & (lane < e_lo + EXPERTS_PER_GROUP)
    el = jnp.where(in_group, logits, neg)
    m1 = jnp.max(el, axis=-1, keepdims=True)
    i1 = jnp.min(jnp.where(el == m1, lane, ROUTER_COLS), axis=-1, keepdims=True)
    el2 = jnp.where(lane == i1, neg, el)
    m2 = jnp.max(el2, axis=-1, keepdims=True)
    i2 = jnp.min(jnp.where(el2 == m2, lane, ROUTER_COLS), axis=-1, keepdims=True)
    t = jnp.exp(m2 - m1)
    w1 = g_w / (1.0 + t)
    w2 = g_w * t / (1.0 + t)
    comb = jnp.where(lane == i1, w1, 0.0) + jnp.where(lane == i2, w2, 0.0)
    return comb, g_idx


META_COLS = LANES


def _route_kernel(x_ref, g_ref, rw_hi_ref, rw_lo_ref, rb_ref, xr_ref, meta_ref, cnt_ref,
                  carry_ref, tri_ref):
    tm, D = x_ref.shape

    @pl.when(pl.program_id(0) == 0)
    def _():
        carry_ref[...] = jnp.zeros(carry_ref.shape, F32)
        r = lax.broadcasted_iota(jnp.int32, (tm, tm), 0)
        c = lax.broadcasted_iota(jnp.int32, (tm, tm), 1)
        tri_ref[...] = jnp.where(c < r, 1.0, 0.0).astype(BF16)

    x = x_ref[...]
    h = x * _rms_scale(x) * g_ref[...]
    comb, g_idx = _router(h, rw_hi_ref, rw_lo_ref, rb_ref)
    xr_ref[:, 0:D] = x
    xr_ref[:, D:D + ROUTER_COLS] = comb

    lane = lax.broadcasted_iota(jnp.int32, (tm, META_COLS), 1)
    onehot = jnp.where(lane == g_idx, 1.0, 0.0)
    earlier = jnp.dot(tri_ref[...], onehot.astype(BF16), preferred_element_type=F32)
    carry = carry_ref[...]
    rank = jnp.sum((earlier + carry) * onehot, axis=-1, keepdims=True).astype(jnp.int32)
    meta_ref[...] = jnp.where(lane == 0, g_idx, jnp.where(lane == 1, rank, 0))
    carry = carry + jnp.sum(onehot, axis=0, keepdims=True)
    carry_ref[...] = carry
    cnt_ref[...] = carry.astype(jnp.int32)


def _route(x2, g, rw_hi, rw_lo, rb, tm):
    T, D = x2.shape
    row = pl.BlockSpec((1, D), lambda i: (0, 0))
    rspec = pl.BlockSpec((D, ROUTER_COLS), lambda i: (0, 0))
    return pl.pallas_call(
        _route_kernel,
        grid=(T // tm,),
        in_specs=[pl.BlockSpec((tm, D), lambda i: (i, 0)), row, rspec, rspec,
                  pl.BlockSpec((1, ROUTER_COLS), lambda i: (0, 0))],
        out_specs=[pl.BlockSpec((tm, D + ROUTER_COLS), lambda i: (i, 0)),
                   pl.BlockSpec((tm, META_COLS), lambda i: (i, 0)),
                   pl.BlockSpec((1, META_COLS), lambda i: (0, 0))],
        out_shape=[jax.ShapeDtypeStruct((T, D + ROUTER_COLS), F32),
                   jax.ShapeDtypeStruct((T, META_COLS), jnp.int32),
                   jax.ShapeDtypeStruct((1, META_COLS), jnp.int32)],
        scratch_shapes=[pltpu.VMEM((1, META_COLS), F32), pltpu.VMEM((tm, tm), BF16)],
        compiler_params=_params(("arbitrary",)),
        name="moe_route",
    )(x2, g, rw_hi, rw_lo, rb)


def _invert_kernel(pos_ref, tok_ref):
    def put(t, carry):
        tok_ref[pos_ref[t]] = t
        return carry
    lax.fori_loop(0, pos_ref.shape[0], put, 0, unroll=16)


def _invert(pos):
    return pl.pallas_call(
        _invert_kernel,
        in_specs=[pl.BlockSpec(memory_space=pltpu.SMEM)],
        out_specs=pl.BlockSpec(memory_space=pltpu.SMEM),
        out_shape=jax.ShapeDtypeStruct(pos.shape, jnp.int32),
        name="moe_invert",
    )(pos)


ITEM_FIRST, ITEM_LAST, ITEM_LIVE, ITEM_FINAL = 1, 2, 4, 8


def _experts_kernel(tok_ref, itile_ref, igroup_ref, iflags_ref, xr_hbm, g_ref, w1_ref, w3_ref,
                    w2_ref, fg_ref, out_hbm, xbuf, obuf, h_ref, comb_ref, gsem, ssem,
                    *, rows, final_norm):
    w = pl.program_id(0)
    e = pl.program_id(1)
    slot = w % 2
    tile = itile_ref[w]
    osl = tile % 2
    flags = iflags_ref[w]
    live = (flags & ITEM_LIVE) != 0
    D = out_hbm.shape[1]
    part = rows // EXPERTS_PER_GROUP

    def gather_copy(t, r, sl):
        tok = tok_ref[t * rows + r]
        return pltpu.make_async_copy(xr_hbm.at[pl.ds(tok, 1), :],
                                     xbuf.at[sl, pl.ds(r, 1), :], gsem.at[sl])

    def scatter_copy(t, r, sl):
        tok = tok_ref[t * rows + r]
        return pltpu.make_async_copy(obuf.at[sl, pl.ds(r, 1), :],
                                     out_hbm.at[pl.ds(tok, 1), :], ssem.at[sl])

    def for_rows(fn, n=rows):
        def body(r, carry):
            fn(r)
            return carry
        lax.fori_loop(0, n, body, 0, unroll=8)

    @pl.when(e == 0)
    def _():
        @pl.when(w == 0)
        def _():
            for_rows(lambda r: gather_copy(tile, r, slot).start())
        for_rows(lambda r: gather_copy(tile, r, slot).wait())

        @pl.when((flags & ITEM_FIRST) != 0)
        def _():
            @pl.when(tile >= 2)
            def _():
                for_rows(lambda r: scatter_copy(tile - 2, r, osl).wait())
            x = xbuf[slot, :, 0:D]
            h_ref[...] = (x * _rms_scale(x) * g_ref[...]).astype(BF16)
            comb_ref[...] = xbuf[slot, :, D:D + ROUTER_COLS]
            obuf[osl] = x

    next_tile = itile_ref[w + 1]

    send_prev = jnp.logical_and((flags & ITEM_FIRST) != 0, tile >= 1)

    def expert_step(with_scatter):
        for r in range(part):
            gather_copy(next_tile, e * part + r, 1 - slot).start()
        if with_scatter:
            for r in range(part):
                scatter_copy(tile - 1, e * part + r, 1 - osl).start()
        h = h_ref[...]
        hid = jax.nn.silu(jnp.dot(h, w1_ref[0], preferred_element_type=F32)) \
            * jnp.dot(h, w3_ref[0], preferred_element_type=F32)
        lane = lax.broadcasted_iota(jnp.int32, (rows, ROUTER_COLS), 1)
        col = N_GROUPS + igroup_ref[w] * EXPERTS_PER_GROUP + e
        c_e = jnp.sum(jnp.where(lane == col, comb_ref[...], 0.0), axis=-1, keepdims=True)
        obuf[osl] += c_e * jnp.dot(hid.astype(BF16), w2_ref[0], preferred_element_type=F32)

    @pl.when(jnp.logical_and(live, send_prev))
    def _():
        expert_step(True)

    @pl.when(jnp.logical_and(live, jnp.logical_not(send_prev)))
    def _():
        expert_step(False)

    @pl.when(jnp.logical_not(live))
    def _():
        for_rows(lambda r: gather_copy(next_tile, e * part + r, 1 - slot).start(), part)

    @pl.when(jnp.logical_and(e == EXPERTS_PER_GROUP - 1, (flags & ITEM_LAST) != 0))
    def _():
        if final_norm:
            y = obuf[osl]
            obuf[osl] = y * _rms_scale(y) * fg_ref[...]

        @pl.when((flags & ITEM_FINAL) != 0)
        def _():
            for_rows(lambda r: scatter_copy(tile, r, osl).start())

            @pl.when(tile >= 1)
            def _():
                for_rows(lambda r: scatter_copy(tile - 1, r, 1 - osl).wait())
            for_rows(lambda r: scatter_copy(tile, r, osl).wait())

    @pl.when(jnp.logical_and(e == EXPERTS_PER_GROUP - 1, w == pl.num_programs(0) - 1))
    def _():
        for_rows(lambda r: gather_copy(next_tile, r, 1 - slot).wait())


def _experts(xr, toks, item_tile, item_group, item_flags, g, w1, w3, w2, final_g, rows,
             final_norm):
    T = xr.shape[0]
    D = D_MODEL
    max_items = item_group.shape[0]

    def expert_block(w, e, tok, itile, igroup, iflags):
        return (igroup[w] * EXPERTS_PER_GROUP + e, 0, 0)

    row = pl.BlockSpec((1, D), lambda w, e, tok, itile, igroup, iflags: (0, 0))
    grid_spec = pltpu.PrefetchScalarGridSpec(
        num_scalar_prefetch=4,
        grid=(max_items, EXPERTS_PER_GROUP),
        in_specs=[
            pl.BlockSpec(memory_space=pl.ANY),
            row,
            pl.BlockSpec((1, D, EXPERT_FF), expert_block),
            pl.BlockSpec((1, D, EXPERT_FF), expert_block),
            pl.BlockSpec((1, EXPERT_FF, D), expert_block),
            row,
        ],
        out_specs=pl.BlockSpec(memory_space=pl.ANY),
        scratch_shapes=[
            pltpu.VMEM((2, rows, D + ROUTER_COLS), F32),
            pltpu.VMEM((2, rows, D), F32),
            pltpu.VMEM((rows, D), BF16),
            pltpu.VMEM((rows, ROUTER_COLS), F32),
            pltpu.SemaphoreType.DMA((2,)),
            pltpu.SemaphoreType.DMA((2,)),
        ],
    )
    return pl.pallas_call(
        functools.partial(_experts_kernel, rows=rows, final_norm=final_norm),
        grid_spec=grid_spec,
        out_shape=jax.ShapeDtypeStruct((T, D), F32),
        compiler_params=_params(("arbitrary", "arbitrary")),
        name="moe_experts",
    )(toks, item_tile, item_group, item_flags, xr, g, w1, w3, w2, final_g)


def _work_items(cnt, rows, n_tiles):
    i32 = jnp.int32
    max_items = n_tiles + N_GROUPS - 1
    ends = jnp.cumsum(cnt)
    starts = ends - cnt
    first_tile = starts // rows
    n_g = jnp.where(cnt > 0, (ends - 1) // rows - first_tile + 1, 0)
    item_end = jnp.cumsum(n_g)
    n_items = item_end[-1]
    w = jnp.arange(max_items + 1, dtype=i32)
    grp = jnp.minimum(jnp.sum(w[:, None] >= item_end[None, :], axis=1), N_GROUPS - 1).astype(i32)
    tile = first_tile[grp] + (w - (item_end - n_g)[grp])
    live = w < n_items
    tile = jnp.where(live, tile, n_tiles - 1).astype(i32)
    prev_tile = jnp.concatenate([jnp.full((1,), -1, i32), tile[:-1]])
    next_tile = jnp.concatenate([tile[1:], jnp.full((1,), -1, i32)])
    final = w == n_items - 1
    first = live & (tile != prev_tile)
    last = live & ((tile != next_tile) | final)
    flags = (first * ITEM_FIRST + last * ITEM_LAST + live * ITEM_LIVE
             + final * ITEM_FINAL).astype(i32)
    return starts, tile, grp[:max_items], flags[:max_items]


def _moe(x2, g, rw_hi, rw_lo, rb, w1, w3, w2, final_g, rows, final_norm):
    T = x2.shape[0]
    xr, meta, counts = _route(x2, g, rw_hi, rw_lo, rb, rows)
    starts, item_tile, item_group, item_flags = _work_items(counts[0, :N_GROUPS], rows, T // rows)
    pos = (starts[meta[:, 0]] + meta[:, 1]).astype(jnp.int32)
    toks = _invert(pos)
    return _experts(xr, toks, item_tile, item_group, item_flags, g, w1, w3, w2, final_g, rows,
                    final_norm)


def _router_weights(rg_w, rg_b, re_w, re_b):
    D = rg_w.shape[0]
    w = jnp.zeros((D, ROUTER_COLS), F32)
    w = w.at[:, :N_GROUPS].set(rg_w).at[:, N_GROUPS:N_GROUPS + N_EXPERTS].set(re_w)
    b = jnp.zeros((1, ROUTER_COLS), F32)
    b = b.at[0, :N_GROUPS].set(rg_b).at[0, N_GROUPS:N_GROUPS + N_EXPERTS].set(re_b)
    hi = w.astype(BF16)
    lo = (w - hi.astype(F32)).astype(BF16)
    return hi, lo, b


def kernel(x, norm1_g, w_in, b_gate, sgu_ln_g, sgu_ln_b, sgu_w, sgu_b, lam_q1, lam_k1, lam_q2,
           lam_k2, diff_norm_g, w_proj_a, w_proj_b, w_out, norm2_g, router_g_w, router_g_b,
           router_e_w, router_e_b, w1, w3, w2, final_g):
    B, S, D = x.shape
    assert D == D_MODEL and w_in.shape[2] == IN_COLS
    depth = w_in.shape[0]
    T = B * S
    slopes = jnp.exp2(-8.0 * jnp.arange(1, DIFF_HEADS + 1, dtype=F32) / DIFF_HEADS)

    tm_in = min(1024, T)
    tm_sgu = min(512, T)
    tq = min(1024, S)
    tk = min(1024, S)
    tm_merge = min(256, T)
    tm_moe = min(512, T)

    x2 = x.reshape(T, D)
    for l in range(depth):
        lam_init = 0.8 - 0.6 * math.exp(-0.3 * l)
        z = _inproj(x2, norm1_g[l][None], w_in[l].astype(BF16), tm_in, 512)
        a = _sgu(z, sgu_ln_g[l][None], sgu_ln_b[l][None], sgu_w[l].astype(BF16),
                 sgu_b[l][:, :, None], tm_sgu)
        o = _attention(z, slopes, lam_q1[l][None], lam_k1[l][None], lam_q2[l][None],
                       lam_k2[l][None], diff_norm_g[l][:, None], B, S, tq, tk, lam_init)
        x2 = _merge(x2, a, o, z, b_gate[l], w_proj_a[l].astype(BF16), w_proj_b[l].astype(BF16),
                    w_out[l].astype(BF16), tm_merge)
        rw_hi, rw_lo, rb = _router_weights(router_g_w[l], router_g_b[l], router_e_w[l],
                                           router_e_b[l])
        x2 = _moe(x2, norm2_g[l][None], rw_hi, rw_lo, rb, w1[l].astype(BF16), w3[l].astype(BF16),
                  w2[l].astype(BF16), final_g[None], tm_moe, final_norm=(l == depth - 1))
    return x2.reshape(B, S, D)
```

```python
import functools
import math

import jax
import jax.numpy as jnp
from jax import lax
from jax.experimental import pallas as pl
from jax.experimental.pallas import tpu as pltpu

F32 = jnp.float32
BF16 = jnp.bfloat16

D_MODEL = 2048
SGU_GROUPS = 8
SGU_WIDTH = 1024
SGU_GROUP_DIM = SGU_WIDTH // SGU_GROUPS
CHUNK = 128
DIFF_HEADS = 8
DIFF_HEAD_DIM = 64
DIFF_V_DIM = 2 * DIFF_HEAD_DIM
DIFF_QK_WIDTH = DIFF_HEADS * 2 * DIFF_HEAD_DIM
DIFF_WIDTH = DIFF_HEADS * DIFF_V_DIM
N_BRANCHES = 2
IN_COLS = 2 * SGU_WIDTH + 2 * DIFF_QK_WIDTH + DIFF_WIDTH + N_BRANCHES * D_MODEL
N_GROUPS = 4
EXPERTS_PER_GROUP = 4
N_EXPERTS = N_GROUPS * EXPERTS_PER_GROUP
EXPERT_FF = 512
RMS_EPS = 1e-6
LN_EPS = 1e-5

Q_COL128 = (2 * SGU_WIDTH) // 128
K_COL128 = (2 * SGU_WIDTH + DIFF_QK_WIDTH) // 128
V_COL128 = (2 * SGU_WIDTH + 2 * DIFF_QK_WIDTH) // 128
GATE_COL1024 = (2 * SGU_WIDTH + 2 * DIFF_QK_WIDTH + DIFF_WIDTH) // 1024

LANES = 128
MXU_COLS = 256
ROUTER_COLS = LANES
VMEM_LIMIT = 56 * 1024 * 1024


def _params(semantics):
    return pltpu.CompilerParams(dimension_semantics=semantics, vmem_limit_bytes=VMEM_LIMIT)


def _gelu(x):
    return 0.5 * x * (1.0 + jnp.tanh(0.7978845608028654 * (x + 0.044715 * (x * x * x))))


def _rms_scale(x):
    return lax.rsqrt(jnp.mean(x * x, axis=-1, keepdims=True) + RMS_EPS)


def _inproj_kernel(x_ref, g_ref, w_ref, z_ref, h_ref):
    @pl.when(pl.program_id(1) == 0)
    def _():
        x = x_ref[...]
        h_ref[...] = (x * _rms_scale(x) * g_ref[...]).astype(BF16)

    z_ref[...] = jnp.dot(h_ref[...], w_ref[...], preferred_element_type=F32).astype(BF16)


def _inproj(x2, g, w_bf16, layer, tm, tn):
    T, D = x2.shape
    N = w_bf16.shape[2]
    return pl.pallas_call(
        _inproj_kernel,
        grid=(T // tm, N // tn),
        in_specs=[
            pl.BlockSpec((tm, D), lambda i, j: (i, 0)),
            pl.BlockSpec((1, D), lambda i, j: (0, 0)),
            pl.BlockSpec((None, D, tn), lambda i, j: (layer, 0, j)),
        ],
        out_specs=pl.BlockSpec((tm, tn), lambda i, j: (i, j)),
        out_shape=jax.ShapeDtypeStruct((T, N), BF16),
        scratch_shapes=[pltpu.VMEM((tm, D), BF16)],
        compiler_params=_params(("parallel", "arbitrary")),
        name="inproj",
    )(x2, g, w_bf16)


def _sgu_kernel(u_ref, v_ref, lng_ref, lnb_ref, ws_ref, bs_ref, a_ref, *, chunks):
    v = _gelu(v_ref[...].astype(F32))
    mu = jnp.mean(v, axis=-1, keepdims=True)
    vc = v - mu
    var = jnp.mean(vc * vc, axis=-1, keepdims=True)
    vn = (vc * lax.rsqrt(var + LN_EPS) * lng_ref[...] + lnb_ref[...]).astype(BF16)
    for c in range(chunks):
        rows = slice(c * CHUNK, (c + 1) * CHUNK)
        for g in range(SGU_GROUPS):
            cols = slice(g * SGU_GROUP_DIM, (g + 1) * SGU_GROUP_DIM)
            mixed = jnp.dot(ws_ref[g], vn[rows, cols], preferred_element_type=F32)
            mixed = mixed + bs_ref[g]
            u = _gelu(u_ref[rows, cols].astype(F32))
            a_ref[rows, cols] = (u * mixed).astype(BF16)


def _sgu(z, ln_g, ln_b, ws_bf16, bs_col, tm):
    T = z.shape[0]
    wblk = SGU_WIDTH
    return pl.pallas_call(
        functools.partial(_sgu_kernel, chunks=tm // CHUNK),
        grid=(T // tm,),
        in_specs=[
            pl.BlockSpec((tm, wblk), lambda i: (i, 0)),
            pl.BlockSpec((tm, wblk), lambda i: (i, 1)),
            pl.BlockSpec((1, wblk), lambda i: (0, 0)),
            pl.BlockSpec((1, wblk), lambda i: (0, 0)),
            pl.BlockSpec((SGU_GROUPS, CHUNK, CHUNK), lambda i: (0, 0, 0)),
            pl.BlockSpec((SGU_GROUPS, CHUNK, 1), lambda i: (0, 0, 0)),
        ],
        out_specs=pl.BlockSpec((tm, wblk), lambda i: (i, 0)),
        out_shape=jax.ShapeDtypeStruct((T, SGU_WIDTH), BF16),
        compiler_params=_params(("parallel",)),
        name="sgu",
    )(z, z, ln_g, ln_b, ws_bf16, bs_col)


LOG2E = 1.4426950408889634
VT_PAD = 16
ATTN_LAG = 2


def _split3(x):
    a1 = x.astype(BF16).astype(F32)
    a2 = (x - a1).astype(BF16).astype(F32)
    a3 = (x - a1 - a2).astype(BF16).astype(F32)
    return a1, a2, a3


def _pick(row, values):
    out = jnp.zeros(values[0].shape, F32)
    for i, v in enumerate(values):
        out = jnp.where(row == i, v, out)
    return out


def _attn_kernel(slopes_ref, lq1_ref, lk1_ref, lq2_ref, lk2_ref, dgt_ref, q_ref, k_ref, v_ref,
                 o_ref, qx_ref, kx_ref, vt_ref, rel_ref, s0_ref, s1_ref, ml0_ref, ml1_ref,
                 m_ref, acc_ref, *, tq, tk, seq, lam_init):
    h = pl.program_id(1)
    qi = pl.program_id(2)
    slope2 = slopes_ref[h] * LOG2E
    hd = DIFF_V_DIM
    n = seq // tk
    q0 = qi * tq
    jd = q0 // tk

    @pl.when(qi == 0)
    def _():
        pad_row = lax.broadcasted_iota(jnp.int32, (VT_PAD, tk), 0)
        ones_pad = jnp.where(pad_row == 0, 1.0, 0.0).astype(BF16)

        def transpose_chunk(c, carry):
            sl = pl.ds(pl.multiple_of(c * tk, tk), tk)
            vt_ref[0:hd, sl] = v_ref[sl, :].astype(F32).T.astype(BF16)
            vt_ref[hd:hd + VT_PAD, sl] = ones_pad
            return carry
        lax.fori_loop(0, n, transpose_chunk, 0)

        c_idx = lax.broadcasted_iota(jnp.int32, (tk, hd), 0)
        k_lane = lax.broadcasted_iota(jnp.int32, (tk, hd), 1)
        a = _split3(jnp.full((tk, hd), slope2, F32))
        c_lo = (c_idx & (LANES - 1)).astype(F32)
        c_hi = (c_idx >> 7).astype(F32)
        kx = _pick(k_lane, [-a[0], -a[1], -a[2], -LANES * a[0], -LANES * a[1], -LANES * a[2],
                            c_lo, c_lo, c_lo, c_hi, c_hi, c_hi])
        kx_ref[...] = kx.astype(BF16)

        x_row = lax.broadcasted_iota(jnp.int32, (hd, 2 * tq), 0)
        r_idx = lax.broadcasted_iota(jnp.int32, (hd, 2 * tq), 1)
        r_idx = jnp.where(r_idx >= tq, r_idx - tq, r_idx)
        r_lo = (r_idx & (LANES - 1)).astype(F32)
        r_hi = (r_idx >> 7).astype(F32)
        a = _split3(jnp.full((hd, 2 * tq), slope2, F32))
        ext = _pick(x_row, [r_lo, r_lo, r_lo, r_hi, r_hi, r_hi,
                            a[0], a[1], a[2], LANES * a[0], LANES * a[1], LANES * a[2]])
        for side, sign in ((0, 1.0), (1, -1.0), (2, 0.0)):
            qx_ref[side, hd:2 * hd, :] = (sign * ext).astype(BF16)

        rel_ref[...] = (lax.broadcasted_iota(jnp.int32, (tk, tq), 0)
                        - lax.broadcasted_iota(jnp.int32, (tk, tq), 1)).astype(F32)

    qt = (q_ref[...].astype(F32) * (DIFF_HEAD_DIM ** -0.5 * LOG2E)).T
    d_row = lax.broadcasted_iota(jnp.int32, (hd, tq), 0)
    q_main = jnp.concatenate([jnp.where(d_row < DIFF_HEAD_DIM, qt, 0.0),
                              jnp.where(d_row >= DIFF_HEAD_DIM, qt, 0.0)], axis=1).astype(BF16)
    for side in range(3):
        qx_ref[side, 0:hd, :] = q_main

    m_ref[...] = jnp.full(m_ref.shape, -jnp.inf, F32)
    acc_ref[...] = jnp.zeros(acc_ref.shape, F32)

    n_col = 2 * tq // MXU_COLS

    def scores_tile(c, j, side, s_ref, ml_ref, bias=None):
        cs = slice(c * MXU_COLS, (c + 1) * MXU_COLS)
        rows = pl.ds(pl.multiple_of(j * tk, tk), tk)
        kcx = jnp.concatenate([k_ref[rows, :], kx_ref[...]], axis=1)
        s = jnp.dot(kcx, qx_ref[side, :, cs], preferred_element_type=F32)
        if bias is not None:
            b0 = (c * MXU_COLS) % tq
            s = s + bias[:, b0:b0 + MXU_COLS]
        s_ref[:, cs] = s
        ml_ref[:, cs] = jnp.max(s, axis=0, keepdims=True)

    def accumulate_tile(c, j, cst, s_ref, ml_ref):
        cs = slice(c * MXU_COLS, (c + 1) * MXU_COLS)
        m_prev = m_ref[:, cs]
        m_new = jnp.maximum(m_prev, ml_ref[:, cs] + cst)
        alpha = jnp.exp2(m_prev - m_new)
        p = jnp.exp2(s_ref[:, cs] - (m_new - cst)).astype(BF16)
        keys = pl.ds(pl.multiple_of(j * tk, tk), tk)
        acc_ref[:, cs] = alpha * acc_ref[:, cs] + jnp.dot(vt_ref[:, keys], p,
                                                          preferred_element_type=F32)
        m_ref[:, cs] = m_new

    def run(score_units, acc_units):
        assert len(score_units) == len(acc_units)
        for su, au in zip(score_units, acc_units):
            if su is not None:
                scores_tile(*su)
            if au is not None:
                accumulate_tile(*au)

    def tiles(desc, lo, hi):
        return [(c,) + desc for c in range(lo, hi)]

    def chunk(t):
        side = (t >= jd).astype(jnp.int32)
        j = t + side
        sign = (1 - 2 * side).astype(F32)
        cst = -sign * slope2 * (q0 - j * tk).astype(F32)
        return j, side, cst

    lag = min(ATTN_LAG, n_col)
    diag_bias = -slope2 * jnp.abs(rel_ref[...] + (jd * tk - q0).astype(F32))
    diag_s = (jd, 2, s0_ref, ml0_ref, diag_bias)
    diag_a = (jd, 0.0, s0_ref, ml0_ref)
    if n > 1:
        j0, side0, cst0 = chunk(jnp.int32(0))
        run(tiles(diag_s, 0, n_col) + tiles((j0, side0, s1_ref, ml1_ref, None), 0, n_col),
            [None] * lag + tiles(diag_a, 0, n_col)
            + tiles((j0, cst0, s1_ref, ml1_ref), 0, n_col - lag))

        def pair(i, carry):
            ja, _, ca = chunk(2 * i)
            jb, sb, cb = chunk(2 * i + 1)
            jc, sc, cc = chunk(2 * i + 2)
            run(tiles((jb, sb, s0_ref, ml0_ref, None), 0, n_col)
                + tiles((jc, sc, s1_ref, ml1_ref, None), 0, n_col),
                tiles((ja, ca, s1_ref, ml1_ref), n_col - lag, n_col)
                + tiles((jb, cb, s0_ref, ml0_ref), 0, n_col)
                + tiles((jc, cc, s1_ref, ml1_ref), 0, n_col - lag))
            return carry

        lax.fori_loop(0, (n - 2) // 2, pair, 0)
        jl, _, cl = chunk(jnp.int32(n - 2))
        run([None] * lag, tiles((jl, cl, s1_ref, ml1_ref), n_col - lag, n_col))
    else:
        run(tiles(diag_s, 0, n_col), [None] * lag + tiles(diag_a, 0, n_col - lag))
        run([None] * lag, tiles(diag_a, n_col - lag, n_col))

    lam = (jnp.exp(jnp.sum(lq1_ref[...] * lk1_ref[...], axis=-1, keepdims=True))
           - jnp.exp(jnp.sum(lq2_ref[...] * lk2_ref[...], axis=-1, keepdims=True))
           + lam_init)
    acc = acc_ref[...]
    out = acc[0:hd, :] * (1.0 / acc[hd:hd + 1, :])
    o = out[:, 0:tq] - lam * out[:, tq:2 * tq]
    o = o * lax.rsqrt(jnp.mean(o * o, axis=0, keepdims=True) + RMS_EPS)
    o = o * dgt_ref[...] * (1.0 - lam_init)
    o_ref[...] = o.T.astype(BF16)


def _attention(z, slopes, lq1, lk1, lq2, lk2, dg, batch, seq, tq, tk, lam_init):
    T = z.shape[0]
    nq = seq // tq
    hd = DIFF_V_DIM
    lam_spec = pl.BlockSpec((1, DIFF_HEAD_DIM), lambda b, h, i, s: (0, 0))
    grid_spec = pltpu.PrefetchScalarGridSpec(
        num_scalar_prefetch=1,
        grid=(batch, DIFF_HEADS, nq),
        in_specs=[
            lam_spec, lam_spec, lam_spec, lam_spec,
            pl.BlockSpec((hd, 1), lambda b, h, i, s: (0, 0)),
            pl.BlockSpec((tq, hd), lambda b, h, i, s: (b * nq + i, Q_COL128 + h)),
            pl.BlockSpec((seq, hd), lambda b, h, i, s: (b, K_COL128 + h),
                         pipeline_mode=pl.Buffered(1)),
            pl.BlockSpec((seq, hd), lambda b, h, i, s: (b, V_COL128 + h),
                         pipeline_mode=pl.Buffered(1)),
        ],
        out_specs=pl.BlockSpec((tq, hd), lambda b, h, i, s: (b * nq + i, h)),
        scratch_shapes=[
            pltpu.VMEM((3, 2 * hd, 2 * tq), BF16),
            pltpu.VMEM((tk, hd), BF16),
            pltpu.VMEM((hd + VT_PAD, seq), BF16),
            pltpu.VMEM((tk, tq), F32),
            pltpu.VMEM((tk, 2 * tq), F32),
            pltpu.VMEM((tk, 2 * tq), F32),
            pltpu.VMEM((1, 2 * tq), F32),
            pltpu.VMEM((1, 2 * tq), F32),
            pltpu.VMEM((1, 2 * tq), F32),
            pltpu.VMEM((hd + VT_PAD, 2 * tq), F32),
        ],
    )
    return pl.pallas_call(
        functools.partial(_attn_kernel, tq=tq, tk=tk, seq=seq, lam_init=lam_init),
        grid_spec=grid_spec,
        out_shape=jax.ShapeDtypeStruct((T, DIFF_WIDTH), BF16),
        compiler_params=_params(("parallel", "parallel", "arbitrary")),
        name="diff_attn",
    )(slopes, lq1, lk1, lq2, lk2, dg, z, z, z)


def _merge_kernel(x_ref, a_ref, b_ref, g00_ref, g01_ref, g10_ref, g11_ref, bg_ref,
                  wa_ref, wb_ref, wo_ref, o_ref, merged_ref):
    gate_refs = ((g00_ref, g01_ref), (g10_ref, g11_ref))
    half = D_MODEL // 2
    a = a_ref[...]
    b = b_ref[...]
    for c in range(2):
        cols = slice(c * half, (c + 1) * half)
        ga = jax.nn.sigmoid(gate_refs[0][c][...].astype(F32) + bg_ref[0:1, cols])
        gb = jax.nn.sigmoid(gate_refs[1][c][...].astype(F32) + bg_ref[1:2, cols])
        pa = jnp.dot(a, wa_ref[:, cols], preferred_element_type=F32)
        pb = jnp.dot(b, wb_ref[:, cols], preferred_element_type=F32)
        merged_ref[:, cols] = (ga * pa + gb * pb).astype(BF16)
    o_ref[...] = x_ref[...] + jnp.dot(merged_ref[...], wo_ref[...], preferred_element_type=F32)


def _merge(x2, a, b, z, b_gate, wa, wb, wo, layer, tm):
    T, D = x2.shape
    half = D // 2

    def gate_spec(k):
        return pl.BlockSpec((tm, half), lambda i: (i, GATE_COL1024 + k))

    def const_spec(shape):
        return pl.BlockSpec((None,) + shape, lambda i: (layer, 0, 0),
                            pipeline_mode=pl.Buffered(1))

    return pl.pallas_call(
        _merge_kernel,
        grid=(T // tm,),
        in_specs=[
            pl.BlockSpec((tm, D), lambda i: (i, 0)),
            pl.BlockSpec((tm, SGU_WIDTH), lambda i: (i, 0)),
            pl.BlockSpec((tm, DIFF_WIDTH), lambda i: (i, 0)),
            gate_spec(0), gate_spec(1), gate_spec(2), gate_spec(3),
            const_spec((N_BRANCHES, D)),
            const_spec((SGU_WIDTH, D)),
            const_spec((DIFF_WIDTH, D)),
            const_spec((D, D)),
        ],
        out_specs=pl.BlockSpec((tm, D), lambda i: (i, 0)),
        out_shape=jax.ShapeDtypeStruct((T, D), F32),
        scratch_shapes=[pltpu.VMEM((tm, D), BF16)],
        compiler_params=_params(("parallel",)),
        name="merge_out",
    )(x2, a, b, z, z, z, z, b_gate, wa, wb, wo)


def _split_bf16(x):
    hi = x.astype(BF16)
    lo = (x - hi.astype(F32)).astype(BF16)
    return hi, lo


def _dot3(x_hi, x_lo, w_hi, w_lo):
    return (jnp.dot(x_hi, w_hi, preferred_element_type=F32)
            + jnp.dot(x_lo, w_hi, preferred_element_type=F32)
            + jnp.dot(x_hi, w_lo, preferred_element_type=F32))


def _router(h, rw_hi_ref, rw_lo_ref, rb_ref):
    tm = h.shape[0]
    h_hi, h_lo = _split_bf16(h)
    logits = _dot3(h_hi, h_lo, rw_hi_ref[...], rw_lo_ref[...]) + rb_ref[...]
    lane = lax.broadcasted_iota(jnp.int32, (tm, ROUTER_COLS), 1)
    neg = jnp.float32(-jnp.inf)
    is_group = lane < N_GROUPS
    gl = jnp.where(is_group, logits, neg)
    gmax = jnp.max(gl, axis=-1, keepdims=True)
    gexp = jnp.exp(gl - gmax)
    g_w = 1.0 / jnp.sum(gexp, axis=-1, keepdims=True)
    g_idx = jnp.min(jnp.where(gl == gmax, lane, ROUTER_COLS), axis=-1, keepdims=True)
    e_lo = N_GROUPS + g_idx * EXPERTS_PER_GROUP
    in_group = (lane >= e_lo) & (lane < e_lo + EXPERTS_PER_GROUP)
    el = jnp.where(in_group, logits, neg)
    m1 = jnp.max(el, axis=-1, keepdims=True)
    i1 = jnp.min(jnp.where(el == m1, lane, ROUTER_COLS), axis=-1, keepdims=True)
    el2 = jnp.where(lane == i1, neg, el)
    m2 = jnp.max(el2, axis=-1, keepdims=True)
    i2 = jnp.min(jnp.where(el2 == m2, lane, ROUTER_COLS), axis=-1, keepdims=True)
    t = jnp.exp(m2 - m1)
    w1 = g_w / (1.0 + t)
    w2 = g_w * t / (1.0 + t)
    comb = jnp.where(lane == i1, w1, 0.0) + jnp.where(lane == i2, w2, 0.0)
    return comb, g_idx


META_COLS = LANES


def _route_kernel(x_ref, g_ref, rw_hi_ref, rw_lo_ref, rb_ref, xr_ref, meta_ref, cnt_ref,
                  carry_ref, tri_ref):
    tm, D = x_ref.shape

    @pl.when(pl.program_id(0) == 0)
    def _():
        carry_ref[...] = jnp.zeros(carry_ref.shape, F32)
        r = lax.broadcasted_iota(jnp.int32, (tm, tm), 0)
        c = lax.broadcasted_iota(jnp.int32, (tm, tm), 1)
        tri_ref[...] = jnp.where(c < r, 1.0, 0.0).astype(BF16)

    x = x_ref[...]
    h = x * _rms_scale(x) * g_ref[...]
    comb, g_idx = _router(h, rw_hi_ref, rw_lo_ref, rb_ref)
    xr_ref[:, 0:D] = x
    xr_ref[:, D:D + ROUTER_COLS] = comb

    lane = lax.broadcasted_iota(jnp.int32, (tm, META_COLS), 1)
    onehot = jnp.where(lane == g_idx, 1.0, 0.0)
    earlier = jnp.dot(tri_ref[...], onehot.astype(BF16), preferred_element_type=F32)
    carry = carry_ref[...]
    rank = jnp.sum((earlier + carry) * onehot, axis=-1, keepdims=True).astype(jnp.int32)
    meta_ref[...] = jnp.where(lane == 0, g_idx, jnp.where(lane == 1, rank, 0))
    carry = carry + jnp.sum(onehot, axis=0, keepdims=True)
    carry_ref[...] = carry
    cnt_ref[...] = carry.astype(jnp.int32)


def _route(x2, g, rw_hi, rw_lo, rb, tm):
    T, D = x2.shape
    row = pl.BlockSpec((1, D), lambda i: (0, 0))
    rspec = pl.BlockSpec((D, ROUTER_COLS), lambda i: (0, 0))
    return pl.pallas_call(
        _route_kernel,
        grid=(T // tm,),
        in_specs=[pl.BlockSpec((tm, D), lambda i: (i, 0)), row, rspec, rspec,
                  pl.BlockSpec((1, ROUTER_COLS), lambda i: (0, 0))],
        out_specs=[pl.BlockSpec((tm, D + ROUTER_COLS), lambda i: (i, 0)),
                   pl.BlockSpec((tm, META_COLS), lambda i: (i, 0)),
                   pl.BlockSpec((1, META_COLS), lambda i: (0, 0))],
        out_shape=[jax.ShapeDtypeStruct((T, D + ROUTER_COLS), F32),
                   jax.ShapeDtypeStruct((T, META_COLS), jnp.int32),
                   jax.ShapeDtypeStruct((1, META_COLS), jnp.int32)],
        scratch_shapes=[pltpu.VMEM((1, META_COLS), F32), pltpu.VMEM((tm, tm), BF16)],
        compiler_params=_params(("arbitrary",)),
        name="moe_route",
    )(x2, g, rw_hi, rw_lo, rb)


def _invert_kernel(pos_ref, tok_ref):
    def put(t, carry):
        tok_ref[pos_ref[t]] = t
        return carry
    lax.fori_loop(0, pos_ref.shape[0], put, 0, unroll=16)


def _invert(pos):
    return pl.pallas_call(
        _invert_kernel,
        in_specs=[pl.BlockSpec(memory_space=pltpu.SMEM)],
        out_specs=pl.BlockSpec(memory_space=pltpu.SMEM),
        out_shape=jax.ShapeDtypeStruct(pos.shape, jnp.int32),
        name="moe_invert",
    )(pos)


ITEM_FIRST, ITEM_LAST, ITEM_LIVE, ITEM_FINAL = 1, 2, 4, 8
GATHER_STEPS = EXPERTS_PER_GROUP // 2


def _experts_kernel(tok_ref, itile_ref, igroup_ref, iflags_ref, xr_hbm, g_ref, w1_ref, w3_ref,
                    w2_ref, fg_ref, out_hbm, xbuf, obuf, h_ref, comb_ref, gsem, ssem,
                    *, rows, final_norm):
    w = pl.program_id(0)
    e = pl.program_id(1)
    slot = w % 2
    tile = itile_ref[w]
    osl = tile % 2
    flags = iflags_ref[w]
    live = (flags & ITEM_LIVE) != 0
    D = out_hbm.shape[1]
    part = rows // GATHER_STEPS

    def gather_copy(t, r, sl):
        tok = tok_ref[t * rows + r]
        return pltpu.make_async_copy(xr_hbm.at[pl.ds(tok, 1), :],
                                     xbuf.at[sl, pl.ds(r, 1), :], gsem.at[sl])

    def scatter_copy(t, r, sl):
        tok = tok_ref[t * rows + r]
        return pltpu.make_async_copy(obuf.at[sl, pl.ds(r, 1), :],
                                     out_hbm.at[pl.ds(tok, 1), :], ssem.at[sl])

    def for_rows(fn, n=rows):
        def body(r, carry):
            fn(r)
            return carry
        lax.fori_loop(0, n, body, 0, unroll=8)

    @pl.when(e == 0)
    def _():
        @pl.when(w == 0)
        def _():
            for_rows(lambda r: gather_copy(tile, r, slot).start())
        for_rows(lambda r: gather_copy(tile, r, slot).wait())

        @pl.when((flags & ITEM_FIRST) != 0)
        def _():
            @pl.when(tile >= 2)
            def _():
                for_rows(lambda r: scatter_copy(tile - 2, r, osl).wait())
            x = xbuf[slot, :, 0:D]
            h_ref[...] = (x * _rms_scale(x) * g_ref[...]).astype(BF16)
            comb_ref[...] = xbuf[slot, :, D:D + ROUTER_COLS]
            obuf[osl] = x

    next_tile = itile_ref[w + 1]

    send_prev = jnp.logical_and((flags & ITEM_FIRST) != 0, tile >= 1)

    def expert_step(dma):
        if dma == "gather":
            for r in range(part):
                gather_copy(next_tile, e * part + r, 1 - slot).start()
        elif dma == "scatter":
            for r in range(part):
                scatter_copy(tile - 1, (e - GATHER_STEPS) * part + r, 1 - osl).start()
        h = h_ref[...]
        hid = jax.nn.silu(jnp.dot(h, w1_ref[0], preferred_element_type=F32)) \
            * jnp.dot(h, w3_ref[0], preferred_element_type=F32)
        lane = lax.broadcasted_iota(jnp.int32, (rows, ROUTER_COLS), 1)
        col = N_GROUPS + igroup_ref[w] * EXPERTS_PER_GROUP + e
        c_e = jnp.sum(jnp.where(lane == col, comb_ref[...], 0.0), axis=-1, keepdims=True)
        obuf[osl] += c_e * jnp.dot(hid.astype(BF16), w2_ref[0], preferred_element_type=F32)

    early = e < GATHER_STEPS

    @pl.when(jnp.logical_and(live, early))
    def _():
        expert_step("gather")

    @pl.when(jnp.logical_and(live, jnp.logical_and(jnp.logical_not(early), send_prev)))
    def _():
        expert_step("scatter")

    @pl.when(jnp.logical_and(live, jnp.logical_and(jnp.logical_not(early),
                                                   jnp.logical_not(send_prev))))
    def _():
        expert_step("plain")

    @pl.when(jnp.logical_and(jnp.logical_not(live), early))
    def _():
        for_rows(lambda r: gather_copy(next_tile, e * part + r, 1 - slot).start(), part)

    @pl.when(jnp.logical_and(e == EXPERTS_PER_GROUP - 1, (flags & ITEM_LAST) != 0))
    def _():
        if final_norm:
            y = obuf[osl]
            obuf[osl] = y * _rms_scale(y) * fg_ref[...]

        @pl.when((flags & ITEM_FINAL) != 0)
        def _():
            for_rows(lambda r: scatter_copy(tile, r, osl).start())

            @pl.when(tile >= 1)
            def _():
                for_rows(lambda r: scatter_copy(tile - 1, r, 1 - osl).wait())
            for_rows(lambda r: scatter_copy(tile, r, osl).wait())

    @pl.when(jnp.logical_and(e == EXPERTS_PER_GROUP - 1, w == pl.num_programs(0) - 1))
    def _():
        for_rows(lambda r: gather_copy(next_tile, r, 1 - slot).wait())


def _experts(xr, toks, item_tile, item_group, item_flags, g, w1, w3, w2, final_g, layer, rows,
             final_norm):
    T = xr.shape[0]
    D = D_MODEL
    max_items = item_group.shape[0]

    def expert_block(w, e, tok, itile, igroup, iflags):
        return (layer * N_EXPERTS + igroup[w] * EXPERTS_PER_GROUP + e, 0, 0)

    row = pl.BlockSpec((1, D), lambda w, e, tok, itile, igroup, iflags: (0, 0))
    grid_spec = pltpu.PrefetchScalarGridSpec(
        num_scalar_prefetch=4,
        grid=(max_items, EXPERTS_PER_GROUP),
        in_specs=[
            pl.BlockSpec(memory_space=pl.ANY),
            row,
            pl.BlockSpec((1, D, EXPERT_FF), expert_block),
            pl.BlockSpec((1, D, EXPERT_FF), expert_block),
            pl.BlockSpec((1, EXPERT_FF, D), expert_block),
            row,
        ],
        out_specs=pl.BlockSpec(memory_space=pl.ANY),
        scratch_shapes=[
            pltpu.VMEM((2, rows, D + ROUTER_COLS), F32),
            pltpu.VMEM((2, rows, D), F32),
            pltpu.VMEM((rows, D), BF16),
            pltpu.VMEM((rows, ROUTER_COLS), F32),
            pltpu.SemaphoreType.DMA((2,)),
            pltpu.SemaphoreType.DMA((2,)),
        ],
    )
    return pl.pallas_call(
        functools.partial(_experts_kernel, rows=rows, final_norm=final_norm),
        grid_spec=grid_spec,
        out_shape=jax.ShapeDtypeStruct((T, D), F32),
        compiler_params=_params(("arbitrary", "arbitrary")),
        name="moe_experts",
    )(toks, item_tile, item_group, item_flags, xr, g, w1, w3, w2, final_g)


def _work_items(cnt, rows, n_tiles):
    i32 = jnp.int32
    max_items = n_tiles + N_GROUPS - 1
    ends = jnp.cumsum(cnt)
    starts = ends - cnt
    first_tile = starts // rows
    n_g = jnp.where(cnt > 0, (ends - 1) // rows - first_tile + 1, 0)
    item_end = jnp.cumsum(n_g)
    n_items = item_end[-1]
    w = jnp.arange(max_items + 1, dtype=i32)
    grp = jnp.minimum(jnp.sum(w[:, None] >= item_end[None, :], axis=1), N_GROUPS - 1).astype(i32)
    tile = first_tile[grp] + (w - (item_end - n_g)[grp])
    live = w < n_items
    tile = jnp.where(live, tile, n_tiles - 1).astype(i32)
    prev_tile = jnp.concatenate([jnp.full((1,), -1, i32), tile[:-1]])
    next_tile = jnp.concatenate([tile[1:], jnp.full((1,), -1, i32)])
    final = w == n_items - 1
    first = live & (tile != prev_tile)
    last = live & ((tile != next_tile) | final)
    flags = (first * ITEM_FIRST + last * ITEM_LAST + live * ITEM_LIVE
             + final * ITEM_FINAL).astype(i32)
    return starts, tile, grp[:max_items], flags[:max_items]


def _moe(x2, g, rw_hi, rw_lo, rb, w1, w3, w2, final_g, layer, rows, final_norm):
    T = x2.shape[0]
    xr, meta, counts = _route(x2, g, rw_hi, rw_lo, rb, rows)
    starts, item_tile, item_group, item_flags = _work_items(counts[0, :N_GROUPS], rows, T // rows)
    pos = (starts[meta[:, 0]] + meta[:, 1]).astype(jnp.int32)
    toks = _invert(pos)
    return _experts(xr, toks, item_tile, item_group, item_flags, g, w1, w3, w2, final_g, layer,
                    rows, final_norm)


def _router_weights(rg_w, rg_b, re_w, re_b):
    D = rg_w.shape[0]
    w = jnp.zeros((D, ROUTER_COLS), F32)
    w = w.at[:, :N_GROUPS].set(rg_w).at[:, N_GROUPS:N_GROUPS + N_EXPERTS].set(re_w)
    b = jnp.zeros((1, ROUTER_COLS), F32)
    b = b.at[0, :N_GROUPS].set(rg_b).at[0, N_GROUPS:N_GROUPS + N_EXPERTS].set(re_b)
    hi = w.astype(BF16)
    lo = (w - hi.astype(F32)).astype(BF16)
    return hi, lo, b


def kernel(x, norm1_g, w_in, b_gate, sgu_ln_g, sgu_ln_b, sgu_w, sgu_b, lam_q1, lam_k1, lam_q2,
           lam_k2, diff_norm_g, w_proj_a, w_proj_b, w_out, norm2_g, router_g_w, router_g_b,
           router_e_w, router_e_b, w1, w3, w2, final_g):
    B, S, D = x.shape
    assert D == D_MODEL and w_in.shape[2] == IN_COLS
    depth = w_in.shape[0]
    T = B * S
    slopes = jnp.exp2(-8.0 * jnp.arange(1, DIFF_HEADS + 1, dtype=F32) / DIFF_HEADS)

    tm_in = min(1024, T)
    tm_sgu = min(512, T)
    tq = min(1024, S)
    tk = min(1024, S)
    tm_merge = min(256, T)
    tm_moe = min(512, T)

    w_in_b = w_in.astype(BF16)
    wa_b, wb_b, wo_b = w_proj_a.astype(BF16), w_proj_b.astype(BF16), w_out.astype(BF16)
    w1_b = w1.astype(BF16).reshape(depth * N_EXPERTS, D, EXPERT_FF)
    w3_b = w3.astype(BF16).reshape(depth * N_EXPERTS, D, EXPERT_FF)
    w2_b = w2.astype(BF16).reshape(depth * N_EXPERTS, EXPERT_FF, D)

    x2 = x.reshape(T, D)
    for l in range(depth):
        lam_init = 0.8 - 0.6 * math.exp(-0.3 * l)
        z = _inproj(x2, norm1_g[l][None], w_in_b, l, tm_in, 512)
        a = _sgu(z, sgu_ln_g[l][None], sgu_ln_b[l][None], sgu_w[l].astype(BF16),
                 sgu_b[l][:, :, None], tm_sgu)
        o = _attention(z, slopes, lam_q1[l][None], lam_k1[l][None], lam_q2[l][None],
                       lam_k2[l][None], diff_norm_g[l][:, None], B, S, tq, tk, lam_init)
        x2 = _merge(x2, a, o, z, b_gate, wa_b, wb_b, wo_b, l, tm_merge)
        rw_hi, rw_lo, rb = _router_weights(router_g_w[l], router_g_b[l], router_e_w[l],
                                           router_e_b[l])
        x2 = _moe(x2, norm2_g[l][None], rw_hi, rw_lo, rb, w1_b, w3_b, w2_b, final_g[None], l,
                  tm_moe, final_norm=(l == depth - 1))
    return x2.reshape(B, S, D)
```

```python
import functools
import math
from typing import NamedTuple

import jax
import jax.numpy as jnp
from jax import lax
from jax.experimental import pallas as pl
from jax.experimental.pallas import tpu as pltpu

F32 = jnp.float32
BF16 = jnp.bfloat16

D_MODEL = 2048
SGU_GROUPS = 8
SGU_WIDTH = 1024
SGU_GROUP_DIM = SGU_WIDTH // SGU_GROUPS
CHUNK = 128
DIFF_HEADS = 8
DIFF_HEAD_DIM = 64
DIFF_V_DIM = 2 * DIFF_HEAD_DIM
DIFF_QK_WIDTH = DIFF_HEADS * 2 * DIFF_HEAD_DIM
DIFF_WIDTH = DIFF_HEADS * DIFF_V_DIM
N_BRANCHES = 2
IN_COLS = 2 * SGU_WIDTH + 2 * DIFF_QK_WIDTH + DIFF_WIDTH + N_BRANCHES * D_MODEL
N_GROUPS = 4
EXPERTS_PER_GROUP = 4
N_EXPERTS = N_GROUPS * EXPERTS_PER_GROUP
EXPERT_FF = 512
RMS_EPS = 1e-6
LN_EPS = 1e-5

Q_COL128 = (2 * SGU_WIDTH) // 128
K_COL128 = (2 * SGU_WIDTH + DIFF_QK_WIDTH) // 128
V_COL128 = (2 * SGU_WIDTH + 2 * DIFF_QK_WIDTH) // 128
GATE_COL1024 = (2 * SGU_WIDTH + 2 * DIFF_QK_WIDTH + DIFF_WIDTH) // 1024

LANES = 128
MXU_COLS = 256
ROUTER_COLS = LANES
VMEM_LIMIT = 56 * 1024 * 1024


def _params(semantics):
    return pltpu.CompilerParams(dimension_semantics=semantics, vmem_limit_bytes=VMEM_LIMIT)


def _gelu(x):
    return 0.5 * x * (1.0 + jnp.tanh(0.7978845608028654 * (x + 0.044715 * (x * x * x))))


def _rms_scale(x):
    return lax.rsqrt(jnp.mean(x * x, axis=-1, keepdims=True) + RMS_EPS)


def _inproj_kernel(x_ref, g_ref, w_ref, z_ref, h_ref):
    @pl.when(pl.program_id(1) == 0)
    def _():
        x = x_ref[...]
        h_ref[...] = (x * _rms_scale(x) * g_ref[...]).astype(BF16)

    z_ref[...] = jnp.dot(h_ref[...], w_ref[...], preferred_element_type=F32).astype(BF16)


def _inproj(x2, g, w_bf16, layer, tm, tn):
    T, D = x2.shape
    N = w_bf16.shape[2]
    return pl.pallas_call(
        _inproj_kernel,
        grid=(T // tm, N // tn),
        in_specs=[
            pl.BlockSpec((tm, D), lambda i, j: (i, 0)),
            pl.BlockSpec((1, D), lambda i, j: (0, 0)),
            pl.BlockSpec((None, D, tn), lambda i, j: (layer, 0, j)),
        ],
        out_specs=pl.BlockSpec((tm, tn), lambda i, j: (i, j)),
        out_shape=jax.ShapeDtypeStruct((T, N), BF16),
        scratch_shapes=[pltpu.VMEM((tm, D), BF16)],
        compiler_params=_params(("parallel", "arbitrary")),
        name="inproj",
    )(x2, g, w_bf16)


def _sgu_kernel(u_ref, v_ref, lng_ref, lnb_ref, ws_ref, bs_ref, a_ref, *, chunks):
    v = _gelu(v_ref[...].astype(F32))
    mu = jnp.mean(v, axis=-1, keepdims=True)
    vc = v - mu
    var = jnp.mean(vc * vc, axis=-1, keepdims=True)
    vn = (vc * lax.rsqrt(var + LN_EPS) * lng_ref[...] + lnb_ref[...]).astype(BF16)
    for c in range(chunks):
        rows = slice(c * CHUNK, (c + 1) * CHUNK)
        for g in range(SGU_GROUPS):
            cols = slice(g * SGU_GROUP_DIM, (g + 1) * SGU_GROUP_DIM)
            mixed = jnp.dot(ws_ref[g], vn[rows, cols], preferred_element_type=F32)
            mixed = mixed + bs_ref[g]
            u = _gelu(u_ref[rows, cols].astype(F32))
            a_ref[rows, cols] = (u * mixed).astype(BF16)


def _sgu(z, ln_g, ln_b, ws_bf16, bs_col, tm):
    T = z.shape[0]
    wblk = SGU_WIDTH
    return pl.pallas_call(
        functools.partial(_sgu_kernel, chunks=tm // CHUNK),
        grid=(T // tm,),
        in_specs=[
            pl.BlockSpec((tm, wblk), lambda i: (i, 0)),
            pl.BlockSpec((tm, wblk), lambda i: (i, 1)),
            pl.BlockSpec((1, wblk), lambda i: (0, 0)),
            pl.BlockSpec((1, wblk), lambda i: (0, 0)),
            pl.BlockSpec((SGU_GROUPS, CHUNK, CHUNK), lambda i: (0, 0, 0)),
            pl.BlockSpec((SGU_GROUPS, CHUNK, 1), lambda i: (0, 0, 0)),
        ],
        out_specs=pl.BlockSpec((tm, wblk), lambda i: (i, 0)),
        out_shape=jax.ShapeDtypeStruct((T, SGU_WIDTH), BF16),
        compiler_params=_params(("parallel",)),
        name="sgu",
    )(z, z, ln_g, ln_b, ws_bf16, bs_col)


LOG2E = 1.4426950408889634
VT_PAD = 16
ATTN_LAG = 2


def _split3(x):
    a1 = x.astype(BF16).astype(F32)
    a2 = (x - a1).astype(BF16).astype(F32)
    a3 = (x - a1 - a2).astype(BF16).astype(F32)
    return a1, a2, a3


def _pick(row, values):
    out = jnp.zeros(values[0].shape, F32)
    for i, v in enumerate(values):
        out = jnp.where(row == i, v, out)
    return out


def _attn_kernel(slopes_ref, lq1_ref, lk1_ref, lq2_ref, lk2_ref, dgt_ref, q_ref, k_ref, v_ref,
                 o_ref, qx_ref, kx_ref, vt_ref, rel_ref, s0_ref, s1_ref, ml0_ref, ml1_ref,
                 m_ref, acc_ref, *, tq, tk, seq, lam_init):
    h = pl.program_id(1)
    qi = pl.program_id(2)
    slope2 = slopes_ref[h] * LOG2E
    hd = DIFF_V_DIM
    n = seq // tk
    q0 = qi * tq
    jd = q0 // tk

    @pl.when(qi == 0)
    def _():
        pad_row = lax.broadcasted_iota(jnp.int32, (VT_PAD, tk), 0)
        ones_pad = jnp.where(pad_row == 0, 1.0, 0.0).astype(BF16)

        def transpose_chunk(c, carry):
            sl = pl.ds(pl.multiple_of(c * tk, tk), tk)
            vt_ref[0:hd, sl] = v_ref[sl, :].astype(F32).T.astype(BF16)
            vt_ref[hd:hd + VT_PAD, sl] = ones_pad
            return carry
        lax.fori_loop(0, n, transpose_chunk, 0)

        c_idx = lax.broadcasted_iota(jnp.int32, (tk, hd), 0)
        k_lane = lax.broadcasted_iota(jnp.int32, (tk, hd), 1)
        a = _split3(jnp.full((tk, hd), slope2, F32))
        c_lo = (c_idx & (LANES - 1)).astype(F32)
        c_hi = (c_idx >> 7).astype(F32)
        kx = _pick(k_lane, [-a[0], -a[1], -a[2], -LANES * a[0], -LANES * a[1], -LANES * a[2],
                            c_lo, c_lo, c_lo, c_hi, c_hi, c_hi])
        kx_ref[...] = kx.astype(BF16)

        x_row = lax.broadcasted_iota(jnp.int32, (hd, 2 * tq), 0)
        r_idx = lax.broadcasted_iota(jnp.int32, (hd, 2 * tq), 1)
        r_idx = jnp.where(r_idx >= tq, r_idx - tq, r_idx)
        r_lo = (r_idx & (LANES - 1)).astype(F32)
        r_hi = (r_idx >> 7).astype(F32)
        a = _split3(jnp.full((hd, 2 * tq), slope2, F32))
        ext = _pick(x_row, [r_lo, r_lo, r_lo, r_hi, r_hi, r_hi,
                            a[0], a[1], a[2], LANES * a[0], LANES * a[1], LANES * a[2]])
        for side, sign in ((0, 1.0), (1, -1.0), (2, 0.0)):
            qx_ref[side, hd:2 * hd, :] = (sign * ext).astype(BF16)

        rel_ref[...] = (lax.broadcasted_iota(jnp.int32, (tk, tq), 0)
                        - lax.broadcasted_iota(jnp.int32, (tk, tq), 1)).astype(F32)

    qt = (q_ref[...].astype(F32) * (DIFF_HEAD_DIM ** -0.5 * LOG2E)).T
    d_row = lax.broadcasted_iota(jnp.int32, (hd, tq), 0)
    q_main = jnp.concatenate([jnp.where(d_row < DIFF_HEAD_DIM, qt, 0.0),
                              jnp.where(d_row >= DIFF_HEAD_DIM, qt, 0.0)], axis=1).astype(BF16)
    for side in range(3):
        qx_ref[side, 0:hd, :] = q_main

    m_ref[...] = jnp.full(m_ref.shape, -jnp.inf, F32)
    acc_ref[...] = jnp.zeros(acc_ref.shape, F32)

    n_col = 2 * tq // MXU_COLS

    def scores_tile(c, j, side, s_ref, ml_ref, bias=None):
        cs = slice(c * MXU_COLS, (c + 1) * MXU_COLS)
        rows = pl.ds(pl.multiple_of(j * tk, tk), tk)
        kcx = jnp.concatenate([k_ref[rows, :], kx_ref[...]], axis=1)
        s = jnp.dot(kcx, qx_ref[side, :, cs], preferred_element_type=F32)
        if bias is not None:
            b0 = (c * MXU_COLS) % tq
            s = s + bias[:, b0:b0 + MXU_COLS]
        s_ref[:, cs] = s
        ml_ref[:, cs] = jnp.max(s, axis=0, keepdims=True)

    def accumulate_tile(c, j, cst, s_ref, ml_ref):
        cs = slice(c * MXU_COLS, (c + 1) * MXU_COLS)
        m_prev = m_ref[:, cs]
        m_new = jnp.maximum(m_prev, ml_ref[:, cs] + cst)
        alpha = jnp.exp2(m_prev - m_new)
        p = jnp.exp2(s_ref[:, cs] - (m_new - cst)).astype(BF16)
        keys = pl.ds(pl.multiple_of(j * tk, tk), tk)
        acc_ref[:, cs] = alpha * acc_ref[:, cs] + jnp.dot(vt_ref[:, keys], p,
                                                          preferred_element_type=F32)
        m_ref[:, cs] = m_new

    def run(score_units, acc_units):
        assert len(score_units) == len(acc_units)
        for su, au in zip(score_units, acc_units):
            if su is not None:
                scores_tile(*su)
            if au is not None:
                accumulate_tile(*au)

    def tiles(desc, lo, hi):
        return [(c,) + desc for c in range(lo, hi)]

    def chunk(t):
        side = (t >= jd).astype(jnp.int32)
        j = t + side
        sign = (1 - 2 * side).astype(F32)
        cst = -sign * slope2 * (q0 - j * tk).astype(F32)
        return j, side, cst

    lag = min(ATTN_LAG, n_col)
    diag_bias = -slope2 * jnp.abs(rel_ref[...] + (jd * tk - q0).astype(F32))
    diag_s = (jd, 2, s0_ref, ml0_ref, diag_bias)
    diag_a = (jd, 0.0, s0_ref, ml0_ref)
    if n > 1:
        j0, side0, cst0 = chunk(jnp.int32(0))
        run(tiles(diag_s, 0, n_col) + tiles((j0, side0, s1_ref, ml1_ref, None), 0, n_col),
            [None] * lag + tiles(diag_a, 0, n_col)
            + tiles((j0, cst0, s1_ref, ml1_ref), 0, n_col - lag))

        def pair(i, carry):
            ja, _, ca = chunk(2 * i)
            jb, sb, cb = chunk(2 * i + 1)
            jc, sc, cc = chunk(2 * i + 2)
            run(tiles((jb, sb, s0_ref, ml0_ref, None), 0, n_col)
                + tiles((jc, sc, s1_ref, ml1_ref, None), 0, n_col),
                tiles((ja, ca, s1_ref, ml1_ref), n_col - lag, n_col)
                + tiles((jb, cb, s0_ref, ml0_ref), 0, n_col)
                + tiles((jc, cc, s1_ref, ml1_ref), 0, n_col - lag))
            return carry

        lax.fori_loop(0, (n - 2) // 2, pair, 0)
        jl, _, cl = chunk(jnp.int32(n - 2))
        run([None] * lag, tiles((jl, cl, s1_ref, ml1_ref), n_col - lag, n_col))
    else:
        run(tiles(diag_s, 0, n_col), [None] * lag + tiles(diag_a, 0, n_col - lag))
        run([None] * lag, tiles(diag_a, n_col - lag, n_col))

    lam = (jnp.exp(jnp.sum(lq1_ref[...] * lk1_ref[...], axis=-1, keepdims=True))
           - jnp.exp(jnp.sum(lq2_ref[...] * lk2_ref[...], axis=-1, keepdims=True))
           + lam_init)
    acc = acc_ref[...]
    out = acc[0:hd, :] * (1.0 / acc[hd:hd + 1, :])
    o = out[:, 0:tq] - lam * out[:, tq:2 * tq]
    o = o * lax.rsqrt(jnp.mean(o * o, axis=0, keepdims=True) + RMS_EPS)
    o = o * dgt_ref[...] * (1.0 - lam_init)
    o_ref[...] = o.T.astype(BF16)


def _attention(z, slopes, lq1, lk1, lq2, lk2, dg, batch, seq, tq, tk, lam_init):
    T = z.shape[0]
    nq = seq // tq
    hd = DIFF_V_DIM
    lam_spec = pl.BlockSpec((1, DIFF_HEAD_DIM), lambda b, h, i, s: (0, 0))
    grid_spec = pltpu.PrefetchScalarGridSpec(
        num_scalar_prefetch=1,
        grid=(batch, DIFF_HEADS, nq),
        in_specs=[
            lam_spec, lam_spec, lam_spec, lam_spec,
            pl.BlockSpec((hd, 1), lambda b, h, i, s: (0, 0)),
            pl.BlockSpec((tq, hd), lambda b, h, i, s: (b * nq + i, Q_COL128 + h)),
            pl.BlockSpec((seq, hd), lambda b, h, i, s: (b, K_COL128 + h)),
            pl.BlockSpec((seq, hd), lambda b, h, i, s: (b, V_COL128 + h)),
        ],
        out_specs=pl.BlockSpec((tq, hd), lambda b, h, i, s: (b * nq + i, h)),
        scratch_shapes=[
            pltpu.VMEM((3, 2 * hd, 2 * tq), BF16),
            pltpu.VMEM((tk, hd), BF16),
            pltpu.VMEM((hd + VT_PAD, seq), BF16),
            pltpu.VMEM((tk, tq), F32),
            pltpu.VMEM((tk, 2 * tq), F32),
            pltpu.VMEM((tk, 2 * tq), F32),
            pltpu.VMEM((1, 2 * tq), F32),
            pltpu.VMEM((1, 2 * tq), F32),
            pltpu.VMEM((1, 2 * tq), F32),
            pltpu.VMEM((hd + VT_PAD, 2 * tq), F32),
        ],
    )
    return pl.pallas_call(
        functools.partial(_attn_kernel, tq=tq, tk=tk, seq=seq, lam_init=lam_init),
        grid_spec=grid_spec,
        out_shape=jax.ShapeDtypeStruct((T, DIFF_WIDTH), BF16),
        compiler_params=_params(("parallel", "parallel", "arbitrary")),
        name="diff_attn",
    )(slopes, lq1, lk1, lq2, lk2, dg, z, z, z)


def _merge_kernel(x_ref, a_ref, b_ref, g00_ref, g01_ref, g10_ref, g11_ref, bg_ref,
                  wa_ref, wb_ref, wo_ref, o_ref, merged_ref):
    gate_refs = ((g00_ref, g01_ref), (g10_ref, g11_ref))
    half = D_MODEL // 2
    a = a_ref[...]
    b = b_ref[...]
    for c in range(2):
        cols = slice(c * half, (c + 1) * half)
        ga = jax.nn.sigmoid(gate_refs[0][c][...].astype(F32) + bg_ref[0:1, cols])
        gb = jax.nn.sigmoid(gate_refs[1][c][...].astype(F32) + bg_ref[1:2, cols])
        pa = jnp.dot(a, wa_ref[:, cols], preferred_element_type=F32)
        pb = jnp.dot(b, wb_ref[:, cols], preferred_element_type=F32)
        merged_ref[:, cols] = (ga * pa + gb * pb).astype(BF16)
    o_ref[...] = x_ref[...] + jnp.dot(merged_ref[...], wo_ref[...], preferred_element_type=F32)


def _merge(x2, a, b, z, b_gate, wa, wb, wo, layer, tm):
    T, D = x2.shape
    half = D // 2

    def gate_spec(k):
        return pl.BlockSpec((tm, half), lambda i: (i, GATE_COL1024 + k))

    def const_spec(shape):
        return pl.BlockSpec((None,) + shape, lambda i: (layer, 0, 0),
                            pipeline_mode=pl.Buffered(1))

    return pl.pallas_call(
        _merge_kernel,
        grid=(T // tm,),
        in_specs=[
            pl.BlockSpec((tm, D), lambda i: (i, 0)),
            pl.BlockSpec((tm, SGU_WIDTH), lambda i: (i, 0)),
            pl.BlockSpec((tm, DIFF_WIDTH), lambda i: (i, 0)),
            gate_spec(0), gate_spec(1), gate_spec(2), gate_spec(3),
            const_spec((N_BRANCHES, D)),
            const_spec((SGU_WIDTH, D)),
            const_spec((DIFF_WIDTH, D)),
            const_spec((D, D)),
        ],
        out_specs=pl.BlockSpec((tm, D), lambda i: (i, 0)),
        out_shape=jax.ShapeDtypeStruct((T, D), F32),
        scratch_shapes=[pltpu.VMEM((tm, D), BF16)],
        compiler_params=_params(("parallel",)),
        name="merge_out",
    )(x2, a, b, z, z, z, z, b_gate, wa, wb, wo)


def _split_bf16(x):
    hi = x.astype(BF16)
    lo = (x - hi.astype(F32)).astype(BF16)
    return hi, lo


def _dot3(x_hi, x_lo, w_hi, w_lo):
    return (jnp.dot(x_hi, w_hi, preferred_element_type=F32)
            + jnp.dot(x_lo, w_hi, preferred_element_type=F32)
            + jnp.dot(x_hi, w_lo, preferred_element_type=F32))


def _router(h, rw_hi_ref, rw_lo_ref, rb_ref):
    tm = h.shape[0]
    h_hi, h_lo = _split_bf16(h)
    logits = _dot3(h_hi, h_lo, rw_hi_ref[...], rw_lo_ref[...]) + rb_ref[...]
    lane = lax.broadcasted_iota(jnp.int32, (tm, ROUTER_COLS), 1)
    neg = jnp.float32(-jnp.inf)
    is_group = lane < N_GROUPS
    gl = jnp.where(is_group, logits, neg)
    gmax = jnp.max(gl, axis=-1, keepdims=True)
    gexp = jnp.exp(gl - gmax)
    g_w = 1.0 / jnp.sum(gexp, axis=-1, keepdims=True)
    g_idx = jnp.min(jnp.where(gl == gmax, lane, ROUTER_COLS), axis=-1, keepdims=True)
    e_lo = N_GROUPS + g_idx * EXPERTS_PER_GROUP
    in_group = (lane >= e_lo) & (lane < e_lo + EXPERTS_PER_GROUP)
    el = jnp.where(in_group, logits, neg)
    m1 = jnp.max(el, axis=-1, keepdims=True)
    i1 = jnp.min(jnp.where(el == m1, lane, ROUTER_COLS), axis=-1, keepdims=True)
    el2 = jnp.where(lane == i1, neg, el)
    m2 = jnp.max(el2, axis=-1, keepdims=True)
    i2 = jnp.min(jnp.where(el2 == m2, lane, ROUTER_COLS), axis=-1, keepdims=True)
    t = jnp.exp(m2 - m1)
    w1 = g_w / (1.0 + t)
    w2 = g_w * t / (1.0 + t)
    comb = jnp.where(lane == i1, w1, 0.0) + jnp.where(lane == i2, w2, 0.0)
    return comb, g_idx


META_COLS = LANES


def _route_kernel(x_ref, g_ref, rw_hi_ref, rw_lo_ref, rb_ref, xr_ref, meta_ref, cnt_ref,
                  carry_ref, tri_ref):
    tm, D = x_ref.shape

    @pl.when(pl.program_id(0) == 0)
    def _():
        carry_ref[...] = jnp.zeros(carry_ref.shape, F32)
        r = lax.broadcasted_iota(jnp.int32, (tm, tm), 0)
        c = lax.broadcasted_iota(jnp.int32, (tm, tm), 1)
        tri_ref[...] = jnp.where(c < r, 1.0, 0.0).astype(BF16)

    x = x_ref[...]
    h = x * _rms_scale(x) * g_ref[...]
    comb, g_idx = _router(h, rw_hi_ref, rw_lo_ref, rb_ref)
    xr_ref[:, 0:D] = x
    xr_ref[:, D:D + ROUTER_COLS] = comb

    lane = lax.broadcasted_iota(jnp.int32, (tm, META_COLS), 1)
    onehot = jnp.where(lane == g_idx, 1.0, 0.0)
    earlier = jnp.dot(tri_ref[...], onehot.astype(BF16), preferred_element_type=F32)
    carry = carry_ref[...]
    rank = jnp.sum((earlier + carry) * onehot, axis=-1, keepdims=True).astype(jnp.int32)
    meta_ref[...] = jnp.where(lane == 0, g_idx, jnp.where(lane == 1, rank, 0))
    carry = carry + jnp.sum(onehot, axis=0, keepdims=True)
    carry_ref[...] = carry
    cnt_ref[...] = carry.astype(jnp.int32)


def _route(x2, g, rw_hi, rw_lo, rb, tm):
    T, D = x2.shape
    row = pl.BlockSpec((1, D), lambda i: (0, 0))
    rspec = pl.BlockSpec((D, ROUTER_COLS), lambda i: (0, 0))
    return pl.pallas_call(
        _route_kernel,
        grid=(T // tm,),
        in_specs=[pl.BlockSpec((tm, D), lambda i: (i, 0)), row, rspec, rspec,
                  pl.BlockSpec((1, ROUTER_COLS), lambda i: (0, 0))],
        out_specs=[pl.BlockSpec((tm, D + ROUTER_COLS), lambda i: (i, 0)),
                   pl.BlockSpec((tm, META_COLS), lambda i: (i, 0)),
                   pl.BlockSpec((1, META_COLS), lambda i: (0, 0))],
        out_shape=[jax.ShapeDtypeStruct((T, D + ROUTER_COLS), F32),
                   jax.ShapeDtypeStruct((T, META_COLS), jnp.int32),
                   jax.ShapeDtypeStruct((1, META_COLS), jnp.int32)],
        scratch_shapes=[pltpu.VMEM((1, META_COLS), F32), pltpu.VMEM((tm, tm), BF16)],
        compiler_params=_params(("arbitrary",)),
        name="moe_route",
    )(x2, g, rw_hi, rw_lo, rb)


def _invert_kernel(pos_ref, tok_ref):
    def put(t, carry):
        tok_ref[pos_ref[t]] = t
        return carry
    lax.fori_loop(0, pos_ref.shape[0], put, 0, unroll=16)


def _invert(pos):
    return pl.pallas_call(
        _invert_kernel,
        in_specs=[pl.BlockSpec(memory_space=pltpu.SMEM)],
        out_specs=pl.BlockSpec(memory_space=pltpu.SMEM),
        out_shape=jax.ShapeDtypeStruct(pos.shape, jnp.int32),
        name="moe_invert",
    )(pos)


ITEM_FIRST, ITEM_LAST, ITEM_LIVE, ITEM_FINAL = 1, 2, 4, 8
GATHER_STEPS = EXPERTS_PER_GROUP // 2


def _experts_kernel(tok_ref, itile_ref, igroup_ref, iflags_ref, xr_hbm, g_ref, w1_ref, w3_ref,
                    w2_ref, fg_ref, out_hbm, xbuf, obuf, h_ref, comb_ref, gsem, ssem,
                    *, rows, final_norm):
    w = pl.program_id(0)
    e = pl.program_id(1)
    slot = w % 2
    tile = itile_ref[w]
    osl = tile % 2
    flags = iflags_ref[w]
    live = (flags & ITEM_LIVE) != 0
    D = out_hbm.shape[1]
    part = rows // GATHER_STEPS

    def gather_copy(t, r, sl):
        tok = tok_ref[t * rows + r]
        return pltpu.make_async_copy(xr_hbm.at[pl.ds(tok, 1), :],
                                     xbuf.at[sl, pl.ds(r, 1), :], gsem.at[sl])

    def scatter_copy(t, r, sl):
        tok = tok_ref[t * rows + r]
        return pltpu.make_async_copy(obuf.at[sl, pl.ds(r, 1), :],
                                     out_hbm.at[pl.ds(tok, 1), :], ssem.at[sl])

    def for_rows(fn, n=rows):
        def body(r, carry):
            fn(r)
            return carry
        lax.fori_loop(0, n, body, 0, unroll=8)

    @pl.when(e == 0)
    def _():
        @pl.when(w == 0)
        def _():
            for_rows(lambda r: gather_copy(tile, r, slot).start())
        for_rows(lambda r: gather_copy(tile, r, slot).wait())

        @pl.when((flags & ITEM_FIRST) != 0)
        def _():
            @pl.when(tile >= 2)
            def _():
                for_rows(lambda r: scatter_copy(tile - 2, r, osl).wait())
            x = xbuf[slot, :, 0:D]
            h_ref[...] = (x * _rms_scale(x) * g_ref[...]).astype(BF16)
            comb_ref[...] = xbuf[slot, :, D:D + ROUTER_COLS]
            obuf[osl] = x

    next_tile = itile_ref[w + 1]

    send_prev = jnp.logical_and((flags & ITEM_FIRST) != 0, tile >= 1)

    def expert_step(dma):
        if dma == "gather":
            for r in range(part):
                gather_copy(next_tile, e * part + r, 1 - slot).start()
        elif dma == "scatter":
            for r in range(part):
                scatter_copy(tile - 1, (e - GATHER_STEPS) * part + r, 1 - osl).start()
        h = h_ref[...]
        hid = jax.nn.silu(jnp.dot(h, w1_ref[0], preferred_element_type=F32)) \
            * jnp.dot(h, w3_ref[0], preferred_element_type=F32)
        lane = lax.broadcasted_iota(jnp.int32, (rows, ROUTER_COLS), 1)
        col = N_GROUPS + igroup_ref[w] * EXPERTS_PER_GROUP + e
        c_e = jnp.sum(jnp.where(lane == col, comb_ref[...], 0.0), axis=-1, keepdims=True)
        obuf[osl] += c_e * jnp.dot(hid.astype(BF16), w2_ref[0], preferred_element_type=F32)

    early = e < GATHER_STEPS

    @pl.when(jnp.logical_and(live, early))
    def _():
        expert_step("gather")

    @pl.when(jnp.logical_and(live, jnp.logical_and(jnp.logical_not(early), send_prev)))
    def _():
        expert_step("scatter")

    @pl.when(jnp.logical_and(live, jnp.logical_and(jnp.logical_not(early),
                                                   jnp.logical_not(send_prev))))
    def _():
        expert_step("plain")

    @pl.when(jnp.logical_and(jnp.logical_not(live), early))
    def _():
        for_rows(lambda r: gather_copy(next_tile, e * part + r, 1 - slot).start(), part)

    @pl.when(jnp.logical_and(e == EXPERTS_PER_GROUP - 1, (flags & ITEM_LAST) != 0))
    def _():
        if final_norm:
            y = obuf[osl]
            obuf[osl] = y * _rms_scale(y) * fg_ref[...]

        @pl.when((flags & ITEM_FINAL) != 0)
        def _():
            for_rows(lambda r: scatter_copy(tile, r, osl).start())

            @pl.when(tile >= 1)
            def _():
                for_rows(lambda r: scatter_copy(tile - 1, r, 1 - osl).wait())
            for_rows(lambda r: scatter_copy(tile, r, osl).wait())

    @pl.when(jnp.logical_and(e == EXPERTS_PER_GROUP - 1, w == pl.num_programs(0) - 1))
    def _():
        for_rows(lambda r: gather_copy(next_tile, r, 1 - slot).wait())


def _experts(xr, toks, item_tile, item_group, item_flags, g, w1, w3, w2, final_g, layer, rows,
             final_norm):
    T = xr.shape[0]
    D = D_MODEL
    max_items = item_group.shape[0]

    def expert_block(w, e, tok, itile, igroup, iflags):
        return (layer * N_EXPERTS + igroup[w] * EXPERTS_PER_GROUP + e, 0, 0)

    row = pl.BlockSpec((1, D), lambda w, e, tok, itile, igroup, iflags: (0, 0))
    grid_spec = pltpu.PrefetchScalarGridSpec(
        num_scalar_prefetch=4,
        grid=(max_items, EXPERTS_PER_GROUP),
        in_specs=[
            pl.BlockSpec(memory_space=pl.ANY),
            row,
            pl.BlockSpec((1, D, EXPERT_FF), expert_block),
            pl.BlockSpec((1, D, EXPERT_FF), expert_block),
            pl.BlockSpec((1, EXPERT_FF, D), expert_block),
            row,
        ],
        out_specs=pl.BlockSpec(memory_space=pl.ANY),
        scratch_shapes=[
            pltpu.VMEM((2, rows, D + ROUTER_COLS), F32),
            pltpu.VMEM((2, rows, D), F32),
            pltpu.VMEM((rows, D), BF16),
            pltpu.VMEM((rows, ROUTER_COLS), F32),
            pltpu.SemaphoreType.DMA((2,)),
            pltpu.SemaphoreType.DMA((2,)),
        ],
    )
    return pl.pallas_call(
        functools.partial(_experts_kernel, rows=rows, final_norm=final_norm),
        grid_spec=grid_spec,
        out_shape=jax.ShapeDtypeStruct((T, D), F32),
        compiler_params=_params(("arbitrary", "arbitrary")),
        name="moe_experts",
    )(toks, item_tile, item_group, item_flags, xr, g, w1, w3, w2, final_g)


def _work_items(cnt, rows, n_tiles):
    i32 = jnp.int32
    max_items = n_tiles + N_GROUPS - 1
    ends = jnp.cumsum(cnt)
    starts = ends - cnt
    first_tile = starts // rows
    n_g = jnp.where(cnt > 0, (ends - 1) // rows - first_tile + 1, 0)
    item_end = jnp.cumsum(n_g)
    n_items = item_end[-1]
    w = jnp.arange(max_items + 1, dtype=i32)
    grp = jnp.minimum(jnp.sum(w[:, None] >= item_end[None, :], axis=1), N_GROUPS - 1).astype(i32)
    tile = first_tile[grp] + (w - (item_end - n_g)[grp])
    live = w < n_items
    tile = jnp.where(live, tile, n_tiles - 1).astype(i32)
    prev_tile = jnp.concatenate([jnp.full((1,), -1, i32), tile[:-1]])
    next_tile = jnp.concatenate([tile[1:], jnp.full((1,), -1, i32)])
    final = w == n_items - 1
    first = live & (tile != prev_tile)
    last = live & ((tile != next_tile) | final)
    flags = (first * ITEM_FIRST + last * ITEM_LAST + live * ITEM_LIVE
             + final * ITEM_FINAL).astype(i32)
    return starts, tile, grp[:max_items], flags[:max_items]


def _moe(x2, g, rw_hi, rw_lo, rb, w1, w3, w2, final_g, layer, rows, final_norm):
    T = x2.shape[0]
    xr, meta, counts = _route(x2, g, rw_hi, rw_lo, rb, rows)
    starts, item_tile, item_group, item_flags = _work_items(counts[0, :N_GROUPS], rows, T // rows)
    pos = (starts[meta[:, 0]] + meta[:, 1]).astype(jnp.int32)
    toks = _invert(pos)
    return _experts(xr, toks, item_tile, item_group, item_flags, g, w1, w3, w2, final_g, layer,
                    rows, final_norm)


def _router_weights(rg_w, rg_b, re_w, re_b):
    D = rg_w.shape[0]
    w = jnp.zeros((D, ROUTER_COLS), F32)
    w = w.at[:, :N_GROUPS].set(rg_w).at[:, N_GROUPS:N_GROUPS + N_EXPERTS].set(re_w)
    b = jnp.zeros((1, ROUTER_COLS), F32)
    b = b.at[0, :N_GROUPS].set(rg_b).at[0, N_GROUPS:N_GROUPS + N_EXPERTS].set(re_b)
    hi = w.astype(BF16)
    lo = (w - hi.astype(F32)).astype(BF16)
    return hi, lo, b


class _Tiles(NamedTuple):
    inproj_rows: int
    inproj_cols: int
    sgu_rows: int
    attn_q: int
    attn_k: int
    merge_rows: int
    moe_rows: int


def _tile_plan(n_tokens, seq):
    return _Tiles(inproj_rows=min(1024, n_tokens), inproj_cols=1024,
                  sgu_rows=min(512, n_tokens), attn_q=min(1024, seq), attn_k=min(1024, seq),
                  merge_rows=min(256, n_tokens), moe_rows=min(512, n_tokens))


def kernel(x, norm1_g, w_in, b_gate, sgu_ln_g, sgu_ln_b, sgu_w, sgu_b, lam_q1, lam_k1, lam_q2,
           lam_k2, diff_norm_g, w_proj_a, w_proj_b, w_out, norm2_g, router_g_w, router_g_b,
           router_e_w, router_e_b, w1, w3, w2, final_g):
    B, S, D = x.shape
    assert D == D_MODEL and w_in.shape[2] == IN_COLS
    depth = w_in.shape[0]
    T = B * S
    slopes = jnp.exp2(-8.0 * jnp.arange(1, DIFF_HEADS + 1, dtype=F32) / DIFF_HEADS)
    tiles = _tile_plan(T, S)

    w_in_b = w_in.astype(BF16)
    wa_b, wb_b, wo_b = w_proj_a.astype(BF16), w_proj_b.astype(BF16), w_out.astype(BF16)
    w1_b = w1.astype(BF16).reshape(depth * N_EXPERTS, D, EXPERT_FF)
    w3_b = w3.astype(BF16).reshape(depth * N_EXPERTS, D, EXPERT_FF)
    w2_b = w2.astype(BF16).reshape(depth * N_EXPERTS, EXPERT_FF, D)

    x2 = x.reshape(T, D)
    for l in range(depth):
        lam_init = 0.8 - 0.6 * math.exp(-0.3 * l)
        z = _inproj(x2, norm1_g[l][None], w_in_b, l, tiles.inproj_rows, tiles.inproj_cols)
        a = _sgu(z, sgu_ln_g[l][None], sgu_ln_b[l][None], sgu_w[l].astype(BF16),
                 sgu_b[l][:, :, None], tiles.sgu_rows)
        o = _attention(z, slopes, lam_q1[l][None], lam_k1[l][None], lam_q2[l][None],
                       lam_k2[l][None], diff_norm_g[l][:, None], B, S, tiles.attn_q,
                       tiles.attn_k, lam_init)
        x2 = _merge(x2, a, o, z, b_gate, wa_b, wb_b, wo_b, l, tiles.merge_rows)
        rw_hi, rw_lo, rb = _router_weights(router_g_w[l], router_g_b[l], router_e_w[l],
                                           router_e_b[l])
        x2 = _moe(x2, norm2_g[l][None], rw_hi, rw_lo, rb, w1_b, w3_b, w2_b, final_g[None], l,
                  tiles.moe_rows, final_norm=(l == depth - 1))
    return x2.reshape(B, S, D)
```

```python
import functools
import math
from typing import NamedTuple

import jax
import jax.numpy as jnp
from jax import lax
from jax.experimental import pallas as pl
from jax.experimental.pallas import tpu as pltpu

F32 = jnp.float32
BF16 = jnp.bfloat16

D_MODEL = 2048
SGU_GROUPS = 8
SGU_WIDTH = 1024
SGU_GROUP_DIM = SGU_WIDTH // SGU_GROUPS
CHUNK = 128
DIFF_HEADS = 8
DIFF_HEAD_DIM = 64
DIFF_V_DIM = 2 * DIFF_HEAD_DIM
DIFF_QK_WIDTH = DIFF_HEADS * 2 * DIFF_HEAD_DIM
DIFF_WIDTH = DIFF_HEADS * DIFF_V_DIM
N_BRANCHES = 2
IN_COLS = 2 * SGU_WIDTH + 2 * DIFF_QK_WIDTH + DIFF_WIDTH + N_BRANCHES * D_MODEL
N_GROUPS = 4
EXPERTS_PER_GROUP = 4
N_EXPERTS = N_GROUPS * EXPERTS_PER_GROUP
EXPERT_FF = 512
RMS_EPS = 1e-6
LN_EPS = 1e-5

Q_COL128 = (2 * SGU_WIDTH) // 128
K_COL128 = (2 * SGU_WIDTH + DIFF_QK_WIDTH) // 128
V_COL128 = (2 * SGU_WIDTH + 2 * DIFF_QK_WIDTH) // 128
GATE_COL1024 = (2 * SGU_WIDTH + 2 * DIFF_QK_WIDTH + DIFF_WIDTH) // 1024

LANES = 128
MXU_COLS = 256
ROUTER_COLS = LANES
VMEM_LIMIT = 56 * 1024 * 1024


def _params(semantics):
    return pltpu.CompilerParams(dimension_semantics=semantics, vmem_limit_bytes=VMEM_LIMIT)


def _gelu(x):
    return 0.5 * x * (1.0 + jnp.tanh(0.7978845608028654 * (x + 0.044715 * (x * x * x))))


def _rms_scale(x):
    return lax.rsqrt(jnp.mean(x * x, axis=-1, keepdims=True) + RMS_EPS)


def _inproj_kernel(x_ref, g_ref, w_ref, z_ref, h_ref):
    @pl.when(pl.program_id(1) == 0)
    def _():
        x = x_ref[...]
        h_ref[...] = (x * _rms_scale(x) * g_ref[...]).astype(BF16)

    z_ref[...] = jnp.dot(h_ref[...], w_ref[...], preferred_element_type=F32).astype(BF16)


def _inproj(x2, g, w_bf16, layer, tm, tn):
    T, D = x2.shape
    N = w_bf16.shape[2]
    return pl.pallas_call(
        _inproj_kernel,
        grid=(T // tm, N // tn),
        in_specs=[
            pl.BlockSpec((tm, D), lambda i, j: (i, 0)),
            pl.BlockSpec((1, D), lambda i, j: (0, 0)),
            pl.BlockSpec((None, D, tn), lambda i, j: (layer, 0, j)),
        ],
        out_specs=pl.BlockSpec((tm, tn), lambda i, j: (i, j)),
        out_shape=jax.ShapeDtypeStruct((T, N), BF16),
        scratch_shapes=[pltpu.VMEM((tm, D), BF16)],
        compiler_params=_params(("parallel", "arbitrary")),
        name="inproj",
    )(x2, g, w_bf16)


def _sgu_kernel(u_ref, v_ref, lng_ref, lnb_ref, ws_ref, bs_ref, a_ref, *, chunks):
    v = _gelu(v_ref[...].astype(F32))
    mu = jnp.mean(v, axis=-1, keepdims=True)
    vc = v - mu
    var = jnp.mean(vc * vc, axis=-1, keepdims=True)
    vn = (vc * lax.rsqrt(var + LN_EPS) * lng_ref[...] + lnb_ref[...]).astype(BF16)
    for c in range(chunks):
        rows = slice(c * CHUNK, (c + 1) * CHUNK)
        for g in range(SGU_GROUPS):
            cols = slice(g * SGU_GROUP_DIM, (g + 1) * SGU_GROUP_DIM)
            mixed = jnp.dot(ws_ref[g], vn[rows, cols], preferred_element_type=F32)
            mixed = mixed + bs_ref[g]
            u = _gelu(u_ref[rows, cols].astype(F32))
            a_ref[rows, cols] = (u * mixed).astype(BF16)


def _sgu(z, ln_g, ln_b, ws_bf16, bs_col, tm):
    T = z.shape[0]
    wblk = SGU_WIDTH
    return pl.pallas_call(
        functools.partial(_sgu_kernel, chunks=tm // CHUNK),
        grid=(T // tm,),
        in_specs=[
            pl.BlockSpec((tm, wblk), lambda i: (i, 0)),
            pl.BlockSpec((tm, wblk), lambda i: (i, 1)),
            pl.BlockSpec((1, wblk), lambda i: (0, 0)),
            pl.BlockSpec((1, wblk), lambda i: (0, 0)),
            pl.BlockSpec((SGU_GROUPS, CHUNK, CHUNK), lambda i: (0, 0, 0)),
            pl.BlockSpec((SGU_GROUPS, CHUNK, 1), lambda i: (0, 0, 0)),
        ],
        out_specs=pl.BlockSpec((tm, wblk), lambda i: (i, 0)),
        out_shape=jax.ShapeDtypeStruct((T, SGU_WIDTH), BF16),
        compiler_params=_params(("parallel",)),
        name="sgu",
    )(z, z, ln_g, ln_b, ws_bf16, bs_col)


LOG2E = 1.4426950408889634
VT_PAD = 16
ATTN_LAG = 3


def _split3(x):
    a1 = x.astype(BF16).astype(F32)
    a2 = (x - a1).astype(BF16).astype(F32)
    a3 = (x - a1 - a2).astype(BF16).astype(F32)
    return a1, a2, a3


def _pick(row, values):
    out = jnp.zeros(values[0].shape, F32)
    for i, v in enumerate(values):
        out = jnp.where(row == i, v, out)
    return out


def _attn_kernel(slopes_ref, lq1_ref, lk1_ref, lq2_ref, lk2_ref, dgt_ref, q_ref, k_ref, v_ref,
                 o_ref, qx_ref, kx_ref, vt_ref, rel_ref, s0_ref, s1_ref, ml0_ref, ml1_ref,
                 m_ref, acc_ref, *, tq, tk, seq, lam_init):
    h = pl.program_id(1)
    qi = pl.program_id(2)
    slope2 = slopes_ref[h] * LOG2E
    hd = DIFF_V_DIM
    n = seq // tk
    q0 = qi * tq
    jd = q0 // tk

    @pl.when(qi == 0)
    def _():
        pad_row = lax.broadcasted_iota(jnp.int32, (VT_PAD, tk), 0)
        ones_pad = jnp.where(pad_row == 0, 1.0, 0.0).astype(BF16)

        def transpose_chunk(c, carry):
            sl = pl.ds(pl.multiple_of(c * tk, tk), tk)
            vt_ref[0:hd, sl] = v_ref[sl, :].astype(F32).T.astype(BF16)
            vt_ref[hd:hd + VT_PAD, sl] = ones_pad
            return carry
        lax.fori_loop(0, n, transpose_chunk, 0)

        c_idx = lax.broadcasted_iota(jnp.int32, (tk, hd), 0)
        k_lane = lax.broadcasted_iota(jnp.int32, (tk, hd), 1)
        a = _split3(jnp.full((tk, hd), slope2, F32))
        c_lo = (c_idx & (LANES - 1)).astype(F32)
        c_hi = (c_idx >> 7).astype(F32)
        kx = _pick(k_lane, [-a[0], -a[1], -a[2], -LANES * a[0], -LANES * a[1], -LANES * a[2],
                            c_lo, c_lo, c_lo, c_hi, c_hi, c_hi])
        kx_ref[...] = kx.astype(BF16)

        x_row = lax.broadcasted_iota(jnp.int32, (hd, 2 * tq), 0)
        r_idx = lax.broadcasted_iota(jnp.int32, (hd, 2 * tq), 1)
        r_idx = jnp.where(r_idx >= tq, r_idx - tq, r_idx)
        r_lo = (r_idx & (LANES - 1)).astype(F32)
        r_hi = (r_idx >> 7).astype(F32)
        a = _split3(jnp.full((hd, 2 * tq), slope2, F32))
        ext = _pick(x_row, [r_lo, r_lo, r_lo, r_hi, r_hi, r_hi,
                            a[0], a[1], a[2], LANES * a[0], LANES * a[1], LANES * a[2]])
        for side, sign in ((0, 1.0), (1, -1.0), (2, 0.0)):
            qx_ref[side, hd:2 * hd, :] = (sign * ext).astype(BF16)

        rel_ref[...] = (lax.broadcasted_iota(jnp.int32, (tk, tq), 0)
                        - lax.broadcasted_iota(jnp.int32, (tk, tq), 1)).astype(F32)

    qt = (q_ref[...].astype(F32) * (DIFF_HEAD_DIM ** -0.5 * LOG2E)).T
    d_row = lax.broadcasted_iota(jnp.int32, (hd, tq), 0)
    q_main = jnp.concatenate([jnp.where(d_row < DIFF_HEAD_DIM, qt, 0.0),
                              jnp.where(d_row >= DIFF_HEAD_DIM, qt, 0.0)], axis=1).astype(BF16)
    for side in range(3):
        qx_ref[side, 0:hd, :] = q_main

    m_ref[...] = jnp.full(m_ref.shape, -jnp.inf, F32)
    acc_ref[...] = jnp.zeros(acc_ref.shape, F32)

    n_col = 2 * tq // MXU_COLS

    def scores_tile(c, j, side, s_ref, ml_ref, bias=None):
        cs = slice(c * MXU_COLS, (c + 1) * MXU_COLS)
        rows = pl.ds(pl.multiple_of(j * tk, tk), tk)
        kcx = jnp.concatenate([k_ref[rows, :], kx_ref[...]], axis=1)
        s = jnp.dot(kcx, qx_ref[side, :, cs], preferred_element_type=F32)
        if bias is not None:
            b0 = (c * MXU_COLS) % tq
            s = s + bias[:, b0:b0 + MXU_COLS]
        s_ref[:, cs] = s
        ml_ref[:, cs] = jnp.max(s, axis=0, keepdims=True)

    def accumulate_tile(c, j, cst, s_ref, ml_ref):
        cs = slice(c * MXU_COLS, (c + 1) * MXU_COLS)
        m_prev = m_ref[:, cs]
        m_new = jnp.maximum(m_prev, ml_ref[:, cs] + cst)
        alpha = jnp.exp2(m_prev - m_new)
        p = jnp.exp2(s_ref[:, cs] - (m_new - cst)).astype(BF16)
        keys = pl.ds(pl.multiple_of(j * tk, tk), tk)
        acc_ref[:, cs] = alpha * acc_ref[:, cs] + jnp.dot(vt_ref[:, keys], p,
                                                          preferred_element_type=F32)
        m_ref[:, cs] = m_new

    def run(score_units, acc_units):
        assert len(score_units) == len(acc_units)
        for su, au in zip(score_units, acc_units):
            if su is not None:
                scores_tile(*su)
            if au is not None:
                accumulate_tile(*au)

    def tiles(desc, lo, hi):
        return [(c,) + desc for c in range(lo, hi)]

    def chunk(t):
        side = (t >= jd).astype(jnp.int32)
        j = t + side
        sign = (1 - 2 * side).astype(F32)
        cst = -sign * slope2 * (q0 - j * tk).astype(F32)
        return j, side, cst

    lag = min(ATTN_LAG, n_col)
    diag_bias = -slope2 * jnp.abs(rel_ref[...] + (jd * tk - q0).astype(F32))
    diag_s = (jd, 2, s0_ref, ml0_ref, diag_bias)
    diag_a = (jd, 0.0, s0_ref, ml0_ref)
    if n > 1:
        j0, side0, cst0 = chunk(jnp.int32(0))
        run(tiles(diag_s, 0, n_col) + tiles((j0, side0, s1_ref, ml1_ref, None), 0, n_col),
            [None] * lag + tiles(diag_a, 0, n_col)
            + tiles((j0, cst0, s1_ref, ml1_ref), 0, n_col - lag))

        def pair(i, carry):
            ja, _, ca = chunk(2 * i)
            jb, sb, cb = chunk(2 * i + 1)
            jc, sc, cc = chunk(2 * i + 2)
            run(tiles((jb, sb, s0_ref, ml0_ref, None), 0, n_col)
                + tiles((jc, sc, s1_ref, ml1_ref, None), 0, n_col),
                tiles((ja, ca, s1_ref, ml1_ref), n_col - lag, n_col)
                + tiles((jb, cb, s0_ref, ml0_ref), 0, n_col)
                + tiles((jc, cc, s1_ref, ml1_ref), 0, n_col - lag))
            return carry

        lax.fori_loop(0, (n - 2) // 2, pair, 0)
        jl, _, cl = chunk(jnp.int32(n - 2))
        run([None] * lag, tiles((jl, cl, s1_ref, ml1_ref), n_col - lag, n_col))
    else:
        run(tiles(diag_s, 0, n_col), [None] * lag + tiles(diag_a, 0, n_col - lag))
        run([None] * lag, tiles(diag_a, n_col - lag, n_col))

    lam = (jnp.exp(jnp.sum(lq1_ref[...] * lk1_ref[...], axis=-1, keepdims=True))
           - jnp.exp(jnp.sum(lq2_ref[...] * lk2_ref[...], axis=-1, keepdims=True))
           + lam_init)
    acc = acc_ref[...]
    out = acc[0:hd, :] * (1.0 / acc[hd:hd + 1, :])
    o = out[:, 0:tq] - lam * out[:, tq:2 * tq]
    o = o * lax.rsqrt(jnp.mean(o * o, axis=0, keepdims=True) + RMS_EPS)
    o = o * dgt_ref[...] * (1.0 - lam_init)
    o_ref[...] = o.T.astype(BF16)


def _attention(z, slopes, lq1, lk1, lq2, lk2, dg, batch, seq, tq, tk, lam_init):
    T = z.shape[0]
    nq = seq // tq
    hd = DIFF_V_DIM
    lam_spec = pl.BlockSpec((1, DIFF_HEAD_DIM), lambda b, h, i, s: (0, 0))
    grid_spec = pltpu.PrefetchScalarGridSpec(
        num_scalar_prefetch=1,
        grid=(batch, DIFF_HEADS, nq),
        in_specs=[
            lam_spec, lam_spec, lam_spec, lam_spec,
            pl.BlockSpec((hd, 1), lambda b, h, i, s: (0, 0)),
            pl.BlockSpec((tq, hd), lambda b, h, i, s: (b * nq + i, Q_COL128 + h)),
            pl.BlockSpec((seq, hd), lambda b, h, i, s: (b, K_COL128 + h)),
            pl.BlockSpec((seq, hd), lambda b, h, i, s: (b, V_COL128 + h)),
        ],
        out_specs=pl.BlockSpec((tq, hd), lambda b, h, i, s: (b * nq + i, h)),
        scratch_shapes=[
            pltpu.VMEM((3, 2 * hd, 2 * tq), BF16),
            pltpu.VMEM((tk, hd), BF16),
            pltpu.VMEM((hd + VT_PAD, seq), BF16),
            pltpu.VMEM((tk, tq), F32),
            pltpu.VMEM((tk, 2 * tq), F32),
            pltpu.VMEM((tk, 2 * tq), F32),
            pltpu.VMEM((1, 2 * tq), F32),
            pltpu.VMEM((1, 2 * tq), F32),
            pltpu.VMEM((1, 2 * tq), F32),
            pltpu.VMEM((hd + VT_PAD, 2 * tq), F32),
        ],
    )
    return pl.pallas_call(
        functools.partial(_attn_kernel, tq=tq, tk=tk, seq=seq, lam_init=lam_init),
        grid_spec=grid_spec,
        out_shape=jax.ShapeDtypeStruct((T, DIFF_WIDTH), BF16),
        compiler_params=_params(("parallel", "parallel", "arbitrary")),
        name="diff_attn",
    )(slopes, lq1, lk1, lq2, lk2, dg, z, z, z)


def _merge_kernel(x_ref, a_ref, b_ref, g00_ref, g01_ref, g10_ref, g11_ref, bg_ref,
                  wa_ref, wb_ref, wo_ref, o_ref, merged_ref):
    gate_refs = ((g00_ref, g01_ref), (g10_ref, g11_ref))
    half = D_MODEL // 2
    a = a_ref[...]
    b = b_ref[...]
    for c in range(2):
        cols = slice(c * half, (c + 1) * half)
        ga = jax.nn.sigmoid(gate_refs[0][c][...].astype(F32) + bg_ref[0:1, cols])
        gb = jax.nn.sigmoid(gate_refs[1][c][...].astype(F32) + bg_ref[1:2, cols])
        pa = jnp.dot(a, wa_ref[:, cols], preferred_element_type=F32)
        pb = jnp.dot(b, wb_ref[:, cols], preferred_element_type=F32)
        merged_ref[:, cols] = (ga * pa + gb * pb).astype(BF16)
    o_ref[...] = x_ref[...] + jnp.dot(merged_ref[...], wo_ref[...], preferred_element_type=F32)


def _merge(x2, a, b, z, b_gate, wa, wb, wo, layer, tm):
    T, D = x2.shape
    half = D // 2

    def gate_spec(k):
        return pl.BlockSpec((tm, half), lambda i: (i, GATE_COL1024 + k))

    def const_spec(shape):
        return pl.BlockSpec((None,) + shape, lambda i: (layer, 0, 0),
                            pipeline_mode=pl.Buffered(1))

    return pl.pallas_call(
        _merge_kernel,
        grid=(T // tm,),
        in_specs=[
            pl.BlockSpec((tm, D), lambda i: (i, 0)),
            pl.BlockSpec((tm, SGU_WIDTH), lambda i: (i, 0)),
            pl.BlockSpec((tm, DIFF_WIDTH), lambda i: (i, 0)),
            gate_spec(0), gate_spec(1), gate_spec(2), gate_spec(3),
            const_spec((N_BRANCHES, D)),
            const_spec((SGU_WIDTH, D)),
            const_spec((DIFF_WIDTH, D)),
            const_spec((D, D)),
        ],
        out_specs=pl.BlockSpec((tm, D), lambda i: (i, 0)),
        out_shape=jax.ShapeDtypeStruct((T, D), F32),
        scratch_shapes=[pltpu.VMEM((tm, D), BF16)],
        compiler_params=_params(("parallel",)),
        name="merge_out",
    )(x2, a, b, z, z, z, z, b_gate, wa, wb, wo)


def _split_bf16(x):
    hi = x.astype(BF16)
    lo = (x - hi.astype(F32)).astype(BF16)
    return hi, lo


def _dot3(x_hi, x_lo, w_hi, w_lo):
    return (jnp.dot(x_hi, w_hi, preferred_element_type=F32)
            + jnp.dot(x_lo, w_hi, preferred_element_type=F32)
            + jnp.dot(x_hi, w_lo, preferred_element_type=F32))


def _router(h, rw_hi_ref, rw_lo_ref, rb_ref):
    tm = h.shape[0]
    h_hi, h_lo = _split_bf16(h)
    logits = _dot3(h_hi, h_lo, rw_hi_ref[...], rw_lo_ref[...]) + rb_ref[...]
    lane = lax.broadcasted_iota(jnp.int32, (tm, ROUTER_COLS), 1)
    neg = jnp.float32(-jnp.inf)
    is_group = lane < N_GROUPS
    gl = jnp.where(is_group, logits, neg)
    gmax = jnp.max(gl, axis=-1, keepdims=True)
    gexp = jnp.exp(gl - gmax)
    g_w = 1.0 / jnp.sum(gexp, axis=-1, keepdims=True)
    g_idx = jnp.min(jnp.where(gl == gmax, lane, ROUTER_COLS), axis=-1, keepdims=True)
    e_lo = N_GROUPS + g_idx * EXPERTS_PER_GROUP
    in_group = (lane >= e_lo) & (lane < e_lo + EXPERTS_PER_GROUP)
    el = jnp.where(in_group, logits, neg)
    m1 = jnp.max(el, axis=-1, keepdims=True)
    i1 = jnp.min(jnp.where(el == m1, lane, ROUTER_COLS), axis=-1, keepdims=True)
    el2 = jnp.where(lane == i1, neg, el)
    m2 = jnp.max(el2, axis=-1, keepdims=True)
    i2 = jnp.min(jnp.where(el2 == m2, lane, ROUTER_COLS), axis=-1, keepdims=True)
    t = jnp.exp(m2 - m1)
    w1 = g_w / (1.0 + t)
    w2 = g_w * t / (1.0 + t)
    comb = jnp.where(lane == i1, w1, 0.0) + jnp.where(lane == i2, w2, 0.0)
    return comb, g_idx


META_COLS = LANES


def _route_kernel(x_ref, g_ref, rw_hi_ref, rw_lo_ref, rb_ref, xr_ref, meta_ref, cnt_ref,
                  carry_ref, tri_ref):
    tm, D = x_ref.shape

    @pl.when(pl.program_id(0) == 0)
    def _():
        carry_ref[...] = jnp.zeros(carry_ref.shape, F32)
        r = lax.broadcasted_iota(jnp.int32, (tm, tm), 0)
        c = lax.broadcasted_iota(jnp.int32, (tm, tm), 1)
        tri_ref[...] = jnp.where(c < r, 1.0, 0.0).astype(BF16)

    x = x_ref[...]
    h = x * _rms_scale(x) * g_ref[...]
    comb, g_idx = _router(h, rw_hi_ref, rw_lo_ref, rb_ref)
    xr_ref[:, 0:D] = x
    xr_ref[:, D:D + ROUTER_COLS] = comb

    lane = lax.broadcasted_iota(jnp.int32, (tm, META_COLS), 1)
    onehot = jnp.where(lane == g_idx, 1.0, 0.0)
    earlier = jnp.dot(tri_ref[...], onehot.astype(BF16), preferred_element_type=F32)
    carry = carry_ref[...]
    rank = jnp.sum((earlier + carry) * onehot, axis=-1, keepdims=True).astype(jnp.int32)
    meta_ref[...] = jnp.where(lane == 0, g_idx, jnp.where(lane == 1, rank, 0))
    carry = carry + jnp.sum(onehot, axis=0, keepdims=True)
    carry_ref[...] = carry
    cnt_ref[...] = carry.astype(jnp.int32)


def _route(x2, g, rw_hi, rw_lo, rb, tm):
    T, D = x2.shape
    row = pl.BlockSpec((1, D), lambda i: (0, 0))
    rspec = pl.BlockSpec((D, ROUTER_COLS), lambda i: (0, 0))
    return pl.pallas_call(
        _route_kernel,
        grid=(T // tm,),
        in_specs=[pl.BlockSpec((tm, D), lambda i: (i, 0)), row, rspec, rspec,
                  pl.BlockSpec((1, ROUTER_COLS), lambda i: (0, 0))],
        out_specs=[pl.BlockSpec((tm, D + ROUTER_COLS), lambda i: (i, 0)),
                   pl.BlockSpec((tm, META_COLS), lambda i: (i, 0)),
                   pl.BlockSpec((1, META_COLS), lambda i: (0, 0))],
        out_shape=[jax.ShapeDtypeStruct((T, D + ROUTER_COLS), F32),
                   jax.ShapeDtypeStruct((T, META_COLS), jnp.int32),
                   jax.ShapeDtypeStruct((1, META_COLS), jnp.int32)],
        scratch_shapes=[pltpu.VMEM((1, META_COLS), F32), pltpu.VMEM((tm, tm), BF16)],
        compiler_params=_params(("arbitrary",)),
        name="moe_route",
    )(x2, g, rw_hi, rw_lo, rb)


def _invert_kernel(pos_ref, tok_ref):
    def put(t, carry):
        tok_ref[pos_ref[t]] = t
        return carry
    lax.fori_loop(0, pos_ref.shape[0], put, 0, unroll=16)


def _invert(pos):
    return pl.pallas_call(
        _invert_kernel,
        in_specs=[pl.BlockSpec(memory_space=pltpu.SMEM)],
        out_specs=pl.BlockSpec(memory_space=pltpu.SMEM),
        out_shape=jax.ShapeDtypeStruct(pos.shape, jnp.int32),
        name="moe_invert",
    )(pos)


ITEM_FIRST, ITEM_LAST, ITEM_LIVE, ITEM_FINAL = 1, 2, 4, 8
GATHER_STEPS = EXPERTS_PER_GROUP // 2


def _experts_kernel(tok_ref, itile_ref, igroup_ref, iflags_ref, xr_hbm, g_ref, w1_ref, w3_ref,
                    w2_ref, fg_ref, out_hbm, xbuf, obuf, h_ref, comb_ref, gsem, ssem,
                    *, rows, final_norm):
    w = pl.program_id(0)
    e = pl.program_id(1)
    slot = w % 2
    tile = itile_ref[w]
    osl = tile % 2
    flags = iflags_ref[w]
    live = (flags & ITEM_LIVE) != 0
    D = out_hbm.shape[1]
    part = rows // GATHER_STEPS

    def gather_copy(t, r, sl):
        tok = tok_ref[t * rows + r]
        return pltpu.make_async_copy(xr_hbm.at[pl.ds(tok, 1), :],
                                     xbuf.at[sl, pl.ds(r, 1), :], gsem.at[sl])

    def scatter_copy(t, r, sl):
        tok = tok_ref[t * rows + r]
        return pltpu.make_async_copy(obuf.at[sl, pl.ds(r, 1), :],
                                     out_hbm.at[pl.ds(tok, 1), :], ssem.at[sl])

    def for_rows(fn, n=rows):
        def body(r, carry):
            fn(r)
            return carry
        lax.fori_loop(0, n, body, 0, unroll=8)

    @pl.when(e == 0)
    def _():
        @pl.when(w == 0)
        def _():
            for_rows(lambda r: gather_copy(tile, r, slot).start())
        for_rows(lambda r: gather_copy(tile, r, slot).wait())

        @pl.when((flags & ITEM_FIRST) != 0)
        def _():
            @pl.when(tile >= 2)
            def _():
                for_rows(lambda r: scatter_copy(tile - 2, r, osl).wait())
            x = xbuf[slot, :, 0:D]
            h_ref[...] = (x * _rms_scale(x) * g_ref[...]).astype(BF16)
            comb_ref[...] = xbuf[slot, :, D:D + ROUTER_COLS]
            obuf[osl] = x

    next_tile = itile_ref[w + 1]

    send_prev = jnp.logical_and((flags & ITEM_FIRST) != 0, tile >= 1)

    def expert_step(dma):
        if dma == "gather":
            for r in range(part):
                gather_copy(next_tile, e * part + r, 1 - slot).start()
        elif dma == "scatter":
            for r in range(part):
                scatter_copy(tile - 1, (e - GATHER_STEPS) * part + r, 1 - osl).start()
        h = h_ref[...]
        hid = jax.nn.silu(jnp.dot(h, w1_ref[0], preferred_element_type=F32)) \
            * jnp.dot(h, w3_ref[0], preferred_element_type=F32)
        lane = lax.broadcasted_iota(jnp.int32, (rows, ROUTER_COLS), 1)
        col = N_GROUPS + igroup_ref[w] * EXPERTS_PER_GROUP + e
        c_e = jnp.sum(jnp.where(lane == col, comb_ref[...], 0.0), axis=-1, keepdims=True)
        obuf[osl] += c_e * jnp.dot(hid.astype(BF16), w2_ref[0], preferred_element_type=F32)

    early = e < GATHER_STEPS

    @pl.when(jnp.logical_and(live, early))
    def _():
        expert_step("gather")

    @pl.when(jnp.logical_and(live, jnp.logical_and(jnp.logical_not(early), send_prev)))
    def _():
        expert_step("scatter")

    @pl.when(jnp.logical_and(live, jnp.logical_and(jnp.logical_not(early),
                                                   jnp.logical_not(send_prev))))
    def _():
        expert_step("plain")

    @pl.when(jnp.logical_and(jnp.logical_not(live), early))
    def _():
        for_rows(lambda r: gather_copy(next_tile, e * part + r, 1 - slot).start(), part)

    @pl.when(jnp.logical_and(e == EXPERTS_PER_GROUP - 1, (flags & ITEM_LAST) != 0))
    def _():
        if final_norm:
            y = obuf[osl]
            obuf[osl] = y * _rms_scale(y) * fg_ref[...]

        @pl.when((flags & ITEM_FINAL) != 0)
        def _():
            for_rows(lambda r: scatter_copy(tile, r, osl).start())

            @pl.when(tile >= 1)
            def _():
                for_rows(lambda r: scatter_copy(tile - 1, r, 1 - osl).wait())
            for_rows(lambda r: scatter_copy(tile, r, osl).wait())

    @pl.when(jnp.logical_and(e == EXPERTS_PER_GROUP - 1, w == pl.num_programs(0) - 1))
    def _():
        for_rows(lambda r: gather_copy(next_tile, r, 1 - slot).wait())


def _experts(xr, toks, item_tile, item_group, item_flags, g, w1, w3, w2, final_g, layer, rows,
             final_norm):
    T = xr.shape[0]
    D = D_MODEL
    max_items = item_group.shape[0]

    def expert_block(w, e, tok, itile, igroup, iflags):
        return (layer * N_EXPERTS + igroup[w] * EXPERTS_PER_GROUP + e, 0, 0)

    row = pl.BlockSpec((1, D), lambda w, e, tok, itile, igroup, iflags: (0, 0))
    grid_spec = pltpu.PrefetchScalarGridSpec(
        num_scalar_prefetch=4,
        grid=(max_items, EXPERTS_PER_GROUP),
        in_specs=[
            pl.BlockSpec(memory_space=pl.ANY),
            row,
            pl.BlockSpec((1, D, EXPERT_FF), expert_block),
            pl.BlockSpec((1, D, EXPERT_FF), expert_block),
            pl.BlockSpec((1, EXPERT_FF, D), expert_block),
            row,
        ],
        out_specs=pl.BlockSpec(memory_space=pl.ANY),
        scratch_shapes=[
            pltpu.VMEM((2, rows, D + ROUTER_COLS), F32),
            pltpu.VMEM((2, rows, D), F32),
            pltpu.VMEM((rows, D), BF16),
            pltpu.VMEM((rows, ROUTER_COLS), F32),
            pltpu.SemaphoreType.DMA((2,)),
            pltpu.SemaphoreType.DMA((2,)),
        ],
    )
    return pl.pallas_call(
        functools.partial(_experts_kernel, rows=rows, final_norm=final_norm),
        grid_spec=grid_spec,
        out_shape=jax.ShapeDtypeStruct((T, D), F32),
        compiler_params=_params(("arbitrary", "arbitrary")),
        name="moe_experts",
    )(toks, item_tile, item_group, item_flags, xr, g, w1, w3, w2, final_g)


def _work_items(cnt, rows, n_tiles):
    i32 = jnp.int32
    max_items = n_tiles + N_GROUPS - 1
    ends = jnp.cumsum(cnt)
    starts = ends - cnt
    first_tile = starts // rows
    n_g = jnp.where(cnt > 0, (ends - 1) // rows - first_tile + 1, 0)
    item_end = jnp.cumsum(n_g)
    n_items = item_end[-1]
    w = jnp.arange(max_items + 1, dtype=i32)
    grp = jnp.minimum(jnp.sum(w[:, None] >= item_end[None, :], axis=1), N_GROUPS - 1).astype(i32)
    tile = first_tile[grp] + (w - (item_end - n_g)[grp])
    live = w < n_items
    tile = jnp.where(live, tile, n_tiles - 1).astype(i32)
    prev_tile = jnp.concatenate([jnp.full((1,), -1, i32), tile[:-1]])
    next_tile = jnp.concatenate([tile[1:], jnp.full((1,), -1, i32)])
    final = w == n_items - 1
    first = live & (tile != prev_tile)
    last = live & ((tile != next_tile) | final)
    flags = (first * ITEM_FIRST + last * ITEM_LAST + live * ITEM_LIVE
             + final * ITEM_FINAL).astype(i32)
    return starts, tile, grp[:max_items], flags[:max_items]


def _moe(x2, g, rw_hi, rw_lo, rb, w1, w3, w2, final_g, layer, rows, final_norm):
    T = x2.shape[0]
    xr, meta, counts = _route(x2, g, rw_hi, rw_lo, rb, rows)
    starts, item_tile, item_group, item_flags = _work_items(counts[0, :N_GROUPS], rows, T // rows)
    pos = (starts[meta[:, 0]] + meta[:, 1]).astype(jnp.int32)
    toks = _invert(pos)
    return _experts(xr, toks, item_tile, item_group, item_flags, g, w1, w3, w2, final_g, layer,
                    rows, final_norm)


def _router_weights(rg_w, rg_b, re_w, re_b):
    D = rg_w.shape[0]
    w = jnp.zeros((D, ROUTER_COLS), F32)
    w = w.at[:, :N_GROUPS].set(rg_w).at[:, N_GROUPS:N_GROUPS + N_EXPERTS].set(re_w)
    b = jnp.zeros((1, ROUTER_COLS), F32)
    b = b.at[0, :N_GROUPS].set(rg_b).at[0, N_GROUPS:N_GROUPS + N_EXPERTS].set(re_b)
    hi = w.astype(BF16)
    lo = (w - hi.astype(F32)).astype(BF16)
    return hi, lo, b


class _Tiles(NamedTuple):
    inproj_rows: int
    inproj_cols: int
    sgu_rows: int
    attn_q: int
    attn_k: int
    merge_rows: int
    moe_rows: int


def _tile_plan(n_tokens, seq):
    return _Tiles(inproj_rows=min(1024, n_tokens), inproj_cols=2304,
                  sgu_rows=min(512, n_tokens), attn_q=min(1024, seq), attn_k=min(1024, seq),
                  merge_rows=min(256, n_tokens), moe_rows=min(512, n_tokens))


def kernel(x, norm1_g, w_in, b_gate, sgu_ln_g, sgu_ln_b, sgu_w, sgu_b, lam_q1, lam_k1, lam_q2,
           lam_k2, diff_norm_g, w_proj_a, w_proj_b, w_out, norm2_g, router_g_w, router_g_b,
           router_e_w, router_e_b, w1, w3, w2, final_g):
    B, S, D = x.shape
    assert D == D_MODEL and w_in.shape[2] == IN_COLS
    depth = w_in.shape[0]
    T = B * S
    slopes = jnp.exp2(-8.0 * jnp.arange(1, DIFF_HEADS + 1, dtype=F32) / DIFF_HEADS)
    tiles = _tile_plan(T, S)

    w_in_b = w_in.astype(BF16)
    wa_b, wb_b, wo_b = w_proj_a.astype(BF16), w_proj_b.astype(BF16), w_out.astype(BF16)
    w1_b = w1.astype(BF16).reshape(depth * N_EXPERTS, D, EXPERT_FF)
    w3_b = w3.astype(BF16).reshape(depth * N_EXPERTS, D, EXPERT_FF)
    w2_b = w2.astype(BF16).reshape(depth * N_EXPERTS, EXPERT_FF, D)

    x2 = x.reshape(T, D)
    for l in range(depth):
        lam_init = 0.8 - 0.6 * math.exp(-0.3 * l)
        z = _inproj(x2, norm1_g[l][None], w_in_b, l, tiles.inproj_rows, tiles.inproj_cols)
        a = _sgu(z, sgu_ln_g[l][None], sgu_ln_b[l][None], sgu_w[l].astype(BF16),
                 sgu_b[l][:, :, None], tiles.sgu_rows)
        o = _attention(z, slopes, lam_q1[l][None], lam_k1[l][None], lam_q2[l][None],
                       lam_k2[l][None], diff_norm_g[l][:, None], B, S, tiles.attn_q,
                       tiles.attn_k, lam_init)
        x2 = _merge(x2, a, o, z, b_gate, wa_b, wb_b, wo_b, l, tiles.merge_rows)
        rw_hi, rw_lo, rb = _router_weights(router_g_w[l], router_g_b[l], router_e_w[l],
                                           router_e_b[l])
        x2 = _moe(x2, norm2_g[l][None], rw_hi, rw_lo, rb, w1_b, w3_b, w2_b, final_g[None], l,
                  tiles.moe_rows, final_norm=(l == depth - 1))
    return x2.reshape(B, S, D)
```

```python
import functools
import math
from typing import NamedTuple

import jax
import jax.numpy as jnp
from jax import lax
from jax.experimental import pallas as pl
from jax.experimental.pallas import tpu as pltpu

F32 = jnp.float32
BF16 = jnp.bfloat16

D_MODEL = 2048
SGU_GROUPS = 8
SGU_WIDTH = 1024
SGU_GROUP_DIM = SGU_WIDTH // SGU_GROUPS
CHUNK = 128
DIFF_HEADS = 8
DIFF_HEAD_DIM = 64
DIFF_V_DIM = 2 * DIFF_HEAD_DIM
DIFF_QK_WIDTH = DIFF_HEADS * 2 * DIFF_HEAD_DIM
DIFF_WIDTH = DIFF_HEADS * DIFF_V_DIM
N_BRANCHES = 2
IN_COLS = 2 * SGU_WIDTH + 2 * DIFF_QK_WIDTH + DIFF_WIDTH + N_BRANCHES * D_MODEL
N_GROUPS = 4
EXPERTS_PER_GROUP = 4
N_EXPERTS = N_GROUPS * EXPERTS_PER_GROUP
EXPERT_FF = 512
RMS_EPS = 1e-6
LN_EPS = 1e-5

Q_COL128 = (2 * SGU_WIDTH) // 128
K_COL128 = (2 * SGU_WIDTH + DIFF_QK_WIDTH) // 128
V_COL128 = (2 * SGU_WIDTH + 2 * DIFF_QK_WIDTH) // 128
GATE_COL1024 = (2 * SGU_WIDTH + 2 * DIFF_QK_WIDTH + DIFF_WIDTH) // 1024

LANES = 128
MXU_COLS = 256
ROUTER_COLS = LANES
VMEM_LIMIT = 56 * 1024 * 1024


def _params(semantics):
    return pltpu.CompilerParams(dimension_semantics=semantics, vmem_limit_bytes=VMEM_LIMIT)


def _gelu(x):
    return 0.5 * x * (1.0 + jnp.tanh(0.7978845608028654 * (x + 0.044715 * (x * x * x))))


def _rms_scale(x):
    return lax.rsqrt(jnp.mean(x * x, axis=-1, keepdims=True) + RMS_EPS)


def _inproj_kernel(x_ref, g_ref, w_ref, z_ref, h_ref):
    @pl.when(pl.program_id(1) == 0)
    def _():
        x = x_ref[...]
        h_ref[...] = (x * _rms_scale(x) * g_ref[...]).astype(BF16)

    z_ref[...] = jnp.dot(h_ref[...], w_ref[...], preferred_element_type=F32).astype(BF16)


def _inproj(x2, g, w_bf16, layer, tm, tn):
    T, D = x2.shape
    N = w_bf16.shape[2]
    return pl.pallas_call(
        _inproj_kernel,
        grid=(T // tm, N // tn),
        in_specs=[
            pl.BlockSpec((tm, D), lambda i, j: (i, 0)),
            pl.BlockSpec((1, D), lambda i, j: (0, 0)),
            pl.BlockSpec((None, D, tn), lambda i, j: (layer, 0, j)),
        ],
        out_specs=pl.BlockSpec((tm, tn), lambda i, j: (i, j)),
        out_shape=jax.ShapeDtypeStruct((T, N), BF16),
        scratch_shapes=[pltpu.VMEM((tm, D), BF16)],
        compiler_params=_params(("parallel", "arbitrary")),
        name="inproj",
    )(x2, g, w_bf16)


def _sgu_kernel(u_ref, v_ref, lng_ref, lnb_ref, ws_ref, bs_ref, a_ref, *, chunks):
    v = _gelu(v_ref[...].astype(F32))
    mu = jnp.mean(v, axis=-1, keepdims=True)
    vc = v - mu
    var = jnp.mean(vc * vc, axis=-1, keepdims=True)
    vn = (vc * lax.rsqrt(var + LN_EPS) * lng_ref[...] + lnb_ref[...]).astype(BF16)
    for c in range(chunks):
        rows = slice(c * CHUNK, (c + 1) * CHUNK)
        for g in range(SGU_GROUPS):
            cols = slice(g * SGU_GROUP_DIM, (g + 1) * SGU_GROUP_DIM)
            mixed = jnp.dot(ws_ref[g], vn[rows, cols], preferred_element_type=F32)
            mixed = mixed + bs_ref[g]
            u = _gelu(u_ref[rows, cols].astype(F32))
            a_ref[rows, cols] = (u * mixed).astype(BF16)


def _sgu(z, ln_g, ln_b, ws_bf16, bs_col, tm):
    T = z.shape[0]
    wblk = SGU_WIDTH
    return pl.pallas_call(
        functools.partial(_sgu_kernel, chunks=tm // CHUNK),
        grid=(T // tm,),
        in_specs=[
            pl.BlockSpec((tm, wblk), lambda i: (i, 0)),
            pl.BlockSpec((tm, wblk), lambda i: (i, 1)),
            pl.BlockSpec((1, wblk), lambda i: (0, 0)),
            pl.BlockSpec((1, wblk), lambda i: (0, 0)),
            pl.BlockSpec((SGU_GROUPS, CHUNK, CHUNK), lambda i: (0, 0, 0)),
            pl.BlockSpec((SGU_GROUPS, CHUNK, 1), lambda i: (0, 0, 0)),
        ],
        out_specs=pl.BlockSpec((tm, wblk), lambda i: (i, 0)),
        out_shape=jax.ShapeDtypeStruct((T, SGU_WIDTH), BF16),
        compiler_params=_params(("parallel",)),
        name="sgu",
    )(z, z, ln_g, ln_b, ws_bf16, bs_col)


LOG2E = 1.4426950408889634
VT_PAD = 16
ATTN_LAG = 3


def _split3(x):
    a1 = x.astype(BF16).astype(F32)
    a2 = (x - a1).astype(BF16).astype(F32)
    a3 = (x - a1 - a2).astype(BF16).astype(F32)
    return a1, a2, a3


def _pick(row, values):
    out = jnp.zeros(values[0].shape, F32)
    for i, v in enumerate(values):
        out = jnp.where(row == i, v, out)
    return out


def _attn_kernel(slopes_ref, lq1_ref, lk1_ref, lq2_ref, lk2_ref, dgt_ref, q_ref, k_ref, v_ref,
                 o_ref, qx_ref, kx_ref, vt_ref, rel_ref, s0_ref, s1_ref, ml0_ref, ml1_ref,
                 m_ref, acc_ref, *, tq, tk, seq, lam_init):
    h = pl.program_id(1)
    qi = pl.program_id(2)
    slope2 = slopes_ref[h] * LOG2E
    hd = DIFF_V_DIM
    n = seq // tk
    q0 = qi * tq
    jd = q0 // tk

    @pl.when(qi == 0)
    def _():
        pad_row = lax.broadcasted_iota(jnp.int32, (VT_PAD, tk), 0)
        ones_pad = jnp.where(pad_row == 0, 1.0, 0.0).astype(BF16)

        def transpose_chunk(c, carry):
            sl = pl.ds(pl.multiple_of(c * tk, tk), tk)
            vt_ref[0:hd, sl] = v_ref[sl, :].astype(F32).T.astype(BF16)
            vt_ref[hd:hd + VT_PAD, sl] = ones_pad
            return carry
        lax.fori_loop(0, n, transpose_chunk, 0)

        c_idx = lax.broadcasted_iota(jnp.int32, (tk, hd), 0)
        k_lane = lax.broadcasted_iota(jnp.int32, (tk, hd), 1)
        a = _split3(jnp.full((tk, hd), slope2, F32))
        c_lo = (c_idx & (LANES - 1)).astype(F32)
        c_hi = (c_idx >> 7).astype(F32)
        kx = _pick(k_lane, [-a[0], -a[1], -a[2], -LANES * a[0], -LANES * a[1], -LANES * a[2],
                            c_lo, c_lo, c_lo, c_hi, c_hi, c_hi])
        kx_ref[...] = kx.astype(BF16)

        x_row = lax.broadcasted_iota(jnp.int32, (hd, 2 * tq), 0)
        r_idx = lax.broadcasted_iota(jnp.int32, (hd, 2 * tq), 1)
        r_idx = jnp.where(r_idx >= tq, r_idx - tq, r_idx)
        r_lo = (r_idx & (LANES - 1)).astype(F32)
        r_hi = (r_idx >> 7).astype(F32)
        a = _split3(jnp.full((hd, 2 * tq), slope2, F32))
        ext = _pick(x_row, [r_lo, r_lo, r_lo, r_hi, r_hi, r_hi,
                            a[0], a[1], a[2], LANES * a[0], LANES * a[1], LANES * a[2]])
        for side, sign in ((0, 1.0), (1, -1.0), (2, 0.0)):
            qx_ref[side, hd:2 * hd, :] = (sign * ext).astype(BF16)

        rel_ref[...] = (lax.broadcasted_iota(jnp.int32, (tk, tq), 0)
                        - lax.broadcasted_iota(jnp.int32, (tk, tq), 1)).astype(F32)

    qt = (q_ref[...].astype(F32) * (DIFF_HEAD_DIM ** -0.5 * LOG2E)).T
    d_row = lax.broadcasted_iota(jnp.int32, (hd, tq), 0)
    q_main = jnp.concatenate([jnp.where(d_row < DIFF_HEAD_DIM, qt, 0.0),
                              jnp.where(d_row >= DIFF_HEAD_DIM, qt, 0.0)], axis=1).astype(BF16)
    for side in range(3):
        qx_ref[side, 0:hd, :] = q_main

    m_ref[...] = jnp.full(m_ref.shape, -jnp.inf, F32)
    acc_ref[...] = jnp.zeros(acc_ref.shape, F32)

    n_col = 2 * tq // MXU_COLS

    def scores_tile(c, j, side, s_ref, ml_ref, bias=None):
        cs = slice(c * MXU_COLS, (c + 1) * MXU_COLS)
        rows = pl.ds(pl.multiple_of(j * tk, tk), tk)
        kcx = jnp.concatenate([k_ref[rows, :], kx_ref[...]], axis=1)
        s = jnp.dot(kcx, qx_ref[side, :, cs], preferred_element_type=F32)
        if bias is not None:
            b0 = (c * MXU_COLS) % tq
            s = s + bias[:, b0:b0 + MXU_COLS]
        s_ref[c] = s
        ml_ref[:, cs] = jnp.max(s, axis=0, keepdims=True)

    def accumulate_tile(c, j, cst, s_ref, ml_ref):
        cs = slice(c * MXU_COLS, (c + 1) * MXU_COLS)
        m_prev = m_ref[:, cs]
        m_new = jnp.maximum(m_prev, ml_ref[:, cs] + cst)
        alpha = jnp.exp2(m_prev - m_new)
        p = jnp.exp2(s_ref[c] - (m_new - cst)).astype(BF16)
        keys = pl.ds(pl.multiple_of(j * tk, tk), tk)
        acc_ref[:, cs] = alpha * acc_ref[:, cs] + jnp.dot(vt_ref[:, keys], p,
                                                          preferred_element_type=F32)
        m_ref[:, cs] = m_new

    def run(score_units, acc_units):
        assert len(score_units) == len(acc_units)
        for su, au in zip(score_units, acc_units):
            if su is not None:
                scores_tile(*su)
            if au is not None:
                accumulate_tile(*au)

    def tiles(desc, lo, hi):
        return [(c,) + desc for c in range(lo, hi)]

    def chunk(t):
        side = (t >= jd).astype(jnp.int32)
        j = t + side
        sign = (1 - 2 * side).astype(F32)
        cst = -sign * slope2 * (q0 - j * tk).astype(F32)
        return j, side, cst

    lag = min(ATTN_LAG, n_col)
    diag_bias = -slope2 * jnp.abs(rel_ref[...] + (jd * tk - q0).astype(F32))
    diag_s = (jd, 2, s0_ref, ml0_ref, diag_bias)
    diag_a = (jd, 0.0, s0_ref, ml0_ref)
    if n > 1:
        j0, side0, cst0 = chunk(jnp.int32(0))
        run(tiles(diag_s, 0, n_col) + tiles((j0, side0, s1_ref, ml1_ref, None), 0, n_col),
            [None] * lag + tiles(diag_a, 0, n_col)
            + tiles((j0, cst0, s1_ref, ml1_ref), 0, n_col - lag))

        def pair(i, carry):
            ja, _, ca = chunk(2 * i)
            jb, sb, cb = chunk(2 * i + 1)
            jc, sc, cc = chunk(2 * i + 2)
            run(tiles((jb, sb, s0_ref, ml0_ref, None), 0, n_col)
                + tiles((jc, sc, s1_ref, ml1_ref, None), 0, n_col),
                tiles((ja, ca, s1_ref, ml1_ref), n_col - lag, n_col)
                + tiles((jb, cb, s0_ref, ml0_ref), 0, n_col)
                + tiles((jc, cc, s1_ref, ml1_ref), 0, n_col - lag))
            return carry

        lax.fori_loop(0, (n - 2) // 2, pair, 0)
        jl, _, cl = chunk(jnp.int32(n - 2))
        run([None] * lag, tiles((jl, cl, s1_ref, ml1_ref), n_col - lag, n_col))
    else:
        run(tiles(diag_s, 0, n_col), [None] * lag + tiles(diag_a, 0, n_col - lag))
        run([None] * lag, tiles(diag_a, n_col - lag, n_col))

    lam = (jnp.exp(jnp.sum(lq1_ref[...] * lk1_ref[...], axis=-1, keepdims=True))
           - jnp.exp(jnp.sum(lq2_ref[...] * lk2_ref[...], axis=-1, keepdims=True))
           + lam_init)
    acc = acc_ref[...]
    out = acc[0:hd, :] * (1.0 / acc[hd:hd + 1, :])
    o = out[:, 0:tq] - lam * out[:, tq:2 * tq]
    o = o * lax.rsqrt(jnp.mean(o * o, axis=0, keepdims=True) + RMS_EPS)
    o = o * dgt_ref[...] * (1.0 - lam_init)
    o_ref[...] = o.T.astype(BF16)


def _attention(z, slopes, lq1, lk1, lq2, lk2, dg, batch, seq, tq, tk, lam_init):
    T = z.shape[0]
    nq = seq // tq
    hd = DIFF_V_DIM
    lam_spec = pl.BlockSpec((1, DIFF_HEAD_DIM), lambda b, h, i, s: (0, 0))
    grid_spec = pltpu.PrefetchScalarGridSpec(
        num_scalar_prefetch=1,
        grid=(batch, DIFF_HEADS, nq),
        in_specs=[
            lam_spec, lam_spec, lam_spec, lam_spec,
            pl.BlockSpec((hd, 1), lambda b, h, i, s: (0, 0)),
            pl.BlockSpec((tq, hd), lambda b, h, i, s: (b * nq + i, Q_COL128 + h)),
            pl.BlockSpec((seq, hd), lambda b, h, i, s: (b, K_COL128 + h)),
            pl.BlockSpec((seq, hd), lambda b, h, i, s: (b, V_COL128 + h)),
        ],
        out_specs=pl.BlockSpec((tq, hd), lambda b, h, i, s: (b * nq + i, h)),
        scratch_shapes=[
            pltpu.VMEM((3, 2 * hd, 2 * tq), BF16),
            pltpu.VMEM((tk, hd), BF16),
            pltpu.VMEM((hd + VT_PAD, seq), BF16),
            pltpu.VMEM((tk, tq), F32),
            pltpu.VMEM((2 * tq // MXU_COLS, tk, MXU_COLS), F32),
            pltpu.VMEM((2 * tq // MXU_COLS, tk, MXU_COLS), F32),
            pltpu.VMEM((1, 2 * tq), F32),
            pltpu.VMEM((1, 2 * tq), F32),
            pltpu.VMEM((1, 2 * tq), F32),
            pltpu.VMEM((hd + VT_PAD, 2 * tq), F32),
        ],
    )
    return pl.pallas_call(
        functools.partial(_attn_kernel, tq=tq, tk=tk, seq=seq, lam_init=lam_init),
        grid_spec=grid_spec,
        out_shape=jax.ShapeDtypeStruct((T, DIFF_WIDTH), BF16),
        compiler_params=_params(("parallel", "parallel", "arbitrary")),
        name="diff_attn",
    )(slopes, lq1, lk1, lq2, lk2, dg, z, z, z)


def _merge_kernel(x_ref, a_ref, b_ref, g00_ref, g01_ref, g10_ref, g11_ref, bg_ref,
                  wa_ref, wb_ref, wo_ref, o_ref, merged_ref):
    gate_refs = ((g00_ref, g01_ref), (g10_ref, g11_ref))
    half = D_MODEL // 2
    a = a_ref[...]
    b = b_ref[...]
    for c in range(2):
        cols = slice(c * half, (c + 1) * half)
        ga = jax.nn.sigmoid(gate_refs[0][c][...].astype(F32) + bg_ref[0:1, cols])
        gb = jax.nn.sigmoid(gate_refs[1][c][...].astype(F32) + bg_ref[1:2, cols])
        pa = jnp.dot(a, wa_ref[:, cols], preferred_element_type=F32)
        pb = jnp.dot(b, wb_ref[:, cols], preferred_element_type=F32)
        merged_ref[:, cols] = (ga * pa + gb * pb).astype(BF16)
    o_ref[...] = x_ref[...] + jnp.dot(merged_ref[...], wo_ref[...], preferred_element_type=F32)


def _merge(x2, a, b, z, b_gate, wa, wb, wo, layer, tm):
    T, D = x2.shape
    half = D // 2

    def gate_spec(k):
        return pl.BlockSpec((tm, half), lambda i: (i, GATE_COL1024 + k))

    def const_spec(shape):
        return pl.BlockSpec((None,) + shape, lambda i: (layer, 0, 0),
                            pipeline_mode=pl.Buffered(1))

    return pl.pallas_call(
        _merge_kernel,
        grid=(T // tm,),
        in_specs=[
            pl.BlockSpec((tm, D), lambda i: (i, 0)),
            pl.BlockSpec((tm, SGU_WIDTH), lambda i: (i, 0)),
            pl.BlockSpec((tm, DIFF_WIDTH), lambda i: (i, 0)),
            gate_spec(0), gate_spec(1), gate_spec(2), gate_spec(3),
            const_spec((N_BRANCHES, D)),
            const_spec((SGU_WIDTH, D)),
            const_spec((DIFF_WIDTH, D)),
            const_spec((D, D)),
        ],
        out_specs=pl.BlockSpec((tm, D), lambda i: (i, 0)),
        out_shape=jax.ShapeDtypeStruct((T, D), F32),
        scratch_shapes=[pltpu.VMEM((tm, D), BF16)],
        compiler_params=_params(("parallel",)),
        name="merge_out",
    )(x2, a, b, z, z, z, z, b_gate, wa, wb, wo)


def _split_bf16(x):
    hi = x.astype(BF16)
    lo = (x - hi.astype(F32)).astype(BF16)
    return hi, lo


def _dot3(x_hi, x_lo, w_hi, w_lo):
    return (jnp.dot(x_hi, w_hi, preferred_element_type=F32)
            + jnp.dot(x_lo, w_hi, preferred_element_type=F32)
            + jnp.dot(x_hi, w_lo, preferred_element_type=F32))


def _router(h, rw_hi_ref, rw_lo_ref, rb_ref):
    tm = h.shape[0]
    h_hi, h_lo = _split_bf16(h)
    logits = _dot3(h_hi, h_lo, rw_hi_ref[...], rw_lo_ref[...]) + rb_ref[...]
    lane = lax.broadcasted_iota(jnp.int32, (tm, ROUTER_COLS), 1)
    neg = jnp.float32(-jnp.inf)
    is_group = lane < N_GROUPS
    gl = jnp.where(is_group, logits, neg)
    gmax = jnp.max(gl, axis=-1, keepdims=True)
    gexp = jnp.exp(gl - gmax)
    g_w = 1.0 / jnp.sum(gexp, axis=-1, keepdims=True)
    g_idx = jnp.min(jnp.where(gl == gmax, lane, ROUTER_COLS), axis=-1, keepdims=True)
    e_lo = N_GROUPS + g_idx * EXPERTS_PER_GROUP
    in_group = (lane >= e_lo) & (lane < e_lo + EXPERTS_PER_GROUP)
    el = jnp.where(in_group, logits, neg)
    m1 = jnp.max(el, axis=-1, keepdims=True)
    i1 = jnp.min(jnp.where(el == m1, lane, ROUTER_COLS), axis=-1, keepdims=True)
    el2 = jnp.where(lane == i1, neg, el)
    m2 = jnp.max(el2, axis=-1, keepdims=True)
    i2 = jnp.min(jnp.where(el2 == m2, lane, ROUTER_COLS), axis=-1, keepdims=True)
    t = jnp.exp(m2 - m1)
    w1 = g_w / (1.0 + t)
    w2 = g_w * t / (1.0 + t)
    comb = jnp.where(lane == i1, w1, 0.0) + jnp.where(lane == i2, w2, 0.0)
    return comb, g_idx


META_COLS = LANES


def _route_kernel(x_ref, g_ref, rw_hi_ref, rw_lo_ref, rb_ref, xr_ref, meta_ref, cnt_ref,
                  carry_ref, tri_ref):
    tm, D = x_ref.shape

    @pl.when(pl.program_id(0) == 0)
    def _():
        carry_ref[...] = jnp.zeros(carry_ref.shape, F32)
        r = lax.broadcasted_iota(jnp.int32, (tm, tm), 0)
        c = lax.broadcasted_iota(jnp.int32, (tm, tm), 1)
        tri_ref[...] = jnp.where(c < r, 1.0, 0.0).astype(BF16)

    x = x_ref[...]
    h = x * _rms_scale(x) * g_ref[...]
    comb, g_idx = _router(h, rw_hi_ref, rw_lo_ref, rb_ref)
    xr_ref[:, 0:D] = x
    xr_ref[:, D:D + ROUTER_COLS] = comb

    lane = lax.broadcasted_iota(jnp.int32, (tm, META_COLS), 1)
    onehot = jnp.where(lane == g_idx, 1.0, 0.0)
    earlier = jnp.dot(tri_ref[...], onehot.astype(BF16), preferred_element_type=F32)
    carry = carry_ref[...]
    rank = jnp.sum((earlier + carry) * onehot, axis=-1, keepdims=True).astype(jnp.int32)
    meta_ref[...] = jnp.where(lane == 0, g_idx, jnp.where(lane == 1, rank, 0))
    carry = carry + jnp.sum(onehot, axis=0, keepdims=True)
    carry_ref[...] = carry
    cnt_ref[...] = carry.astype(jnp.int32)


def _route(x2, g, rw_hi, rw_lo, rb, tm):
    T, D = x2.shape
    row = pl.BlockSpec((1, D), lambda i: (0, 0))
    rspec = pl.BlockSpec((D, ROUTER_COLS), lambda i: (0, 0))
    return pl.pallas_call(
        _route_kernel,
        grid=(T // tm,),
        in_specs=[pl.BlockSpec((tm, D), lambda i: (i, 0)), row, rspec, rspec,
                  pl.BlockSpec((1, ROUTER_COLS), lambda i: (0, 0))],
        out_specs=[pl.BlockSpec((tm, D + ROUTER_COLS), lambda i: (i, 0)),
                   pl.BlockSpec((tm, META_COLS), lambda i: (i, 0)),
                   pl.BlockSpec((1, META_COLS), lambda i: (0, 0))],
        out_shape=[jax.ShapeDtypeStruct((T, D + ROUTER_COLS), F32),
                   jax.ShapeDtypeStruct((T, META_COLS), jnp.int32),
                   jax.ShapeDtypeStruct((1, META_COLS), jnp.int32)],
        scratch_shapes=[pltpu.VMEM((1, META_COLS), F32), pltpu.VMEM((tm, tm), BF16)],
        compiler_params=_params(("arbitrary",)),
        name="moe_route",
    )(x2, g, rw_hi, rw_lo, rb)


def _invert_kernel(pos_ref, tok_ref):
    def put(t, carry):
        tok_ref[pos_ref[t]] = t
        return carry
    lax.fori_loop(0, pos_ref.shape[0], put, 0, unroll=16)


def _invert(pos):
    return pl.pallas_call(
        _invert_kernel,
        in_specs=[pl.BlockSpec(memory_space=pltpu.SMEM)],
        out_specs=pl.BlockSpec(memory_space=pltpu.SMEM),
        out_shape=jax.ShapeDtypeStruct(pos.shape, jnp.int32),
        name="moe_invert",
    )(pos)


ITEM_FIRST, ITEM_LAST, ITEM_LIVE, ITEM_FINAL = 1, 2, 4, 8
GATHER_STEPS = EXPERTS_PER_GROUP // 2


def _experts_kernel(tok_ref, itile_ref, igroup_ref, iflags_ref, xr_hbm, g_ref, w1_ref, w3_ref,
                    w2_ref, fg_ref, out_hbm, xbuf, obuf, h_ref, comb_ref, gsem, ssem,
                    *, rows, final_norm):
    w = pl.program_id(0)
    e = pl.program_id(1)
    slot = w % 2
    tile = itile_ref[w]
    osl = tile % 2
    flags = iflags_ref[w]
    live = (flags & ITEM_LIVE) != 0
    D = out_hbm.shape[1]
    part = rows // GATHER_STEPS

    def gather_copy(t, r, sl):
        tok = tok_ref[t * rows + r]
        return pltpu.make_async_copy(xr_hbm.at[pl.ds(tok, 1), :],
                                     xbuf.at[sl, pl.ds(r, 1), :], gsem.at[sl])

    def scatter_copy(t, r, sl):
        tok = tok_ref[t * rows + r]
        return pltpu.make_async_copy(obuf.at[sl, pl.ds(r, 1), :],
                                     out_hbm.at[pl.ds(tok, 1), :], ssem.at[sl])

    def for_rows(fn, n=rows):
        def body(r, carry):
            fn(r)
            return carry
        lax.fori_loop(0, n, body, 0, unroll=8)

    @pl.when(e == 0)
    def _():
        @pl.when(w == 0)
        def _():
            for_rows(lambda r: gather_copy(tile, r, slot).start())
        for_rows(lambda r: gather_copy(tile, r, slot).wait())

        @pl.when((flags & ITEM_FIRST) != 0)
        def _():
            @pl.when(tile >= 2)
            def _():
                for_rows(lambda r: scatter_copy(tile - 2, r, osl).wait())
            x = xbuf[slot, :, 0:D]
            h_ref[...] = (x * _rms_scale(x) * g_ref[...]).astype(BF16)
            comb_ref[...] = xbuf[slot, :, D:D + ROUTER_COLS]
            obuf[osl] = x

    next_tile = itile_ref[w + 1]

    send_prev = jnp.logical_and((flags & ITEM_FIRST) != 0, tile >= 1)

    def expert_step(dma):
        if dma == "gather":
            for r in range(part):
                gather_copy(next_tile, e * part + r, 1 - slot).start()
        elif dma == "scatter":
            for r in range(part):
                scatter_copy(tile - 1, (e - GATHER_STEPS) * part + r, 1 - osl).start()
        h = h_ref[...]
        hid = jax.nn.silu(jnp.dot(h, w1_ref[0], preferred_element_type=F32)) \
            * jnp.dot(h, w3_ref[0], preferred_element_type=F32)
        lane = lax.broadcasted_iota(jnp.int32, (rows, ROUTER_COLS), 1)
        col = N_GROUPS + igroup_ref[w] * EXPERTS_PER_GROUP + e
        c_e = jnp.sum(jnp.where(lane == col, comb_ref[...], 0.0), axis=-1, keepdims=True)
        obuf[osl] += c_e * jnp.dot(hid.astype(BF16), w2_ref[0], preferred_element_type=F32)

    early = e < GATHER_STEPS

    @pl.when(jnp.logical_and(live, early))
    def _():
        expert_step("gather")

    @pl.when(jnp.logical_and(live, jnp.logical_and(jnp.logical_not(early), send_prev)))
    def _():
        expert_step("scatter")

    @pl.when(jnp.logical_and(live, jnp.logical_and(jnp.logical_not(early),
                                                   jnp.logical_not(send_prev))))
    def _():
        expert_step("plain")

    @pl.when(jnp.logical_and(jnp.logical_not(live), early))
    def _():
        for_rows(lambda r: gather_copy(next_tile, e * part + r, 1 - slot).start(), part)

    @pl.when(jnp.logical_and(e == EXPERTS_PER_GROUP - 1, (flags & ITEM_LAST) != 0))
    def _():
        if final_norm:
            y = obuf[osl]
            obuf[osl] = y * _rms_scale(y) * fg_ref[...]

        @pl.when((flags & ITEM_FINAL) != 0)
        def _():
            for_rows(lambda r: scatter_copy(tile, r, osl).start())

            @pl.when(tile >= 1)
            def _():
                for_rows(lambda r: scatter_copy(tile - 1, r, 1 - osl).wait())
            for_rows(lambda r: scatter_copy(tile, r, osl).wait())

    @pl.when(jnp.logical_and(e == EXPERTS_PER_GROUP - 1, w == pl.num_programs(0) - 1))
    def _():
        for_rows(lambda r: gather_copy(next_tile, r, 1 - slot).wait())


def _experts(xr, toks, item_tile, item_group, item_flags, g, w1, w3, w2, final_g, layer, rows,
             final_norm):
    T = xr.shape[0]
    D = D_MODEL
    max_items = item_group.shape[0]

    def expert_block(w, e, tok, itile, igroup, iflags):
        return (layer * N_EXPERTS + igroup[w] * EXPERTS_PER_GROUP + e, 0, 0)

    row = pl.BlockSpec((1, D), lambda w, e, tok, itile, igroup, iflags: (0, 0))
    grid_spec = pltpu.PrefetchScalarGridSpec(
        num_scalar_prefetch=4,
        grid=(max_items, EXPERTS_PER_GROUP),
        in_specs=[
            pl.BlockSpec(memory_space=pl.ANY),
            row,
            pl.BlockSpec((1, D, EXPERT_FF), expert_block),
            pl.BlockSpec((1, D, EXPERT_FF), expert_block),
            pl.BlockSpec((1, EXPERT_FF, D), expert_block),
            row,
        ],
        out_specs=pl.BlockSpec(memory_space=pl.ANY),
        scratch_shapes=[
            pltpu.VMEM((2, rows, D + ROUTER_COLS), F32),
            pltpu.VMEM((2, rows, D), F32),
            pltpu.VMEM((rows, D), BF16),
            pltpu.VMEM((rows, ROUTER_COLS), F32),
            pltpu.SemaphoreType.DMA((2,)),
            pltpu.SemaphoreType.DMA((2,)),
        ],
    )
    return pl.pallas_call(
        functools.partial(_experts_kernel, rows=rows, final_norm=final_norm),
        grid_spec=grid_spec,
        out_shape=jax.ShapeDtypeStruct((T, D), F32),
        compiler_params=_params(("arbitrary", "arbitrary")),
        name="moe_experts",
    )(toks, item_tile, item_group, item_flags, xr, g, w1, w3, w2, final_g)


def _work_items(cnt, rows, n_tiles):
    i32 = jnp.int32
    max_items = n_tiles + N_GROUPS - 1
    ends = jnp.cumsum(cnt)
    starts = ends - cnt
    first_tile = starts // rows
    n_g = jnp.where(cnt > 0, (ends - 1) // rows - first_tile + 1, 0)
    item_end = jnp.cumsum(n_g)
    n_items = item_end[-1]
    w = jnp.arange(max_items + 1, dtype=i32)
    grp = jnp.minimum(jnp.sum(w[:, None] >= item_end[None, :], axis=1), N_GROUPS - 1).astype(i32)
    tile = first_tile[grp] + (w - (item_end - n_g)[grp])
    live = w < n_items
    tile = jnp.where(live, tile, n_tiles - 1).astype(i32)
    prev_tile = jnp.concatenate([jnp.full((1,), -1, i32), tile[:-1]])
    next_tile = jnp.concatenate([tile[1:], jnp.full((1,), -1, i32)])
    final = w == n_items - 1
    first = live & (tile != prev_tile)
    last = live & ((tile != next_tile) | final)
    flags = (first * ITEM_FIRST + last * ITEM_LAST + live * ITEM_LIVE
             + final * ITEM_FINAL).astype(i32)
    return starts, tile, grp[:max_items], flags[:max_items]


def _moe(x2, g, rw_hi, rw_lo, rb, w1, w3, w2, final_g, layer, rows, final_norm):
    T = x2.shape[0]
    xr, meta, counts = _route(x2, g, rw_hi, rw_lo, rb, rows)
    starts, item_tile, item_group, item_flags = _work_items(counts[0, :N_GROUPS], rows, T // rows)
    pos = (starts[meta[:, 0]] + meta[:, 1]).astype(jnp.int32)
    toks = _invert(pos)
    return _experts(xr, toks, item_tile, item_group, item_flags, g, w1, w3, w2, final_g, layer,
                    rows, final_norm)


def _router_weights(rg_w, rg_b, re_w, re_b):
    D = rg_w.shape[0]
    w = jnp.zeros((D, ROUTER_COLS), F32)
    w = w.at[:, :N_GROUPS].set(rg_w).at[:, N_GROUPS:N_GROUPS + N_EXPERTS].set(re_w)
    b = jnp.zeros((1, ROUTER_COLS), F32)
    b = b.at[0, :N_GROUPS].set(rg_b).at[0, N_GROUPS:N_GROUPS + N_EXPERTS].set(re_b)
    hi = w.astype(BF16)
    lo = (w - hi.astype(F32)).astype(BF16)
    return hi, lo, b


class _Tiles(NamedTuple):
    inproj_rows: int
    inproj_cols: int
    sgu_rows: int
    attn_q: int
    attn_k: int
    merge_rows: int
    moe_rows: int


def _tile_plan(n_tokens, seq):
    return _Tiles(inproj_rows=min(1024, n_tokens), inproj_cols=2304,
                  sgu_rows=min(512, n_tokens), attn_q=min(1024, seq), attn_k=min(1024, seq),
                  merge_rows=min(256, n_tokens), moe_rows=min(512, n_tokens))


def kernel(x, norm1_g, w_in, b_gate, sgu_ln_g, sgu_ln_b, sgu_w, sgu_b, lam_q1, lam_k1, lam_q2,
           lam_k2, diff_norm_g, w_proj_a, w_proj_b, w_out, norm2_g, router_g_w, router_g_b,
           router_e_w, router_e_b, w1, w3, w2, final_g):
    B, S, D = x.shape
    assert D == D_MODEL and w_in.shape[2] == IN_COLS
    depth = w_in.shape[0]
    T = B * S
    slopes = jnp.exp2(-8.0 * jnp.arange(1, DIFF_HEADS + 1, dtype=F32) / DIFF_HEADS)
    tiles = _tile_plan(T, S)

    w_in_b = w_in.astype(BF16)
    wa_b, wb_b, wo_b = w_proj_a.astype(BF16), w_proj_b.astype(BF16), w_out.astype(BF16)
    w1_b = w1.astype(BF16).reshape(depth * N_EXPERTS, D, EXPERT_FF)
    w3_b = w3.astype(BF16).reshape(depth * N_EXPERTS, D, EXPERT_FF)
    w2_b = w2.astype(BF16).reshape(depth * N_EXPERTS, EXPERT_FF, D)

    x2 = x.reshape(T, D)
    for l in range(depth):
        lam_init = 0.8 - 0.6 * math.exp(-0.3 * l)
        z = _inproj(x2, norm1_g[l][None], w_in_b, l, tiles.inproj_rows, tiles.inproj_cols)
        a = _sgu(z, sgu_ln_g[l][None], sgu_ln_b[l][None], sgu_w[l].astype(BF16),
                 sgu_b[l][:, :, None], tiles.sgu_rows)
        o = _attention(z, slopes, lam_q1[l][None], lam_k1[l][None], lam_q2[l][None],
                       lam_k2[l][None], diff_norm_g[l][:, None], B, S, tiles.attn_q,
                       tiles.attn_k, lam_init)
        x2 = _merge(x2, a, o, z, b_gate, wa_b, wb_b, wo_b, l, tiles.merge_rows)
        rw_hi, rw_lo, rb = _router_weights(router_g_w[l], router_g_b[l], router_e_w[l],
                                           router_e_b[l])
        x2 = _moe(x2, norm2_g[l][None], rw_hi, rw_lo, rb, w1_b, w3_b, w2_b, final_g[None], l,
                  tiles.moe_rows, final_norm=(l == depth - 1))
    return x2.reshape(B, S, D)
```

```python
import functools
import math
from typing import NamedTuple

import jax
import jax.numpy as jnp
from jax import lax
from jax.experimental import pallas as pl
from jax.experimental.pallas import tpu as pltpu

F32 = jnp.float32
BF16 = jnp.bfloat16

D_MODEL = 2048
SGU_GROUPS = 8
SGU_WIDTH = 1024
SGU_GROUP_DIM = SGU_WIDTH // SGU_GROUPS
CHUNK = 128
DIFF_HEADS = 8
DIFF_HEAD_DIM = 64
DIFF_V_DIM = 2 * DIFF_HEAD_DIM
DIFF_QK_WIDTH = DIFF_HEADS * 2 * DIFF_HEAD_DIM
DIFF_WIDTH = DIFF_HEADS * DIFF_V_DIM
N_BRANCHES = 2
IN_COLS = 2 * SGU_WIDTH + 2 * DIFF_QK_WIDTH + DIFF_WIDTH + N_BRANCHES * D_MODEL
N_GROUPS = 4
EXPERTS_PER_GROUP = 4
N_EXPERTS = N_GROUPS * EXPERTS_PER_GROUP
EXPERT_FF = 512
RMS_EPS = 1e-6
LN_EPS = 1e-5

Q_COL128 = (2 * SGU_WIDTH) // 128
K_COL128 = (2 * SGU_WIDTH + DIFF_QK_WIDTH) // 128
V_COL128 = (2 * SGU_WIDTH + 2 * DIFF_QK_WIDTH) // 128
GATE_COL1024 = (2 * SGU_WIDTH + 2 * DIFF_QK_WIDTH + DIFF_WIDTH) // 1024

LANES = 128
MXU_COLS = 256
ROUTER_COLS = LANES
VMEM_LIMIT = 56 * 1024 * 1024


def _params(semantics):
    return pltpu.CompilerParams(dimension_semantics=semantics, vmem_limit_bytes=VMEM_LIMIT)


def _gelu(x):
    return 0.5 * x * (1.0 + jnp.tanh(0.7978845608028654 * (x + 0.044715 * (x * x * x))))


def _rms_scale(x):
    return lax.rsqrt(jnp.mean(x * x, axis=-1, keepdims=True) + RMS_EPS)


def _inproj_kernel(x_ref, g_ref, w_ref, z_ref, h_ref):
    @pl.when(pl.program_id(1) == 0)
    def _():
        x = x_ref[...]
        h_ref[...] = (x * _rms_scale(x) * g_ref[...]).astype(BF16)

    z_ref[...] = jnp.dot(h_ref[...], w_ref[...], preferred_element_type=F32).astype(BF16)


def _inproj(x2, g, w_bf16, layer, tm, tn):
    T, D = x2.shape
    N = w_bf16.shape[2]
    return pl.pallas_call(
        _inproj_kernel,
        grid=(T // tm, N // tn),
        in_specs=[
            pl.BlockSpec((tm, D), lambda i, j: (i, 0)),
            pl.BlockSpec((1, D), lambda i, j: (0, 0)),
            pl.BlockSpec((None, D, tn), lambda i, j: (layer, 0, j)),
        ],
        out_specs=pl.BlockSpec((tm, tn), lambda i, j: (i, j)),
        out_shape=jax.ShapeDtypeStruct((T, N), BF16),
        scratch_shapes=[pltpu.VMEM((tm, D), BF16)],
        compiler_params=_params(("parallel", "arbitrary")),
        name="inproj",
    )(x2, g, w_bf16)


def _sgu_kernel(u_ref, v_ref, lng_ref, lnb_ref, ws_ref, bs_ref, a_ref, *, chunks):
    v = _gelu(v_ref[...].astype(F32))
    mu = jnp.mean(v, axis=-1, keepdims=True)
    vc = v - mu
    var = jnp.mean(vc * vc, axis=-1, keepdims=True)
    vn = (vc * lax.rsqrt(var + LN_EPS) * lng_ref[...] + lnb_ref[...]).astype(BF16)
    for c in range(chunks):
        rows = slice(c * CHUNK, (c + 1) * CHUNK)
        for g in range(SGU_GROUPS):
            cols = slice(g * SGU_GROUP_DIM, (g + 1) * SGU_GROUP_DIM)
            mixed = jnp.dot(ws_ref[g], vn[rows, cols], preferred_element_type=F32)
            mixed = mixed + bs_ref[g]
            u = _gelu(u_ref[rows, cols].astype(F32))
            a_ref[rows, cols] = (u * mixed).astype(BF16)


def _sgu(z, ln_g, ln_b, ws_bf16, bs_col, tm):
    T = z.shape[0]
    wblk = SGU_WIDTH
    return pl.pallas_call(
        functools.partial(_sgu_kernel, chunks=tm // CHUNK),
        grid=(T // tm,),
        in_specs=[
            pl.BlockSpec((tm, wblk), lambda i: (i, 0)),
            pl.BlockSpec((tm, wblk), lambda i: (i, 1)),
            pl.BlockSpec((1, wblk), lambda i: (0, 0)),
            pl.BlockSpec((1, wblk), lambda i: (0, 0)),
            pl.BlockSpec((SGU_GROUPS, CHUNK, CHUNK), lambda i: (0, 0, 0)),
            pl.BlockSpec((SGU_GROUPS, CHUNK, 1), lambda i: (0, 0, 0)),
        ],
        out_specs=pl.BlockSpec((tm, wblk), lambda i: (i, 0)),
        out_shape=jax.ShapeDtypeStruct((T, SGU_WIDTH), BF16),
        compiler_params=_params(("parallel",)),
        name="sgu",
    )(z, z, ln_g, ln_b, ws_bf16, bs_col)


LOG2E = 1.4426950408889634
VT_PAD = 16
ATTN_LAG = 3


def _split3(x):
    a1 = x.astype(BF16).astype(F32)
    a2 = (x - a1).astype(BF16).astype(F32)
    a3 = (x - a1 - a2).astype(BF16).astype(F32)
    return a1, a2, a3


def _pick(row, values):
    out = jnp.zeros(values[0].shape, F32)
    for i, v in enumerate(values):
        out = jnp.where(row == i, v, out)
    return out


def _attn_kernel(slopes_ref, lq1_ref, lk1_ref, lq2_ref, lk2_ref, dgt_ref, q_ref, k_ref, v_ref,
                 o_ref, qx_ref, kx_ref, vt_ref, rel_ref, s0_ref, s1_ref, ml0_ref, ml1_ref,
                 m_ref, acc_ref, *, tq, tk, seq, lam_init):
    h = pl.program_id(1)
    qi = pl.program_id(2)
    slope2 = slopes_ref[h] * LOG2E
    hd = DIFF_V_DIM
    n = seq // tk
    q0 = qi * tq
    jd = q0 // tk

    @pl.when(qi == 0)
    def _():
        pad_row = lax.broadcasted_iota(jnp.int32, (VT_PAD, tk), 0)
        ones_pad = jnp.where(pad_row == 0, 1.0, 0.0).astype(BF16)

        def transpose_chunk(c, carry):
            sl = pl.ds(pl.multiple_of(c * tk, tk), tk)
            vt_ref[0:hd, sl] = v_ref[sl, :].astype(F32).T.astype(BF16)
            vt_ref[hd:hd + VT_PAD, sl] = ones_pad
            return carry
        lax.fori_loop(0, n, transpose_chunk, 0)

        c_idx = lax.broadcasted_iota(jnp.int32, (tk, hd), 0)
        k_lane = lax.broadcasted_iota(jnp.int32, (tk, hd), 1)
        a = _split3(jnp.full((tk, hd), slope2, F32))
        c_lo = (c_idx & (LANES - 1)).astype(F32)
        c_hi = (c_idx >> 7).astype(F32)
        kx = _pick(k_lane, [-a[0], -a[1], -a[2], -LANES * a[0], -LANES * a[1], -LANES * a[2],
                            c_lo, c_lo, c_lo, c_hi, c_hi, c_hi])
        kx_ref[...] = kx.astype(BF16)

        x_row = lax.broadcasted_iota(jnp.int32, (hd, 2 * tq), 0)
        r_idx = lax.broadcasted_iota(jnp.int32, (hd, 2 * tq), 1)
        r_idx = jnp.where(r_idx >= tq, r_idx - tq, r_idx)
        r_lo = (r_idx & (LANES - 1)).astype(F32)
        r_hi = (r_idx >> 7).astype(F32)
        a = _split3(jnp.full((hd, 2 * tq), slope2, F32))
        ext = _pick(x_row, [r_lo, r_lo, r_lo, r_hi, r_hi, r_hi,
                            a[0], a[1], a[2], LANES * a[0], LANES * a[1], LANES * a[2]])
        for side, sign in ((0, 1.0), (1, -1.0), (2, 0.0)):
            qx_ref[side, hd:2 * hd, :] = (sign * ext).astype(BF16)

        rel_ref[...] = (lax.broadcasted_iota(jnp.int32, (tk, tq), 0)
                        - lax.broadcasted_iota(jnp.int32, (tk, tq), 1)).astype(F32)

    qt = (q_ref[...].astype(F32) * (DIFF_HEAD_DIM ** -0.5 * LOG2E)).T
    d_row = lax.broadcasted_iota(jnp.int32, (hd, tq), 0)
    q_main = jnp.concatenate([jnp.where(d_row < DIFF_HEAD_DIM, qt, 0.0),
                              jnp.where(d_row >= DIFF_HEAD_DIM, qt, 0.0)], axis=1).astype(BF16)
    for side in range(3):
        qx_ref[side, 0:hd, :] = q_main

    m_ref[...] = jnp.full(m_ref.shape, -jnp.inf, F32)
    acc_ref[...] = jnp.zeros(acc_ref.shape, F32)

    n_col = 2 * tq // MXU_COLS

    def scores_tile(c, j, side, s_ref, ml_ref, bias=None):
        cs = slice(c * MXU_COLS, (c + 1) * MXU_COLS)
        rows = pl.ds(pl.multiple_of(j * tk, tk), tk)
        kcx = jnp.concatenate([k_ref[rows, :], kx_ref[...]], axis=1)
        s = jnp.dot(kcx, qx_ref[side, :, cs], preferred_element_type=F32)
        if bias is not None:
            b0 = (c * MXU_COLS) % tq
            s = s + bias[:, b0:b0 + MXU_COLS]
        s_ref[:, cs] = s
        ml_ref[:, cs] = jnp.max(s, axis=0, keepdims=True)

    def accumulate_tile(c, j, cst, s_ref, ml_ref):
        cs = slice(c * MXU_COLS, (c + 1) * MXU_COLS)
        m_prev = m_ref[:, cs]
        m_new = jnp.maximum(m_prev, ml_ref[:, cs] + cst)
        alpha = jnp.exp2(m_prev - m_new)
        p = jnp.exp2(s_ref[:, cs] - (m_new - cst)).astype(BF16)
        keys = pl.ds(pl.multiple_of(j * tk, tk), tk)
        acc_ref[:, cs] = alpha * acc_ref[:, cs] + jnp.dot(vt_ref[:, keys], p,
                                                          preferred_element_type=F32)
        m_ref[:, cs] = m_new

    def run(score_units, acc_units):
        assert len(score_units) == len(acc_units)
        for su, au in zip(score_units, acc_units):
            if su is not None:
                scores_tile(*su)
            if au is not None:
                accumulate_tile(*au)

    def tiles(desc, lo, hi):
        return [(c,) + desc for c in range(lo, hi)]

    def chunk(t):
        side = (t >= jd).astype(jnp.int32)
        j = t + side
        sign = (1 - 2 * side).astype(F32)
        cst = -sign * slope2 * (q0 - j * tk).astype(F32)
        return j, side, cst

    lag = min(ATTN_LAG, n_col)
    diag_bias = -slope2 * jnp.abs(rel_ref[...] + (jd * tk - q0).astype(F32))
    diag_s = (jd, 2, s0_ref, ml0_ref, diag_bias)
    diag_a = (jd, 0.0, s0_ref, ml0_ref)
    if n > 1:
        j0, side0, cst0 = chunk(jnp.int32(0))
        run(tiles(diag_s, 0, n_col) + tiles((j0, side0, s1_ref, ml1_ref, None), 0, n_col),
            [None] * lag + tiles(diag_a, 0, n_col)
            + tiles((j0, cst0, s1_ref, ml1_ref), 0, n_col - lag))

        def pair(i, carry):
            ja, _, ca = chunk(2 * i)
            jb, sb, cb = chunk(2 * i + 1)
            jc, sc, cc = chunk(2 * i + 2)
            run(tiles((jb, sb, s0_ref, ml0_ref, None), 0, n_col)
                + tiles((jc, sc, s1_ref, ml1_ref, None), 0, n_col),
                tiles((ja, ca, s1_ref, ml1_ref), n_col - lag, n_col)
                + tiles((jb, cb, s0_ref, ml0_ref), 0, n_col)
                + tiles((jc, cc, s1_ref, ml1_ref), 0, n_col - lag))
            return carry

        lax.fori_loop(0, (n - 2) // 2, pair, 0)
        jl, _, cl = chunk(jnp.int32(n - 2))
        run([None] * lag, tiles((jl, cl, s1_ref, ml1_ref), n_col - lag, n_col))
    else:
        run(tiles(diag_s, 0, n_col), [None] * lag + tiles(diag_a, 0, n_col - lag))
        run([None] * lag, tiles(diag_a, n_col - lag, n_col))

    lam = (jnp.exp(jnp.sum(lq1_ref[...] * lk1_ref[...], axis=-1, keepdims=True))
           - jnp.exp(jnp.sum(lq2_ref[...] * lk2_ref[...], axis=-1, keepdims=True))
           + lam_init)
    acc = acc_ref[...]
    out = acc[0:hd, :] * (1.0 / acc[hd:hd + 1, :])
    o = out[:, 0:tq] - lam * out[:, tq:2 * tq]
    o = o * lax.rsqrt(jnp.mean(o * o, axis=0, keepdims=True) + RMS_EPS)
    o = o * dgt_ref[...] * (1.0 - lam_init)
    o_ref[...] = o.T.astype(BF16)


def _attention(z, slopes, lq1, lk1, lq2, lk2, dg, batch, seq, tq, tk, lam_init):
    T = z.shape[0]
    nq = seq // tq
    hd = DIFF_V_DIM
    lam_spec = pl.BlockSpec((1, DIFF_HEAD_DIM), lambda b, h, i, s: (0, 0))
    grid_spec = pltpu.PrefetchScalarGridSpec(
        num_scalar_prefetch=1,
        grid=(batch, DIFF_HEADS, nq),
        in_specs=[
            lam_spec, lam_spec, lam_spec, lam_spec,
            pl.BlockSpec((hd, 1), lambda b, h, i, s: (0, 0)),
            pl.BlockSpec((tq, hd), lambda b, h, i, s: (b * nq + i, Q_COL128 + h)),
            pl.BlockSpec((seq, hd), lambda b, h, i, s: (b, K_COL128 + h)),
            pl.BlockSpec((seq, hd), lambda b, h, i, s: (b, V_COL128 + h)),
        ],
        out_specs=pl.BlockSpec((tq, hd), lambda b, h, i, s: (b * nq + i, h)),
        scratch_shapes=[
            pltpu.VMEM((3, 2 * hd, 2 * tq), BF16),
            pltpu.VMEM((tk, hd), BF16),
            pltpu.VMEM((hd + VT_PAD, seq), BF16),
            pltpu.VMEM((tk, tq), F32),
            pltpu.VMEM((tk, 2 * tq), F32),
            pltpu.VMEM((tk, 2 * tq), F32),
            pltpu.VMEM((1, 2 * tq), F32),
            pltpu.VMEM((1, 2 * tq), F32),
            pltpu.VMEM((1, 2 * tq), F32),
            pltpu.VMEM((hd + VT_PAD, 2 * tq), F32),
        ],
    )
    return pl.pallas_call(
        functools.partial(_attn_kernel, tq=tq, tk=tk, seq=seq, lam_init=lam_init),
        grid_spec=grid_spec,
        out_shape=jax.ShapeDtypeStruct((T, DIFF_WIDTH), BF16),
        compiler_params=_params(("parallel", "parallel", "arbitrary")),
        name="diff_attn",
    )(slopes, lq1, lk1, lq2, lk2, dg, z, z, z)


def _merge_kernel(x_ref, a_ref, b_ref, g00_ref, g01_ref, g10_ref, g11_ref, bg_ref,
                  wa_ref, wb_ref, wo_ref, o_ref, merged_ref):
    gate_refs = ((g00_ref, g01_ref), (g10_ref, g11_ref))
    half = D_MODEL // 2
    a = a_ref[...]
    b = b_ref[...]
    for c in range(2):
        cols = slice(c * half, (c + 1) * half)
        ga = jax.nn.sigmoid(gate_refs[0][c][...].astype(F32) + bg_ref[0:1, cols])
        gb = jax.nn.sigmoid(gate_refs[1][c][...].astype(F32) + bg_ref[1:2, cols])
        pa = jnp.dot(a, wa_ref[:, cols], preferred_element_type=F32)
        pb = jnp.dot(b, wb_ref[:, cols], preferred_element_type=F32)
        merged_ref[:, cols] = (ga * pa + gb * pb).astype(BF16)
    o_ref[...] = x_ref[...] + jnp.dot(merged_ref[...], wo_ref[...], preferred_element_type=F32)


def _merge(x2, a, b, z, b_gate, wa, wb, wo, layer, tm):
    T, D = x2.shape
    half = D // 2

    def gate_spec(k):
        return pl.BlockSpec((tm, half), lambda i: (i, GATE_COL1024 + k))

    def const_spec(shape):
        return pl.BlockSpec((None,) + shape, lambda i: (layer, 0, 0),
                            pipeline_mode=pl.Buffered(1))

    return pl.pallas_call(
        _merge_kernel,
        grid=(T // tm,),
        in_specs=[
            pl.BlockSpec((tm, D), lambda i: (i, 0)),
            pl.BlockSpec((tm, SGU_WIDTH), lambda i: (i, 0)),
            pl.BlockSpec((tm, DIFF_WIDTH), lambda i: (i, 0)),
            gate_spec(0), gate_spec(1), gate_spec(2), gate_spec(3),
            const_spec((N_BRANCHES, D)),
            const_spec((SGU_WIDTH, D)),
            const_spec((DIFF_WIDTH, D)),
            const_spec((D, D)),
        ],
        out_specs=pl.BlockSpec((tm, D), lambda i: (i, 0)),
        out_shape=jax.ShapeDtypeStruct((T, D), F32),
        scratch_shapes=[pltpu.VMEM((tm, D), BF16)],
        compiler_params=_params(("parallel",)),
        name="merge_out",
    )(x2, a, b, z, z, z, z, b_gate, wa, wb, wo)


def _split_bf16(x):
    hi = x.astype(BF16)
    lo = (x - hi.astype(F32)).astype(BF16)
    return hi, lo


def _dot3(x_hi, x_lo, w_hi, w_lo):
    return (jnp.dot(x_hi, w_hi, preferred_element_type=F32)
            + jnp.dot(x_lo, w_hi, preferred_element_type=F32)
            + jnp.dot(x_hi, w_lo, preferred_element_type=F32))


def _router(h, rw_hi_ref, rw_lo_ref, rb_ref):
    tm = h.shape[0]
    h_hi, h_lo = _split_bf16(h)
    logits = _dot3(h_hi, h_lo, rw_hi_ref[...], rw_lo_ref[...]) + rb_ref[...]
    lane = lax.broadcasted_iota(jnp.int32, (tm, ROUTER_COLS), 1)
    neg = jnp.float32(-jnp.inf)
    is_group = lane < N_GROUPS
    gl = jnp.where(is_group, logits, neg)
    gmax = jnp.max(gl, axis=-1, keepdims=True)
    gexp = jnp.exp(gl - gmax)
    g_w = 1.0 / jnp.sum(gexp, axis=-1, keepdims=True)
    g_idx = jnp.min(jnp.where(gl == gmax, lane, ROUTER_COLS), axis=-1, keepdims=True)
    e_lo = N_GROUPS + g_idx * EXPERTS_PER_GROUP
    in_group = (lane >= e_lo) & (lane < e_lo + EXPERTS_PER_GROUP)
    el = jnp.where(in_group, logits, neg)
    m1 = jnp.max(el, axis=-1, keepdims=True)
    i1 = jnp.min(jnp.where(el == m1, lane, ROUTER_COLS), axis=-1, keepdims=True)
    el2 = jnp.where(lane == i1, neg, el)
    m2 = jnp.max(el2, axis=-1, keepdims=True)
    i2 = jnp.min(jnp.where(el2 == m2, lane, ROUTER_COLS), axis=-1, keepdims=True)
    t = jnp.exp(m2 - m1)
    w1 = g_w / (1.0 + t)
    w2 = g_w * t / (1.0 + t)
    comb = jnp.where(lane == i1, w1, 0.0) + jnp.where(lane == i2, w2, 0.0)
    return comb, g_idx


META_COLS = LANES


def _route_kernel(x_ref, g_ref, rw_hi_ref, rw_lo_ref, rb_ref, xr_ref, meta_ref, cnt_ref,
                  carry_ref, tri_ref):
    tm, D = x_ref.shape

    @pl.when(pl.program_id(0) == 0)
    def _():
        carry_ref[...] = jnp.zeros(carry_ref.shape, F32)
        r = lax.broadcasted_iota(jnp.int32, (tm, tm), 0)
        c = lax.broadcasted_iota(jnp.int32, (tm, tm), 1)
        tri_ref[...] = jnp.where(c < r, 1.0, 0.0).astype(BF16)

    x = x_ref[...]
    h = x * _rms_scale(x) * g_ref[...]
    comb, g_idx = _router(h, rw_hi_ref, rw_lo_ref, rb_ref)
    xr_ref[:, 0:D] = x
    xr_ref[:, D:D + ROUTER_COLS] = comb

    lane = lax.broadcasted_iota(jnp.int32, (tm, META_COLS), 1)
    onehot = jnp.where(lane == g_idx, 1.0, 0.0)
    earlier = jnp.dot(tri_ref[...], onehot.astype(BF16), preferred_element_type=F32)
    carry = carry_ref[...]
    rank = jnp.sum((earlier + carry) * onehot, axis=-1, keepdims=True).astype(jnp.int32)
    meta_ref[...] = jnp.where(lane == 0, g_idx, jnp.where(lane == 1, rank, 0))
    carry = carry + jnp.sum(onehot, axis=0, keepdims=True)
    carry_ref[...] = carry
    cnt_ref[...] = carry.astype(jnp.int32)


def _route(x2, g, rw_hi, rw_lo, rb, tm):
    T, D = x2.shape
    row = pl.BlockSpec((1, D), lambda i: (0, 0))
    rspec = pl.BlockSpec((D, ROUTER_COLS), lambda i: (0, 0))
    return pl.pallas_call(
        _route_kernel,
        grid=(T // tm,),
        in_specs=[pl.BlockSpec((tm, D), lambda i: (i, 0)), row, rspec, rspec,
                  pl.BlockSpec((1, ROUTER_COLS), lambda i: (0, 0))],
        out_specs=[pl.BlockSpec((tm, D + ROUTER_COLS), lambda i: (i, 0)),
                   pl.BlockSpec((tm, META_COLS), lambda i: (i, 0)),
                   pl.BlockSpec((1, META_COLS), lambda i: (0, 0))],
        out_shape=[jax.ShapeDtypeStruct((T, D + ROUTER_COLS), F32),
                   jax.ShapeDtypeStruct((T, META_COLS), jnp.int32),
                   jax.ShapeDtypeStruct((1, META_COLS), jnp.int32)],
        scratch_shapes=[pltpu.VMEM((1, META_COLS), F32), pltpu.VMEM((tm, tm), BF16)],
        compiler_params=_params(("arbitrary",)),
        name="moe_route",
    )(x2, g, rw_hi, rw_lo, rb)


def _invert_kernel(pos_ref, tok_ref):
    def put(t, carry):
        tok_ref[pos_ref[t]] = t
        return carry
    lax.fori_loop(0, pos_ref.shape[0], put, 0, unroll=16)


def _invert(pos):
    return pl.pallas_call(
        _invert_kernel,
        in_specs=[pl.BlockSpec(memory_space=pltpu.SMEM)],
        out_specs=pl.BlockSpec(memory_space=pltpu.SMEM),
        out_shape=jax.ShapeDtypeStruct(pos.shape, jnp.int32),
        name="moe_invert",
    )(pos)


ITEM_FIRST, ITEM_LAST, ITEM_LIVE, ITEM_FINAL = 1, 2, 4, 8
GATHER_STEPS = EXPERTS_PER_GROUP // 2


def _experts_kernel(tok_ref, itile_ref, igroup_ref, iflags_ref, xr_hbm, g_ref, w1_ref, w3_ref,
                    w2_ref, fg_ref, out_hbm, xbuf, obuf, h_ref, comb_ref, gsem, ssem,
                    *, rows, final_norm):
    w = pl.program_id(0)
    e = pl.program_id(1)
    slot = w % 2
    tile = itile_ref[w]
    osl = tile % 2
    flags = iflags_ref[w]
    live = (flags & ITEM_LIVE) != 0
    D = out_hbm.shape[1]
    part = rows // GATHER_STEPS

    def gather_copy(t, r, sl):
        tok = tok_ref[t * rows + r]
        return pltpu.make_async_copy(xr_hbm.at[pl.ds(tok, 1), :],
                                     xbuf.at[sl, pl.ds(r, 1), :], gsem.at[sl])

    def scatter_copy(t, r, sl):
        tok = tok_ref[t * rows + r]
        return pltpu.make_async_copy(obuf.at[sl, pl.ds(r, 1), :],
                                     out_hbm.at[pl.ds(tok, 1), :], ssem.at[sl])

    def for_rows(fn, n=rows):
        def body(r, carry):
            fn(r)
            return carry
        lax.fori_loop(0, n, body, 0, unroll=8)

    @pl.when(e == 0)
    def _():
        @pl.when(w == 0)
        def _():
            for_rows(lambda r: gather_copy(tile, r, slot).start())
        for_rows(lambda r: gather_copy(tile, r, slot).wait())

        @pl.when((flags & ITEM_FIRST) != 0)
        def _():
            @pl.when(tile >= 2)
            def _():
                for_rows(lambda r: scatter_copy(tile - 2, r, osl).wait())
            x = xbuf[slot, :, 0:D]
            h_ref[...] = (x * _rms_scale(x) * g_ref[...]).astype(BF16)
            comb_ref[...] = xbuf[slot, :, D:D + ROUTER_COLS]
            obuf[osl] = x

    next_tile = itile_ref[w + 1]

    send_prev = jnp.logical_and((flags & ITEM_FIRST) != 0, tile >= 1)

    def expert_step(dma):
        if dma == "gather":
            for r in range(part):
                gather_copy(next_tile, e * part + r, 1 - slot).start()
        elif dma == "scatter":
            for r in range(part):
                scatter_copy(tile - 1, (e - GATHER_STEPS) * part + r, 1 - osl).start()
        h = h_ref[...]
        hid = jax.nn.silu(jnp.dot(h, w1_ref[0], preferred_element_type=F32)) \
            * jnp.dot(h, w3_ref[0], preferred_element_type=F32)
        lane = lax.broadcasted_iota(jnp.int32, (rows, ROUTER_COLS), 1)
        col = N_GROUPS + igroup_ref[w] * EXPERTS_PER_GROUP + e
        c_e = jnp.sum(jnp.where(lane == col, comb_ref[...], 0.0), axis=-1, keepdims=True)
        obuf[osl] += c_e * jnp.dot(hid.astype(BF16), w2_ref[0], preferred_element_type=F32)

    early = e < GATHER_STEPS

    @pl.when(jnp.logical_and(live, early))
    def _():
        expert_step("gather")

    @pl.when(jnp.logical_and(live, jnp.logical_and(jnp.logical_not(early), send_prev)))
    def _():
        expert_step("scatter")

    @pl.when(jnp.logical_and(live, jnp.logical_and(jnp.logical_not(early),
                                                   jnp.logical_not(send_prev))))
    def _():
        expert_step("plain")

    @pl.when(jnp.logical_and(jnp.logical_not(live), early))
    def _():
        for_rows(lambda r: gather_copy(next_tile, e * part + r, 1 - slot).start(), part)

    @pl.when(jnp.logical_and(e == EXPERTS_PER_GROUP - 1, (flags & ITEM_LAST) != 0))
    def _():
        if final_norm:
            y = obuf[osl]
            obuf[osl] = y * _rms_scale(y) * fg_ref[...]

        @pl.when((flags & ITEM_FINAL) != 0)
        def _():
            for_rows(lambda r: scatter_copy(tile, r, osl).start())

            @pl.when(tile >= 1)
            def _():
                for_rows(lambda r: scatter_copy(tile - 1, r, 1 - osl).wait())
            for_rows(lambda r: scatter_copy(tile, r, osl).wait())

    @pl.when(jnp.logical_and(e == EXPERTS_PER_GROUP - 1, w == pl.num_programs(0) - 1))
    def _():
        for_rows(lambda r: gather_copy(next_tile, r, 1 - slot).wait())


def _experts(xr, toks, item_tile, item_group, item_flags, g, w1, w3, w2, final_g, layer, rows,
             final_norm):
    T = xr.shape[0]
    D = D_MODEL
    max_items = item_group.shape[0]

    def expert_block(w, e, tok, itile, igroup, iflags):
        return (layer * N_EXPERTS + igroup[w] * EXPERTS_PER_GROUP + e, 0, 0)

    row = pl.BlockSpec((1, D), lambda w, e, tok, itile, igroup, iflags: (0, 0))
    grid_spec = pltpu.PrefetchScalarGridSpec(
        num_scalar_prefetch=4,
        grid=(max_items, EXPERTS_PER_GROUP),
        in_specs=[
            pl.BlockSpec(memory_space=pl.ANY),
            row,
            pl.BlockSpec((1, D, EXPERT_FF), expert_block),
            pl.BlockSpec((1, D, EXPERT_FF), expert_block),
            pl.BlockSpec((1, EXPERT_FF, D), expert_block),
            row,
        ],
        out_specs=pl.BlockSpec(memory_space=pl.ANY),
        scratch_shapes=[
            pltpu.VMEM((2, rows, D + ROUTER_COLS), F32),
            pltpu.VMEM((2, rows, D), F32),
            pltpu.VMEM((rows, D), BF16),
            pltpu.VMEM((rows, ROUTER_COLS), F32),
            pltpu.SemaphoreType.DMA((2,)),
            pltpu.SemaphoreType.DMA((2,)),
        ],
    )
    return pl.pallas_call(
        functools.partial(_experts_kernel, rows=rows, final_norm=final_norm),
        grid_spec=grid_spec,
        out_shape=jax.ShapeDtypeStruct((T, D), F32),
        compiler_params=_params(("arbitrary", "arbitrary")),
        name="moe_experts",
    )(toks, item_tile, item_group, item_flags, xr, g, w1, w3, w2, final_g)


def _work_items(cnt, rows, n_tiles):
    i32 = jnp.int32
    max_items = n_tiles + N_GROUPS - 1
    ends = jnp.cumsum(cnt)
    starts = ends - cnt
    first_tile = starts // rows
    n_g = jnp.where(cnt > 0, (ends - 1) // rows - first_tile + 1, 0)
    item_end = jnp.cumsum(n_g)
    n_items = item_end[-1]
    w = jnp.arange(max_items + 1, dtype=i32)
    grp = jnp.minimum(jnp.sum(w[:, None] >= item_end[None, :], axis=1), N_GROUPS - 1).astype(i32)
    tile = first_tile[grp] + (w - (item_end - n_g)[grp])
    live = w < n_items
    tile = jnp.where(live, tile, n_tiles - 1).astype(i32)
    prev_tile = jnp.concatenate([jnp.full((1,), -1, i32), tile[:-1]])
    next_tile = jnp.concatenate([tile[1:], jnp.full((1,), -1, i32)])
    final = w == n_items - 1
    first = live & (tile != prev_tile)
    last = live & ((tile != next_tile) | final)
    flags = (first * ITEM_FIRST + last * ITEM_LAST + live * ITEM_LIVE
             + final * ITEM_FINAL).astype(i32)
    return starts, tile, grp[:max_items], flags[:max_items]


def _moe(x2, g, rw_hi, rw_lo, rb, w1, w3, w2, final_g, layer, rows, final_norm):
    T = x2.shape[0]
    xr, meta, counts = _route(x2, g, rw_hi, rw_lo, rb, rows)
    starts, item_tile, item_group, item_flags = _work_items(counts[0, :N_GROUPS], rows, T // rows)
    pos = (starts[meta[:, 0]] + meta[:, 1]).astype(jnp.int32)
    toks = _invert(pos)
    return _experts(xr, toks, item_tile, item_group, item_flags, g, w1, w3, w2, final_g, layer,
                    rows, final_norm)


def _router_weights(rg_w, rg_b, re_w, re_b):
    D = rg_w.shape[0]
    w = jnp.zeros((D, ROUTER_COLS), F32)
    w = w.at[:, :N_GROUPS].set(rg_w).at[:, N_GROUPS:N_GROUPS + N_EXPERTS].set(re_w)
    b = jnp.zeros((1, ROUTER_COLS), F32)
    b = b.at[0, :N_GROUPS].set(rg_b).at[0, N_GROUPS:N_GROUPS + N_EXPERTS].set(re_b)
    hi = w.astype(BF16)
    lo = (w - hi.astype(F32)).astype(BF16)
    return hi, lo, b


class _Tiles(NamedTuple):
    inproj_rows: int
    inproj_cols: int
    sgu_rows: int
    attn_q: int
    attn_k: int
    merge_rows: int
    moe_rows: int


def _tile_plan(n_tokens, seq):
    return _Tiles(inproj_rows=min(1024, n_tokens), inproj_cols=2304,
                  sgu_rows=min(512, n_tokens), attn_q=min(1024, seq), attn_k=min(1024, seq),
                  merge_rows=min(256, n_tokens), moe_rows=min(1024, n_tokens))


def kernel(x, norm1_g, w_in, b_gate, sgu_ln_g, sgu_ln_b, sgu_w, sgu_b, lam_q1, lam_k1, lam_q2,
           lam_k2, diff_norm_g, w_proj_a, w_proj_b, w_out, norm2_g, router_g_w, router_g_b,
           router_e_w, router_e_b, w1, w3, w2, final_g):
    B, S, D = x.shape
    assert D == D_MODEL and w_in.shape[2] == IN_COLS
    depth = w_in.shape[0]
    T = B * S
    slopes = jnp.exp2(-8.0 * jnp.arange(1, DIFF_HEADS + 1, dtype=F32) / DIFF_HEADS)
    tiles = _tile_plan(T, S)

    w_in_b = w_in.astype(BF16)
    wa_b, wb_b, wo_b = w_proj_a.astype(BF16), w_proj_b.astype(BF16), w_out.astype(BF16)
    w1_b = w1.astype(BF16).reshape(depth * N_EXPERTS, D, EXPERT_FF)
    w3_b = w3.astype(BF16).reshape(depth * N_EXPERTS, D, EXPERT_FF)
    w2_b = w2.astype(BF16).reshape(depth * N_EXPERTS, EXPERT_FF, D)

    x2 = x.reshape(T, D)
    for l in range(depth):
        lam_init = 0.8 - 0.6 * math.exp(-0.3 * l)
        z = _inproj(x2, norm1_g[l][None], w_in_b, l, tiles.inproj_rows, tiles.inproj_cols)
        a = _sgu(z, sgu_ln_g[l][None], sgu_ln_b[l][None], sgu_w[l].astype(BF16),
                 sgu_b[l][:, :, None], tiles.sgu_rows)
        o = _attention(z, slopes, lam_q1[l][None], lam_k1[l][None], lam_q2[l][None],
                       lam_k2[l][None], diff_norm_g[l][:, None], B, S, tiles.attn_q,
                       tiles.attn_k, lam_init)
        x2 = _merge(x2, a, o, z, b_gate, wa_b, wb_b, wo_b, l, tiles.merge_rows)
        rw_hi, rw_lo, rb = _router_weights(router_g_w[l], router_g_b[l], router_e_w[l],
                                           router_e_b[l])
        x2 = _moe(x2, norm2_g[l][None], rw_hi, rw_lo, rb, w1_b, w3_b, w2_b, final_g[None], l,
                  tiles.moe_rows, final_norm=(l == depth - 1))
    return x2.reshape(B, S, D)
```
